```python
import jax
import jax.numpy as jnp
from jax import lax
import numpy as np

D_MODEL = 1024
BATCH = 2
SEQ = 8192
DEPTH = 2

CTX_LEN = 256
GRID_W = 64
HEAD_DIM = 64
D_MIX = D_MODEL
GROUP_W = D_MIX // 4
Q_BLOCK = 128
ROPE_THETA = 10000.0
NORM_EPS = 1e-6
D_FF = 2816

MLA_HEADS = 4
MLA_Q_RANK = 256
MLA_KV_RANK = 128
MLA_NOPE = 64
MLA_ROPE = 32
MLA_V = 64
RWKV_HEADS = 4
RWKV_N = 64
RWKV_W_LORA = 32
RWKV_A_LORA = 32
RWKV_G_LORA = 64
RWKV_LNX_EPS = 64e-5
NA_HEADS = 4
NA_WIN_R = 8
NA_WIN_C = 16
GQA_Q_HEADS = 4
GQA_KV_HEADS = 2

MLA_COLS = MLA_Q_RANK + MLA_KV_RANK + MLA_ROPE
RWKV_COLS = 3 * GROUP_W + 2 * RWKV_W_LORA + 2 * RWKV_A_LORA + RWKV_G_LORA
NA_COLS = 3 * GROUP_W
GQA_COLS = (GQA_Q_HEADS + 2 * GQA_KV_HEADS) * HEAD_DIM
N_IN_COLS = MLA_COLS + RWKV_COLS + NA_COLS + GQA_COLS
MIX_SPLITS = [MLA_COLS, MLA_COLS + RWKV_COLS, MLA_COLS + RWKV_COLS + NA_COLS]
RWKV_SPLITS = [GROUP_W, 2 * GROUP_W, 3 * GROUP_W,
               3 * GROUP_W + RWKV_W_LORA, 3 * GROUP_W + 2 * RWKV_W_LORA,
               3 * GROUP_W + 2 * RWKV_W_LORA + RWKV_A_LORA,
               3 * GROUP_W + 2 * RWKV_W_LORA + 2 * RWKV_A_LORA]

kernel_name = 'hybrid_parallel_group_diffusion_trunk'


def rms_norm(x, gain):
    xf = x.astype(jnp.float32)
    y = xf * lax.rsqrt(jnp.mean(xf * xf, axis=-1, keepdims=True) + NORM_EPS)
    return (y * gain.astype(jnp.float32)).astype(x.dtype)


def modulate(x, shift, scale):
    return x * (1 + scale) + shift


def swiglu(h, w_gu, w_down):
    gate, up = jnp.split(h @ w_gu, 2, axis=-1)
    return (jax.nn.silu(gate) * up) @ w_down


def rope_axis(x, pos):
    half = x.shape[-1] // 2
    inv_freq = ROPE_THETA ** (-jnp.arange(half, dtype=jnp.float32) / half)
    ang = pos.astype(jnp.float32)[:, None] * inv_freq[None, :]
    cos = jnp.cos(ang)[:, None, :].astype(x.dtype)
    sin = jnp.sin(ang)[:, None, :].astype(x.dtype)
    x1, x2 = x[..., :half], x[..., half:]
    return jnp.concatenate([x1 * cos - x2 * sin, x1 * sin + x2 * cos], axis=-1)


def rope_2d(x, row, col):
    a = x.shape[-1] // 2
    return jnp.concatenate([rope_axis(x[..., :a], row), rope_axis(x[..., a:], col)], axis=-1)


def blocked_attention(q, k, v, scale):
    B, S, H, dq = q.shape
    Hk, dv = k.shape[2], v.shape[-1]
    G = H // Hk
    nb = S // Q_BLOCK
    qb = q.reshape(B, nb, Q_BLOCK, Hk, G, dq).transpose(1, 0, 2, 3, 4, 5)

    def one_block(q_blk):
        s = jnp.einsum('bqkgd,btkd->bkgqt', q_blk, k, preferred_element_type=jnp.float32) * scale
        p = jax.nn.softmax(s, axis=-1).astype(v.dtype)
        return jnp.einsum('bkgqt,btkd->bqkgd', p, v)

    o = lax.map(one_block, qb)
    return o.transpose(1, 0, 2, 3, 4, 5).reshape(B, S, H, dv)


def neighbourhood_attention(q, k, v, k_ctx, v_ctx, rpb, scale):
    B, S, H, dh = q.shape
    rows = S // GRID_W
    win_r = min(NA_WIN_R, rows)
    n_nb = win_r * NA_WIN_C
    qg = q.reshape(B, rows, GRID_W, H, dh).transpose(1, 0, 2, 3, 4)
    kg = k.reshape(B, rows, GRID_W, H, dh)
    vg = v.reshape(B, rows, GRID_W, H, dh)
    cols = jnp.arange(GRID_W)
    c0 = jnp.clip(cols - NA_WIN_C // 2, 0, GRID_W - NA_WIN_C)
    col_idx = c0[:, None] + jnp.arange(NA_WIN_C)[None, :]
    dc_idx = col_idx - cols[:, None] + NA_WIN_C - 1

    def one_row(args):
        r, q_row = args
        r0 = jnp.clip(r - win_r // 2, 0, rows - win_r)
        k_nb = lax.dynamic_slice_in_dim(kg, r0, win_r, axis=1)[:, :, col_idx]
        v_nb = lax.dynamic_slice_in_dim(vg, r0, win_r, axis=1)[:, :, col_idx]
        dr_idx = r0 + jnp.arange(win_r) - r + NA_WIN_R - 1
        bias = rpb[:, dr_idx[:, None, None], dc_idx[None, :, :]].transpose(0, 2, 1, 3)
        s_nb = jnp.einsum('bchd,bicjhd->bhcij', q_row, k_nb,
                          preferred_element_type=jnp.float32) * scale + bias.astype(jnp.float32)
        s_ctx = jnp.einsum('bchd,blhd->bhcl', q_row, k_ctx, preferred_element_type=jnp.float32) * scale
        p = jax.nn.softmax(jnp.concatenate([s_nb.reshape(B, H, GRID_W, n_nb), s_ctx], axis=-1),
                           axis=-1).astype(v.dtype)
        p_nb = p[..., :n_nb].reshape(B, H, GRID_W, win_r, NA_WIN_C)
        return (jnp.einsum('bhcij,bicjhd->bchd', p_nb, v_nb)
                + jnp.einsum('bhcl,blhd->bchd', p[..., n_nb:], v_ctx))

    o = lax.map(one_row, (jnp.arange(rows), qg))
    return o.transpose(1, 0, 2, 3, 4).reshape(B, S, H, dh)


def mla_mixer(u, uc, row, col, q_norm, kv_norm, w_uq, w_ukv, qn, kn, with_ctx_out):
    def project(t, pos):
        B, T, _ = t.shape
        q_c, kv_c, k_r = jnp.split(t, [MLA_Q_RANK, MLA_Q_RANK + MLA_KV_RANK], axis=-1)
        q = (rms_norm(q_c, q_norm) @ w_uq).reshape(B, T, MLA_HEADS, MLA_NOPE + MLA_ROPE)
        kv = (rms_norm(kv_c, kv_norm) @ w_ukv).reshape(B, T, MLA_HEADS, MLA_NOPE + MLA_V)
        k_rope = jnp.broadcast_to(k_r[:, :, None, :], (B, T, MLA_HEADS, MLA_ROPE))
        k = jnp.concatenate([kv[..., :MLA_NOPE], k_rope], axis=-1)
        v = kv[..., MLA_NOPE:]
        q, k = rms_norm(q, qn), rms_norm(k, kn)
        if pos is not None:
            q = jnp.concatenate([q[..., :MLA_NOPE], rope_2d(q[..., MLA_NOPE:], *pos)], axis=-1)
            k = jnp.concatenate([k[..., :MLA_NOPE], rope_2d(k[..., MLA_NOPE:], *pos)], axis=-1)
        return q, k, v

    B, S, _ = u.shape
    scale = (MLA_NOPE + MLA_ROPE) ** -0.5
    q, k, v = project(u, (row, col))
    qc, kc, vc = project(uc, None)
    o = blocked_attention(q, jnp.concatenate([kc, k], axis=1), jnp.concatenate([vc, v], axis=1), scale)
    o_ctx = blocked_attention(qc, kc, vc, scale).reshape(B, -1, GROUP_W) if with_ctx_out else None
    return o.reshape(B, S, GROUP_W), o_ctx


def token_shift(t, mu):
    prev = jnp.pad(t, ((0, 0), (1, 0), (0, 0)))[:, :-1]
    nxt = jnp.pad(t, ((0, 0), (0, 1), (0, 0)))[:, 1:]
    return t + mu[0] * (prev - t) + mu[1] * (nxt - t)


def rwkv_features(t, shift, w0, w2, a0, a2, g2, k_k, k_a):
    B, T, _ = t.shape
    t = token_shift(t, shift).astype(jnp.float32)
    r, k, v, wd_f, wd_b, ad_f, ad_b, gd = jnp.split(t, RWKV_SPLITS, axis=-1)
    heads = lambda z: z.reshape(B, T, RWKV_HEADS, RWKV_N)
    kk = heads(k * k_k)
    kk = kk * lax.rsqrt(jnp.sum(kk * kk, axis=-1, keepdims=True) + 1e-12)
    g = jax.nn.sigmoid(gd) @ g2
    dirs = []
    for d, (wd, ad) in enumerate(((wd_f, ad_f), (wd_b, ad_b))):
        w_log = -jax.nn.softplus(-(w0[d] + jnp.tanh(wd) @ w2[d])) - 0.5
        decay = jnp.exp(-jnp.exp(w_log))
        a = jax.nn.sigmoid(a0[d] + ad @ a2[d])
        kd = k * (1 + (a - 1) * k_a)
        dirs.append((heads(decay), heads(kd), heads(a)))
    return heads(r), heads(v), kk, g, dirs


def wkv_scan(state0, r, w, k, v, kk, a, reverse):
    xs = tuple(jnp.moveaxis(z, 1, 0) for z in (r, w, k, v, kk, a))

    def step(S, inp):
        r_t, w_t, k_t, v_t, kk_t, a_t = inp
        sa = jnp.einsum('bhij,bhj->bhi', S, kk_t)
        S = (S * w_t[:, :, None, :] - sa[..., None] * (kk_t * a_t)[:, :, None, :]
             + v_t[..., None] * k_t[:, :, None, :])
        return S, jnp.einsum('bhij,bhj->bhi', S, r_t)

    s_final, ys = lax.scan(step, state0, xs, reverse=reverse)
    return s_final, jnp.moveaxis(ys, 0, 1)


def head_group_norm(y, gain, bias):
    B, T, H, N = y.shape
    mu = jnp.mean(y, axis=-1, keepdims=True)
    var = jnp.mean(jnp.square(y - mu), axis=-1, keepdims=True)
    return ((y - mu) * lax.rsqrt(var + RWKV_LNX_EPS)).reshape(B, T, H * N) * gain + bias


def rwkv_mixer(u, uc, shift, w0, w2, a0, a2, g2, k_k, k_a, r_k, lnx_g, lnx_b, with_ctx_out):
    B, S, _ = u.shape
    r, v, kk, g, dirs = rwkv_features(u, shift, w0, w2, a0, a2, g2, k_k, k_a)
    rc, vc, kkc, gc, dirs_c = rwkv_features(uc, shift, w0, w2, a0, a2, g2, k_k, k_a)
    ys, ys_c, bon, bon_c = [], [], [], []
    for (dec, kd, a), (dec_c, kd_c, a_c), rev in zip(dirs, dirs_c, (False, True)):
        s0 = jnp.zeros((B, RWKV_HEADS, RWKV_N, RWKV_N), jnp.float32)
        s_ctx, y_c = wkv_scan(s0, rc, dec_c, kd_c, vc, kkc, a_c, rev)
        _, y = wkv_scan(s_ctx, r, dec, kd, v, kk, a, rev)
        ys.append(y)
        ys_c.append(y_c)
        bon.append(jnp.sum(r * kd * r_k, axis=-1, keepdims=True) * v)
        bon_c.append(jnp.sum(rc * kd_c * r_k, axis=-1, keepdims=True) * vc)
    o = (head_group_norm(ys[0] + ys[1], lnx_g, lnx_b) + (bon[0] + bon[1]).reshape(B, S, GROUP_W)) * g
    o_ctx = None
    if with_ctx_out:
        o_ctx = ((head_group_norm(ys_c[0] + ys_c[1], lnx_g, lnx_b)
                  + (bon_c[0] + bon_c[1]).reshape(B, -1, GROUP_W)) * gc).astype(uc.dtype)
    return o.astype(u.dtype), o_ctx


def na_mixer(u, uc, qn, kn, rpb, with_ctx_out):
    def heads(t):
        B, T, _ = t.shape
        q, k, v = (z.reshape(B, T, NA_HEADS, HEAD_DIM) for z in jnp.split(t, 3, axis=-1))
        return rms_norm(q, qn), rms_norm(k, kn), v

    B, S, _ = u.shape
    scale = HEAD_DIM ** -0.5
    q, k, v = heads(u)
    qc, kc, vc = heads(uc)
    o = neighbourhood_attention(q, k, v, kc, vc, rpb, scale)
    o_ctx = blocked_attention(qc, kc, vc, scale).reshape(B, -1, GROUP_W) if with_ctx_out else None
    return o.reshape(B, S, GROUP_W), o_ctx


def gqa_mixer(u, uc, row, col, qn, kn, with_ctx_out):
    def heads(t, pos):
        B, T, _ = t.shape
        q, k, v = jnp.split(t, [GQA_Q_HEADS * HEAD_DIM, (GQA_Q_HEADS + GQA_KV_HEADS) * HEAD_DIM], axis=-1)
        q = rms_norm(q.reshape(B, T, GQA_Q_HEADS, HEAD_DIM), qn)
        k = rms_norm(k.reshape(B, T, GQA_KV_HEADS, HEAD_DIM), kn)
        v = v.reshape(B, T, GQA_KV_HEADS, HEAD_DIM)
        if pos is not None:
            q, k = rope_2d(q, *pos), rope_2d(k, *pos)
        return q, k, v

    B, S, _ = u.shape
    scale = HEAD_DIM ** -0.5
    q, k, v = heads(u, (row, col))
    qc, kc, vc = heads(uc, None)
    o = blocked_attention(q, jnp.concatenate([kc, k], axis=1), jnp.concatenate([vc, v], axis=1), scale)
    o_ctx = blocked_attention(qc, kc, vc, scale).reshape(B, -1, GROUP_W) if with_ctx_out else None
    return o.reshape(B, S, GROUP_W), o_ctx


def setup_inputs(seed: int = 0) -> dict:
    key = jax.random.key(seed)
    ks = iter(jax.random.split(key, 64))
    nrm = lambda shape, s: jax.random.normal(next(ks), shape, jnp.float32) * s
    gain = lambda shape: 1.0 + nrm(shape, 0.1)
    L, D = DEPTH, D_MODEL
    return {
        'x': nrm((BATCH, SEQ, D), 1.0),
        'c': nrm((BATCH, D), 1.0),
        'ctx': nrm((BATCH, CTX_LEN, D), 1.0),
        'c_ctx': nrm((D,), 1.0),
        'w_mod': nrm((L, D, 9 * D), 0.5 * D ** -0.5),
        'b_mod': nrm((L, 9 * D), 0.02),
        'norm_ffn1': gain((L, D)),
        'ffn1_w_gu': nrm((L, D, 2 * D_FF), D ** -0.5),
        'ffn1_w_down': nrm((L, D_FF, D), D_FF ** -0.5),
        'norm_mix': gain((L, D)),
        'w_in': nrm((L, D, N_IN_COLS), D ** -0.5),
        'w_out': nrm((L, D_MIX, D), D_MIX ** -0.5),
        'mla_q_norm': gain((L, MLA_Q_RANK)),
        'mla_kv_norm': gain((L, MLA_KV_RANK)),
        'mla_w_uq': nrm((L, MLA_Q_RANK, MLA_HEADS * (MLA_NOPE + MLA_ROPE)), MLA_Q_RANK ** -0.5),
        'mla_w_ukv': nrm((L, MLA_KV_RANK, MLA_HEADS * (MLA_NOPE + MLA_V)), MLA_KV_RANK ** -0.5),
        'mla_qn': gain((L, MLA_NOPE + MLA_ROPE)),
        'mla_kn': gain((L, MLA_NOPE + MLA_ROPE)),
        'rwkv_shift': jax.random.uniform(next(ks), (L, 2, RWKV_COLS), jnp.float32, 0.0, 0.5),
        'rwkv_w0': jax.random.uniform(next(ks), (L, 2, GROUP_W), jnp.float32, -4.0, 1.0),
        'rwkv_w2': nrm((L, 2, RWKV_W_LORA, GROUP_W), 0.5 * RWKV_W_LORA ** -0.5),
        'rwkv_a0': nrm((L, 2, GROUP_W), 0.5),
        'rwkv_a2': nrm((L, 2, RWKV_A_LORA, GROUP_W), 0.5 * RWKV_A_LORA ** -0.5),
        'rwkv_g2': nrm((L, RWKV_G_LORA, GROUP_W), RWKV_G_LORA ** -0.5),
        'rwkv_k_k': 0.85 + nrm((L, GROUP_W), 0.1),
        'rwkv_k_a': 1.0 + nrm((L, GROUP_W), 0.1),
        'rwkv_r_k': nrm((L, RWKV_HEADS, RWKV_N), 0.1),
        'rwkv_lnx_g': gain((L, GROUP_W)),
        'rwkv_lnx_b': nrm((L, GROUP_W), 0.02),
        'na_qn': gain((L, HEAD_DIM)),
        'na_kn': gain((L, HEAD_DIM)),
        'na_rpb': nrm((L, NA_HEADS, 2 * NA_WIN_R - 1, 2 * NA_WIN_C - 1), 0.5),
        'gqa_qn': gain((L, HEAD_DIM)),
        'gqa_kn': gain((L, HEAD_DIM)),
        'norm_ffn2': gain((L, D)),
        'ffn2_w_gu': nrm((L, D, 2 * D_FF), D ** -0.5),
        'ffn2_w_down': nrm((L, D_FF, D), D_FF ** -0.5),
    }


def reference(x, c, ctx, c_ctx, w_mod, b_mod, norm_ffn1, ffn1_w_gu, ffn1_w_down, norm_mix, w_in, w_out,
              mla_q_norm, mla_kv_norm, mla_w_uq, mla_w_ukv, mla_qn, mla_kn,
              rwkv_shift, rwkv_w0, rwkv_w2, rwkv_a0, rwkv_a2, rwkv_g2, rwkv_k_k, rwkv_k_a, rwkv_r_k,
              rwkv_lnx_g, rwkv_lnx_b, na_qn, na_kn, na_rpb, gqa_qn, gqa_kn,
              norm_ffn2, ffn2_w_gu, ffn2_w_down):
    S = x.shape[1]
    t = jnp.arange(S)
    row, col = t // GRID_W, t % GRID_W
    h, hc = x, ctx
    for l in range(DEPTH):
        ctx_needed = l < DEPTH - 1
        m = jnp.split((jax.nn.silu(c) @ w_mod[l] + b_mod[l])[:, None, :], 9, axis=-1)
        mc = jnp.split(jax.nn.silu(c_ctx) @ w_mod[l] + b_mod[l], 9, axis=-1)

        h = h + 0.5 * m[2] * swiglu(modulate(rms_norm(h, norm_ffn1[l]), m[0], m[1]), ffn1_w_gu[l], ffn1_w_down[l])
        hc = hc + 0.5 * mc[2] * swiglu(modulate(rms_norm(hc, norm_ffn1[l]), mc[0], mc[1]),
                                       ffn1_w_gu[l], ffn1_w_down[l])

        u = modulate(rms_norm(h, norm_mix[l]), m[3], m[4]) @ w_in[l]
        uc = modulate(rms_norm(hc, norm_mix[l]), mc[3], mc[4]) @ w_in[l]
        ua, ub, ucn, ud = jnp.split(u, MIX_SPLITS, axis=-1)
        uca, ucb, uccn, ucd = jnp.split(uc, MIX_SPLITS, axis=-1)
        oa, oca = mla_mixer(ua, uca, row, col, mla_q_norm[l], mla_kv_norm[l], mla_w_uq[l], mla_w_ukv[l],
                            mla_qn[l], mla_kn[l], ctx_needed)
        ob, ocb = rwkv_mixer(ub, ucb, rwkv_shift[l], rwkv_w0[l], rwkv_w2[l], rwkv_a0[l], rwkv_a2[l],
                             rwkv_g2[l], rwkv_k_k[l], rwkv_k_a[l], rwkv_r_k[l], rwkv_lnx_g[l], rwkv_lnx_b[l],
                             ctx_needed)
        on, ocn = na_mixer(ucn, uccn, na_qn[l], na_kn[l], na_rpb[l], ctx_needed)
        od, ocd = gqa_mixer(ud, ucd, row, col, gqa_qn[l], gqa_kn[l], ctx_needed)
        mix = jnp.concatenate([oa, ob, on, od], axis=-1).astype(h.dtype) @ w_out[l]
        h = h + m[5] * mix
        if ctx_needed:
            mix_c = jnp.concatenate([oca, ocb, ocn, ocd], axis=-1).astype(hc.dtype) @ w_out[l]
            hc = hc + mc[5] * mix_c

        h = h + 0.5 * m[8] * swiglu(modulate(rms_norm(h, norm_ffn2[l]), m[6], m[7]), ffn2_w_gu[l], ffn2_w_down[l])
        if ctx_needed:
            hc = hc + 0.5 * mc[8] * swiglu(modulate(rms_norm(hc, norm_ffn2[l]), mc[6], mc[7]),
                                           ffn2_w_gu[l], ffn2_w_down[l])
    return h
```

```python
import functools

import numpy as np
import jax
import jax.numpy as jnp
from jax import lax
from jax.experimental import pallas as pl
from jax.experimental.pallas import tpu as pltpu

F32 = jnp.float32
BF16 = jnp.bfloat16

GRID_W = 64
HEAD_DIM = 64
GROUP_W = 256
ROPE_THETA = 10000.0
NORM_EPS = 1e-6
D_FF = 2816
MLA_HEADS = 4
MLA_Q_RANK = 256
MLA_KV_RANK = 128
MLA_NOPE = 64
MLA_ROPE = 32
MLA_V = 64
MLA_QK = MLA_NOPE + MLA_ROPE
RWKV_COLS = 960
RWKV_LNX_EPS = 64e-5
NA_WIN_R = 8
NA_WIN_C = 16
MLA_COLS = 416
NA_COLS = 768
GQA_COLS = 512

LANES = 128
SUBLANES = 8
VMEM_LIMIT_BYTES = 56 * 1024 * 1024

UA_W = 512
UB_W = 1024
UN_W = 768
UD_W = 512

CHUNK = 64
NEG_INF = -1e30


def _dot(a, b):
    return jnp.dot(a, b, preferred_element_type=F32)


def _dot_nt(a, b):
    return lax.dot_general(a, b, (((1,), (1,)), ((), ())), preferred_element_type=F32)


def _dot_tn(a, b):
    return lax.dot_general(a, b, (((0,), (0,)), ((), ())), preferred_element_type=F32)


def _split2(x):
    hi = x.astype(BF16)
    lo = (x - hi.astype(F32)).astype(BF16)
    return hi, lo


def _dot_exact_rhs(x, m):
    h1 = x.astype(BF16)
    r1 = x - h1.astype(F32)
    h2 = r1.astype(BF16)
    h3 = (r1 - h2.astype(F32)).astype(BF16)
    return _dot(h1, m) + _dot(h2, m) + _dot(h3, m)


def _dot_exact_lhs(m, x):
    h1 = x.astype(BF16)
    r1 = x - h1.astype(F32)
    h2 = r1.astype(BF16)
    h3 = (r1 - h2.astype(F32)).astype(BF16)
    return _dot(m, h1) + _dot(m, h2) + _dot(m, h3)


def _rms(x, eps=NORM_EPS):
    return x * lax.rsqrt(jnp.mean(x * x, axis=-1, keepdims=True) + eps)


def _cparams(sem):
    return pltpu.CompilerParams(dimension_semantics=sem, vmem_limit_bytes=VMEM_LIMIT_BYTES)


def _mod_kernel(c_ref, w_ref, b_ref, o_ref):
    c = c_ref[...]
    s = c * jax.nn.sigmoid(c)
    o_ref[...] = jnp.dot(s, w_ref[...], preferred_element_type=F32,
                         precision=lax.Precision.HIGHEST) + b_ref[...]


def _modulation(cvec, w_mod, b_mod):
    L, D, N = w_mod.shape
    tn = 1024
    return pl.pallas_call(
        _mod_kernel,
        grid=(L, N // tn),
        in_specs=[
            pl.BlockSpec((SUBLANES, D), lambda l, n: (0, 0)),
            pl.BlockSpec((None, D, tn), lambda l, n: (l, 0, n)),
            pl.BlockSpec((None, 1, tn), lambda l, n: (l, 0, n)),
        ],
        out_specs=pl.BlockSpec((None, SUBLANES, tn), lambda l, n: (l, 0, n)),
        out_shape=jax.ShapeDtypeStruct((L, SUBLANES, N), F32),
        compiler_params=_cparams(("parallel", "parallel")),
        name="adaln_mod",
    )(cvec, w_mod, b_mod.reshape(L, 1, N))


def _ffn_kernel(h_ref, mod_ref, g_ref, wg_ref, wu_ref, wd_ref, o_ref, xn_ref, acc_ref, *, base, nf):
    f = pl.program_id(2)

    @pl.when(f == 0)
    def _():
        y = _rms(h_ref[...]) * g_ref[...]
        xm = y * (1.0 + mod_ref[base + 1:base + 2, :]) + mod_ref[base:base + 1, :]
        xn_ref[...] = xm.astype(BF16)
        acc_ref[...] = jnp.zeros_like(acc_ref)

    xn = xn_ref[...]
    g = _dot(xn, wg_ref[...])
    u = _dot(xn, wu_ref[...])
    a = (g * jax.nn.sigmoid(g) * u).astype(BF16)
    acc_ref[...] += _dot(a, wd_ref[...])

    @pl.when(f == nf - 1)
    def _():
        o_ref[...] = h_ref[...] + 0.5 * mod_ref[base + 2:base + 3, :] * acc_ref[...]


def _ffn(h, mod, gain, w_gu, w_down, base):
    B, T, D = h.shape
    F = w_down.shape[0]
    tm = min(T, 512)
    tf = F // 2
    nf = F // tf
    return pl.pallas_call(
        functools.partial(_ffn_kernel, base=base, nf=nf),
        grid=(B, T // tm, nf),
        in_specs=[
            pl.BlockSpec((None, tm, D), lambda b, i, f: (b, i, 0)),
            pl.BlockSpec((None, 9, D), lambda b, i, f: (b, 0, 0)),
            pl.BlockSpec((1, D), lambda b, i, f: (0, 0)),
            pl.BlockSpec((D, tf), lambda b, i, f: (0, f)),
            pl.BlockSpec((D, tf), lambda b, i, f: (0, nf + f)),
            pl.BlockSpec((tf, D), lambda b, i, f: (f, 0)),
        ],
        out_specs=pl.BlockSpec((None, tm, D), lambda b, i, f: (b, i, 0)),
        out_shape=jax.ShapeDtypeStruct((B, T, D), F32),
        scratch_shapes=[pltpu.VMEM((tm, D), BF16), pltpu.VMEM((tm, D), F32)],
        compiler_params=_cparams(("parallel", "parallel", "arbitrary")),
        name="swiglu_halfstep",
    )(h, mod, gain.reshape(1, D), w_gu, w_gu, w_down)


def _inproj_kernel(h_ref, mod_ref, g_ref, w_ref, oa_ref, ob_ref, on_ref, od_ref):
    y = _rms(h_ref[...]) * g_ref[...]
    xm = (y * (1.0 + mod_ref[4:5, :]) + mod_ref[3:4, :]).astype(BF16)
    o = 0
    for ref, w in ((oa_ref, UA_W), (ob_ref, UB_W), (on_ref, UN_W), (od_ref, UD_W)):
        ref[...] = _dot(xm, w_ref[:, o:o + w])
        o += w


def _inproj(h, mod, gain, w_in_p):
    B, T, D = h.shape
    tm = min(T, 512)
    W = w_in_p.shape[1]
    widths = (UA_W, UB_W, UN_W, UD_W)
    return pl.pallas_call(
        _inproj_kernel,
        grid=(B, T // tm),
        in_specs=[
            pl.BlockSpec((None, tm, D), lambda b, i: (b, i, 0)),
            pl.BlockSpec((None, 9, D), lambda b, i: (b, 0, 0)),
            pl.BlockSpec((1, D), lambda b, i: (0, 0)),
            pl.BlockSpec((D, W), lambda b, i: (0, 0)),
        ],
        out_specs=[pl.BlockSpec((None, tm, w), lambda b, i: (b, i, 0)) for w in widths],
        out_shape=[jax.ShapeDtypeStruct((B, T, w), F32) for w in widths],
        compiler_params=_cparams(("parallel", "parallel")),
        name="in_projection",
    )(h, mod, gain.reshape(1, D), w_in_p)


def _outproj_kernel(h_ref, mod_ref, oa_ref, ob_ref, on_ref, od_ref, w_ref, o_ref):
    acc = _dot(oa_ref[...], w_ref[0:GROUP_W, :])
    acc += _dot(ob_ref[...], w_ref[GROUP_W:2 * GROUP_W, :])
    acc += _dot(on_ref[...], w_ref[2 * GROUP_W:3 * GROUP_W, :])
    acc += _dot(od_ref[...], w_ref[3 * GROUP_W:4 * GROUP_W, :])
    o_ref[...] = h_ref[...] + mod_ref[5:6, :] * acc


def _outproj(h, mod, oa, ob, on, od, w_out):
    B, T, D = h.shape
    tm = min(T, 512)
    grp = pl.BlockSpec((None, tm, GROUP_W), lambda b, i: (b, i, 0))
    return pl.pallas_call(
        _outproj_kernel,
        grid=(B, T // tm),
        in_specs=[
            pl.BlockSpec((None, tm, D), lambda b, i: (b, i, 0)),
            pl.BlockSpec((None, 9, D), lambda b, i: (b, 0, 0)),
            grp, grp, grp, grp,
            pl.BlockSpec((4 * GROUP_W, D), lambda b, i: (0, 0)),
        ],
        out_specs=pl.BlockSpec((None, tm, D), lambda b, i: (b, i, 0)),
        out_shape=jax.ShapeDtypeStruct((B, T, D), F32),
        compiler_params=_cparams(("parallel", "parallel")),
        name="out_projection",
    )(h, mod, oa, ob, on, od, w_out)


def _rope_tables(positions, width, group, n_rep):
    half = group // 2
    inv_freq = ROPE_THETA ** (-jnp.arange(half, dtype=F32) / half)

    def axis(pos):
        ang = pos.astype(F32)[:, None] * inv_freq[None, :]
        c, s = jnp.cos(ang), jnp.sin(ang)
        return (jnp.concatenate([c, c], -1), jnp.concatenate([jnp.zeros_like(s), s], -1),
                jnp.concatenate([-s, jnp.zeros_like(s)], -1))

    row, col = positions
    parts = [jnp.concatenate([a, b], -1) for a, b in zip(axis(row), axis(col))]
    return [jnp.tile(p, (1, n_rep)) for p in parts]


def _pad_lanes(t, lo, total, fill):
    T = t.shape[0]
    return jnp.concatenate([jnp.full((T, lo), fill, F32), t,
                            jnp.full((T, total - lo - t.shape[1]), fill, F32)], -1)


def _mla_prep_kernel(ua_ref, c_ref, s1_ref, s2_ref, qnorm_ref, kvnorm_ref, qn_ref, kn_ref,
                     wuq_ref, wuk_ref, wuv_ref, q_out, k_out, v_out):
    ua = ua_ref[...]
    qc = (_rms(ua[:, :MLA_Q_RANK]) * qnorm_ref[...]).astype(BF16)
    kvc = (_rms(ua[:, MLA_Q_RANK:MLA_Q_RANK + MLA_KV_RANK]) * kvnorm_ref[...]).astype(BF16)
    k_rope = ua[:, 3 * LANES:4 * LANES]
    q_all = _dot(qc, wuq_ref[...])
    k_all = _dot(kvc, wuk_ref[...])
    v_all = _dot(kvc, wuv_ref[...])
    cos, s_dn, s_up = c_ref[...], s1_ref[...], s2_ref[...]
    half = MLA_ROPE // 4

    def rope(x):
        return x * cos + pltpu.roll(x, half, 1) * s_dn + pltpu.roll(x, LANES - half, 1) * s_up

    def headnorm(x, gain):
        ms = jnp.sum(x * x, axis=-1, keepdims=True) * (1.0 / MLA_QK)
        return x * lax.rsqrt(ms + NORM_EPS) * gain

    for h in range(MLA_HEADS):
        qh = headnorm(q_all[:, h * LANES:(h + 1) * LANES], qn_ref[...])
        q_out[h] = rope(qh).astype(BF16)
        kh = headnorm(k_all[:, h * LANES:(h + 1) * LANES] + k_rope, kn_ref[...])
        k_out[h] = rope(kh).astype(BF16)
        v_out[h] = v_all[:, h * MLA_V:(h + 1) * MLA_V].astype(BF16)


def _mla_prep(ua, tables, q_norm, kv_norm, qn_p, kn_p, wuq_p, wuk_p, wuv):
    B, T, _ = ua.shape
    tm = min(T, 512)
    tab = pl.BlockSpec((tm, LANES), lambda b, i: (i, 0))
    full = lambda a: pl.BlockSpec(a.shape, lambda b, i: (0,) * a.ndim)
    small = [q_norm, kv_norm, qn_p, kn_p, wuq_p, wuk_p, wuv]
    return pl.pallas_call(
        _mla_prep_kernel,
        grid=(B, T // tm),
        in_specs=[pl.BlockSpec((None, tm, UA_W), lambda b, i: (b, i, 0)), tab, tab, tab]
        + [full(a) for a in small],
        out_specs=[
            pl.BlockSpec((None, MLA_HEADS, tm, LANES), lambda b, i: (b, 0, i, 0)),
            pl.BlockSpec((None, MLA_HEADS, tm, LANES), lambda b, i: (b, 0, i, 0)),
            pl.BlockSpec((None, MLA_HEADS, tm, MLA_V), lambda b, i: (b, 0, i, 0)),
        ],
        out_shape=[
            jax.ShapeDtypeStruct((B, MLA_HEADS, T, LANES), BF16),
            jax.ShapeDtypeStruct((B, MLA_HEADS, T, LANES), BF16),
            jax.ShapeDtypeStruct((B, MLA_HEADS, T, MLA_V), BF16),
        ],
        compiler_params=_cparams(("parallel", "parallel")),
        name="mla_prep",
    )(ua, *tables, *small)


def _qkv_prep_kernel(*refs, q_off, nq, nk, nv, use_rope):
    if use_rope:
        x_ref, seg_ref, gq_ref, gk_ref, c_ref, s1_ref, s2_ref, q_out, k_out, v_out = refs
    else:
        x_ref, seg_ref, gq_ref, gk_ref, q_out, k_out, v_out = refs
    x = x_ref[...]
    wq, wk, wv = nq * HEAD_DIM, nk * HEAD_DIM, nv * HEAD_DIM
    q = x[:, q_off:q_off + wq]
    k = x[:, q_off + wq:q_off + wq + wk]
    v = x[:, q_off + wq + wk:q_off + wq + wk + wv]

    def headnorm(t, gain, w):
        hi, lo = _split2(t * t)
        seg = seg_ref[0:w, 0:w]
        ms = (_dot(hi, seg) + _dot(lo, seg)) * (1.0 / HEAD_DIM)
        return t * lax.rsqrt(ms + NORM_EPS) * gain

    q = headnorm(q, gq_ref[...], wq)
    k = headnorm(k, gk_ref[...], wk)
    if use_rope:
        half = HEAD_DIM // 4

        def rope(t, w):
            return (t * c_ref[:, 0:w] + pltpu.roll(t, half, 1) * s1_ref[:, 0:w]
                    + pltpu.roll(t, w - half, 1) * s2_ref[:, 0:w])

        q = rope(q, wq)
        k = rope(k, wk)
    for h in range(nq):
        q_out[h] = q[:, h * HEAD_DIM:(h + 1) * HEAD_DIM].astype(BF16)
    for h in range(nk):
        k_out[h] = k[:, h * HEAD_DIM:(h + 1) * HEAD_DIM].astype(BF16)
    for h in range(nv):
        v_out[h] = v[:, h * HEAD_DIM:(h + 1) * HEAD_DIM].astype(BF16)


def _qkv_prep(x, seg, gq, gk, nq, nk, nv, tables=None):
    B, T, W = x.shape
    tm = min(T, 512)
    use_rope = tables is not None
    full = lambda a: pl.BlockSpec(a.shape, lambda b, i: (0,) * a.ndim)
    in_specs = [pl.BlockSpec((None, tm, W), lambda b, i: (b, i, 0)), full(seg), full(gq), full(gk)]
    args = [x, seg, gq, gk]
    if use_rope:
        in_specs += [pl.BlockSpec((tm, tables[0].shape[1]), lambda b, i: (i, 0))] * 3
        args += list(tables)
    hm = lambda n: pl.BlockSpec((None, n, tm, HEAD_DIM), lambda b, i: (b, 0, i, 0))
    return pl.pallas_call(
        functools.partial(_qkv_prep_kernel, q_off=0, nq=nq, nk=nk, nv=nv, use_rope=use_rope),
        grid=(B, T // tm),
        in_specs=in_specs,
        out_specs=[hm(nq), hm(nk), hm(nv)],
        out_shape=[jax.ShapeDtypeStruct((B, n, T, HEAD_DIM), BF16) for n in (nq, nk, nv)],
        compiler_params=_cparams(("parallel", "parallel")),
        name="qkv_prep_rope" if use_rope else "qkv_prep",
    )(*args)


def _attn_kernel(*refs, n_heads, group, has_extra, nk):
    if has_extra:
        q_ref, k_ref, v_ref, kx_ref, vx_ref, o_ref, m_ref, l_ref, acc_ref = refs
    else:
        q_ref, k_ref, v_ref, o_ref, m_ref, l_ref, acc_ref = refs
    j = pl.program_id(2)

    def first(h, kt, vt):
        s = _dot_nt(q_ref[h], kt)
        m = jnp.max(s, axis=-1, keepdims=True)
        p = jnp.exp(s - m)
        m_ref[h] = m
        l_ref[h] = jnp.sum(p, axis=-1, keepdims=True)
        acc_ref[h] = _dot(p.astype(BF16), vt)

    def update(h, kt, vt):
        s = _dot_nt(q_ref[h], kt)
        m_prev = m_ref[h]
        m_new = jnp.maximum(m_prev, jnp.max(s, axis=-1, keepdims=True))
        alpha = jnp.exp(m_prev - m_new)
        p = jnp.exp(s - m_new)
        m_ref[h] = m_new
        l_ref[h] = alpha * l_ref[h] + jnp.sum(p, axis=-1, keepdims=True)
        acc_ref[h] = alpha * acc_ref[h] + _dot(p.astype(BF16), vt)

    if has_extra:
        @pl.when(j == 0)
        def _():
            for h in range(n_heads):
                first(h, kx_ref[h // group], vx_ref[h // group])

        for h in range(n_heads):
            update(h, k_ref[h // group], v_ref[h // group])
    else:
        @pl.when(j == 0)
        def _():
            for h in range(n_heads):
                first(h, k_ref[h // group], v_ref[h // group])

        @pl.when(j > 0)
        def _():
            for h in range(n_heads):
                update(h, k_ref[h // group], v_ref[h // group])

    @pl.when(j == nk - 1)
    def _():
        o_ref[...] = jnp.concatenate(
            [acc_ref[h] / l_ref[h] for h in range(n_heads)], axis=-1).astype(o_ref.dtype)


def _attention(q, k, v, kx=None, vx=None):
    B, H, Tq, dq = q.shape
    Hk, Tk, dv = k.shape[1], k.shape[2], v.shape[3]
    group = H // Hk
    tq = min(Tq, 512)
    tk = min(Tk, 1024)
    nk = Tk // tk
    has_extra = kx is not None
    in_specs = [
        pl.BlockSpec((None, H, tq, dq), lambda b, i, j: (b, 0, i, 0)),
        pl.BlockSpec((None, Hk, tk, dq), lambda b, i, j: (b, 0, j, 0)),
        pl.BlockSpec((None, Hk, tk, dv), lambda b, i, j: (b, 0, j, 0)),
    ]
    args = [q, k, v]
    if has_extra:
        Tx = kx.shape[2]
        in_specs += [pl.BlockSpec((None, Hk, Tx, dq), lambda b, i, j: (b, 0, 0, 0)),
                     pl.BlockSpec((None, Hk, Tx, dv), lambda b, i, j: (b, 0, 0, 0))]
        args += [kx, vx]
    return pl.pallas_call(
        functools.partial(_attn_kernel, n_heads=H, group=group, has_extra=has_extra, nk=nk),
        grid=(B, Tq // tq, nk),
        in_specs=in_specs,
        out_specs=pl.BlockSpec((None, tq, H * dv), lambda b, i, j: (b, i, 0)),
        out_shape=jax.ShapeDtypeStruct((B, Tq, H * dv), BF16),
        scratch_shapes=[pltpu.VMEM((H, tq, 1), F32), pltpu.VMEM((H, tq, 1), F32),
                        pltpu.VMEM((H, tq, dv), F32)],
        compiler_params=_cparams(("parallel", "parallel", "arbitrary")),
        name="flash_attention_ctx" if has_extra else "flash_attention",
    )(*args)


def _na_bias_kernel(r_ref, e_ref, m_ref, o_ref):
    o_ref[...] = _dot_exact_rhs(r_ref[...], e_ref[...]) + m_ref[...]


def _na_bias_tables(rpb):
    H, n_dr, n_dc = rpb.shape
    W = GRID_W
    qc = np.arange(W)[:, None]
    kc = np.arange(W)[None, :]
    c0 = np.clip(qc - NA_WIN_C // 2, 0, W - NA_WIN_C)
    valid = (kc >= c0) & (kc < c0 + NA_WIN_C)
    d = kc - qc + NA_WIN_C - 1
    onehot = np.zeros((LANES, W * W), np.float32)
    for a in range(W):
        for b in range(W):
            if valid[a, b]:
                onehot[d[a, b], a * W + b] = 1.0
    mask = np.where(valid, 0.0, NEG_INF).astype(np.float32).reshape(1, W * W)
    rows = H * n_dr
    rows_p = -(-rows // SUBLANES) * SUBLANES
    r2 = jnp.zeros((rows_p, LANES), F32).at[:rows, :n_dc].set(rpb.reshape(rows, n_dc))
    tn = 1024
    tiles = pl.pallas_call(
        _na_bias_kernel,
        grid=(W * W // tn,),
        in_specs=[pl.BlockSpec((rows_p, LANES), lambda n: (0, 0)),
                  pl.BlockSpec((LANES, tn), lambda n: (0, n)),
                  pl.BlockSpec((1, tn), lambda n: (0, n))],
        out_specs=pl.BlockSpec((rows_p, tn), lambda n: (0, n)),
        out_shape=jax.ShapeDtypeStruct((rows_p, W * W), F32),
        compiler_params=_cparams(("parallel",)),
        name="na_bias_expand",
    )(r2, jnp.asarray(onehot, BF16), jnp.asarray(mask))
    tiles = tiles[:rows].reshape(H, n_dr, W, W)
    slabs = [tiles[:, base:base + NA_WIN_R].transpose(0, 2, 1, 3).reshape(H, W, NA_WIN_R * W)
             for base in range(NA_WIN_R)]
    return jnp.stack(slabs, 0)


def _na_kernel(q_ref, k_ref, v_ref, kc_ref, vc_ref, tb_ref, o_ref, *, rb, rows, n_heads):
    blk = pl.program_id(1)
    W = GRID_W
    win = NA_WIN_R * W

    def row_body(i, carry):
        r = blk * rb + i
        r0 = jnp.clip(r - NA_WIN_R // 2, 0, rows - NA_WIN_R)
        base = r0 - r + NA_WIN_R - 1
        k0 = pl.multiple_of(r0 * W, W)
        q0 = pl.multiple_of(i * W, W)
        outs = []
        for h in range(n_heads):
            q = q_ref[h, pl.ds(q0, W), :]
            s_nb = _dot_nt(q, k_ref[h, pl.ds(k0, win), :]) + tb_ref[base, h]
            s_cx = _dot_nt(q, kc_ref[h])
            m = jnp.maximum(jnp.max(s_nb, axis=-1, keepdims=True),
                            jnp.max(s_cx, axis=-1, keepdims=True))
            p_nb = jnp.exp(s_nb - m)
            p_cx = jnp.exp(s_cx - m)
            l = jnp.sum(p_nb, axis=-1, keepdims=True) + jnp.sum(p_cx, axis=-1, keepdims=True)
            o = _dot(p_nb.astype(BF16), v_ref[h, pl.ds(k0, win), :]) + _dot(p_cx.astype(BF16), vc_ref[h])
            outs.append(o / l)
        o_ref[pl.ds(q0, W), :] = jnp.concatenate(outs, axis=-1).astype(o_ref.dtype)
        return carry

    lax.fori_loop(0, rb, row_body, 0)


def _na_attention(q, k, v, kc, vc, tb):
    B, H, S, dh = q.shape
    L = kc.shape[2]
    rows = S // GRID_W
    assert rows >= NA_WIN_R
    rb = 8
    return pl.pallas_call(
        functools.partial(_na_kernel, rb=rb, rows=rows, n_heads=H),
        grid=(B, rows // rb),
        in_specs=[
            pl.BlockSpec((None, H, rb * GRID_W, dh), lambda b, i: (b, 0, i, 0)),
            pl.BlockSpec((None, H, S, dh), lambda b, i: (b, 0, 0, 0)),
            pl.BlockSpec((None, H, S, dh), lambda b, i: (b, 0, 0, 0)),
            pl.BlockSpec((None, H, L, dh), lambda b, i: (b, 0, 0, 0)),
            pl.BlockSpec((None, H, L, dh), lambda b, i: (b, 0, 0, 0)),
            pl.BlockSpec(tb.shape, lambda b, i: (0, 0, 0, 0)),
        ],
        out_specs=pl.BlockSpec((None, rb * GRID_W, H * dh), lambda b, i: (b, i, 0)),
        out_shape=jax.ShapeDtypeStruct((B, S, H * dh), BF16),
        compiler_params=_cparams(("parallel", "arbitrary")),
        name="neighbourhood_attention",
    )(q, k, v, kc, vc, tb)


def _rwkv_feat_kernel(x_ref, xp_ref, xn_ref, mu_ref, seg_ref, w2_ref, a2_ref, g2_ref, vec_ref,
                      r_out, v_out, kk_out, lwf_out, lwb_out, kdf_out, kdb_out, af_out, ab_out,
                      bon_out, g_out, *, nblk):
    i = pl.program_id(1)
    x = x_ref[...]
    tm = x.shape[0]
    has_prev = jnp.where(i > 0, 1.0, 0.0)
    has_next = jnp.where(i < nblk - 1, 1.0, 0.0)
    row = lax.broadcasted_iota(jnp.int32, (tm, 1), 0)
    prev = jnp.where(row == 0, xp_ref[SUBLANES - 1:SUBLANES, :] * has_prev, pltpu.roll(x, 1, 0))
    nxt = jnp.where(row == tm - 1, xn_ref[0:1, :] * has_next, pltpu.roll(x, tm - 1, 0))
    xs = x + mu_ref[0:1, :] * (prev - x) + mu_ref[1:2, :] * (nxt - x)

    G = GROUP_W
    r, k, v = xs[:, 0:G], xs[:, G:2 * G], xs[:, 2 * G:3 * G]
    lora = xs[:, 3 * G:4 * G]
    seg = seg_ref[...]

    def segsum(t):
        hi, lo = _split2(t)
        return _dot(hi, seg) + _dot(lo, seg)

    k_k, k_a, r_k = vec_ref[0:1, :], vec_ref[1:2, :], vec_ref[2:3, :]
    kk = k * k_k
    kk = kk * lax.rsqrt(segsum(kk * kk) + 1e-12)
    g_out[...] = _dot(jax.nn.sigmoid(lora).astype(BF16), g2_ref[...])
    tanh_l = jnp.tanh(lora).astype(BF16)
    lora_b = lora.astype(BF16)
    kd_sum = None
    for d, (lw_out, kd_out, a_out) in enumerate(((lwf_out, kdf_out, af_out), (lwb_out, kdb_out, ab_out))):
        w0, a0 = vec_ref[3 + d:4 + d, :], vec_ref[5 + d:6 + d, :]
        z = w0 + _dot(tanh_l, w2_ref[d])
        w_log = -(jnp.maximum(-z, 0.0) + jnp.log(1.0 + jnp.exp(-jnp.abs(z)))) - 0.5
        lw_out[...] = -jnp.exp(w_log)
        a = jax.nn.sigmoid(a0 + _dot(lora_b, a2_ref[d]))
        kd = k * (1.0 + (a - 1.0) * k_a)
        kd_out[...] = kd
        a_out[...] = a
        kd_sum = kd if kd_sum is None else kd_sum + kd
    r_out[...] = r
    v_out[...] = v
    kk_out[...] = kk
    bon_out[...] = segsum(r * kd_sum * r_k) * v


def _rwkv_features(ub, mu_p, seg, w2_p, a2_p, g2_p, vecs):
    B, T, W = ub.shape
    tm = min(T, 256)
    nblk = T // tm
    hb = tm // SUBLANES
    nh = T // SUBLANES
    full = lambda a: pl.BlockSpec(a.shape, lambda b, i: (0,) * a.ndim)
    out = pl.BlockSpec((None, tm, GROUP_W), lambda b, i: (b, i, 0))
    return pl.pallas_call(
        functools.partial(_rwkv_feat_kernel, nblk=nblk),
        grid=(B, nblk),
        in_specs=[
            pl.BlockSpec((None, tm, W), lambda b, i: (b, i, 0)),
            pl.BlockSpec((None, SUBLANES, W), lambda b, i: (b, jnp.maximum(i * hb - 1, 0), 0)),
            pl.BlockSpec((None, SUBLANES, W), lambda b, i: (b, jnp.minimum((i + 1) * hb, nh - 1), 0)),
            full(mu_p), full(seg), full(w2_p), full(a2_p), full(g2_p), full(vecs),
        ],
        out_specs=[out] * 11,
        out_shape=[jax.ShapeDtypeStruct((B, T, GROUP_W), F32)] * 11,
        compiler_params=_cparams(("parallel", "parallel")),
        name="rwkv_features",
    )(ub, ub, ub, mu_p, seg, w2_p, a2_p, g2_p, vecs)


def _rwkv_chunk_kernel(r_ref, v_ref, kk_ref, lwf_ref, lwb_ref, kdf_ref, kdb_ref, af_ref, ab_ref,
                       cum_ref, tri_ref, hm_ref,
                       rh_out, y1_out, mp_out, np_out, pc_out):
    C = CHUNK
    r, v, kk = r_ref[...], v_ref[...], kk_ref[...]
    hm = hm_ref[...]

    def stack(t):
        return (jnp.concatenate([t, t, t, t], axis=0) * hm).astype(BF16)

    v4 = stack(v)
    eye = tri_ref[2]
    for d, (lw_ref, kd_ref, a_ref) in enumerate(((lwf_ref, kdf_ref, af_ref), (lwb_ref, kdb_ref, ab_ref))):
        lw, kd, a = lw_ref[...], kd_ref[...], a_ref[...]
        cl = _dot_exact_lhs(cum_ref[d], lw)
        last = C - 1 if d == 0 else 0
        lend = cl[last:last + 1, :]
        e_in, e_ex, e_neg, e_end = jnp.exp(cl), jnp.exp(cl - lw), jnp.exp(-cl), jnp.exp(lend)
        at = -kk * e_ex
        rt = r * e_in
        bt = kk * a * e_neg
        kt = kd * e_neg
        a4, r4, b4, k4 = stack(at), stack(rt), stack(bt), stack(kt)
        bh4, kh4 = stack(bt * e_end), stack(kt * e_end)
        strict, incl = tri_ref[d], tri_ref[d] + eye
        a_ab = _dot_nt(a4, b4) * strict
        a_ak = _dot_nt(a4, k4) * strict
        a_rb = _dot_nt(r4, b4) * incl
        a_rk = _dot_nt(r4, k4) * incl
        pw = a_ab.astype(BF16)
        tm = eye + a_ab
        for _ in range(5):
            pw_f = _dot(pw, pw)
            pw = pw_f.astype(BF16)
            tm = tm + _dot(tm.astype(BF16), pw)
        tm_b = tm.astype(BF16)
        wt4 = _dot(tm_b, a4)
        u04 = _dot(tm_b, _dot(a_ak.astype(BF16), v4).astype(BF16))
        wt4_b, u04_b = wt4.astype(BF16), u04.astype(BF16)
        a_rb_b = a_rb.astype(BF16)
        rh4 = _dot(a_rb_b, wt4_b)
        y14 = _dot(a_rb_b, u04_b) + _dot(a_rk.astype(BF16), v4)

        def collapse(t):
            return t[0:C] + t[C:2 * C] + t[2 * C:3 * C] + t[3 * C:4 * C]

        rh_out[d] = (rt + collapse(rh4)).astype(BF16)
        y1_out[d] = collapse(y14)
        mp_out[d] = _dot_tn(wt4_b, bh4).astype(BF16)
        np_out[d] = _dot_tn(u04_b, bh4) + _dot_tn(v4, kh4)
        pc_out[d] = jnp.broadcast_to(e_end, (SUBLANES, GROUP_W))


def _rwkv_chunks(feats, cum, tri, hm):
    r = feats[0]
    B, T, G = r.shape
    nch = T // CHUNK
    tok = pl.BlockSpec((None, CHUNK, G), lambda b, c: (b, c, 0))
    full = lambda a: pl.BlockSpec(a.shape, lambda b, c: (0,) * a.ndim)
    return pl.pallas_call(
        _rwkv_chunk_kernel,
        grid=(B, nch),
        in_specs=[tok] * 9 + [full(cum), full(tri), full(hm)],
        out_specs=[
            pl.BlockSpec((None, 2, CHUNK, G), lambda b, c: (b, 0, c, 0)),
            pl.BlockSpec((None, 2, CHUNK, G), lambda b, c: (b, 0, c, 0)),
            pl.BlockSpec((None, 2, None, G, G), lambda b, c: (b, 0, c, 0, 0)),
            pl.BlockSpec((None, 2, None, G, G), lambda b, c: (b, 0, c, 0, 0)),
            pl.BlockSpec((None, 2, None, SUBLANES, G), lambda b, c: (b, 0, c, 0, 0)),
        ],
        out_shape=[
            jax.ShapeDtypeStruct((B, 2, T, G), BF16),
            jax.ShapeDtypeStruct((B, 2, T, G), F32),
            jax.ShapeDtypeStruct((B, 2, nch, G, G), BF16),
            jax.ShapeDtypeStruct((B, 2, nch, G, G), F32),
            jax.ShapeDtypeStruct((B, 2, nch, SUBLANES, G), F32),
        ],
        compiler_params=_cparams(("parallel", "parallel")),
        name="rwkv_chunk_transitions",
    )(*feats, cum, tri, hm)


def _rwkv_scan_kernel(s0_ref, rhf_ref, y1f_ref, mpf_ref, npf_ref, pcf_ref,
                      rhb_ref, y1b_ref, mpb_ref, npb_ref, pcb_ref,
                      yf_out, yb_out, s_out, *, batch):
    c = pl.program_id(0)

    @pl.when(c == 0)
    def _():
        s_out[...] = s0_ref[...]

    dirs = ((rhf_ref, y1f_ref, mpf_ref, npf_ref, pcf_ref, yf_out),
            (rhb_ref, y1b_ref, mpb_ref, npb_ref, pcb_ref, yb_out))
    for b in range(batch):
        for d, (rh_ref, y1_ref, mp_ref, np_ref, pc_ref, y_out) in enumerate(dirs):
            s = s_out[b, d]
            s_b = s.astype(BF16)
            y_out[b] = _dot_nt(rh_ref[b], s_b) + y1_ref[b]
            s_out[b, d] = s * pc_ref[b, 0:1, :] + _dot(s_b, mp_ref[b]) + np_ref[b]


def _rwkv_scan(s0, rh, y1, mp, npm, pc):
    B, _, T, G = rh.shape
    nch = T // CHUNK
    fwd = lambda c: c
    bwd = lambda c: nch - 1 - c

    def specs(order, d):
        return [
            pl.BlockSpec((B, None, CHUNK, G), lambda c: (0, d, order(c), 0)),
            pl.BlockSpec((B, None, CHUNK, G), lambda c: (0, d, order(c), 0)),
            pl.BlockSpec((B, None, None, G, G), lambda c: (0, d, order(c), 0, 0)),
            pl.BlockSpec((B, None, None, G, G), lambda c: (0, d, order(c), 0, 0)),
            pl.BlockSpec((B, None, None, SUBLANES, G), lambda c: (0, d, order(c), 0, 0)),
        ]

    return pl.pallas_call(
        functools.partial(_rwkv_scan_kernel, batch=B),
        grid=(nch,),
        in_specs=[pl.BlockSpec(s0.shape, lambda c: (0, 0, 0, 0))] + specs(fwd, 0) + specs(bwd, 1),
        out_specs=[
            pl.BlockSpec((B, CHUNK, G), lambda c: (0, fwd(c), 0)),
            pl.BlockSpec((B, CHUNK, G), lambda c: (0, bwd(c), 0)),
            pl.BlockSpec(s0.shape, lambda c: (0, 0, 0, 0)),
        ],
        out_shape=[
            jax.ShapeDtypeStruct((B, T, G), F32),
            jax.ShapeDtypeStruct((B, T, G), F32),
            jax.ShapeDtypeStruct(s0.shape, F32),
        ],
        compiler_params=_cparams(("arbitrary",)),
        name="rwkv_state_scan",
    )(s0, rh, y1, mp, npm, pc, rh, y1, mp, npm, pc)


def _rwkv_post_kernel(yf_ref, yb_ref, bon_ref, g_ref, seg_ref, vec_ref, o_ref):
    y = yf_ref[...] + yb_ref[...]
    seg = seg_ref[...]

    def segmean(t):
        hi, lo = _split2(t)
        return (_dot(hi, seg) + _dot(lo, seg)) * (1.0 / HEAD_DIM)

    mu = segmean(y)
    yc = y - mu
    var = segmean(yc * yc)
    yn = yc * lax.rsqrt(var + RWKV_LNX_EPS) * vec_ref[0:1, :] + vec_ref[1:2, :]
    o_ref[...] = ((yn + bon_ref[...]) * g_ref[...]).astype(o_ref.dtype)


def _rwkv_post(yf, yb, bon, g, seg, vecs):
    B, T, G = yf.shape
    tm = min(T, 512)
    tok = pl.BlockSpec((None, tm, G), lambda b, i: (b, i, 0))
    full = lambda a: pl.BlockSpec(a.shape, lambda b, i: (0,) * a.ndim)
    return pl.pallas_call(
        _rwkv_post_kernel,
        grid=(B, T // tm),
        in_specs=[tok, tok, tok, tok, full(seg), full(vecs)],
        out_specs=tok,
        out_shape=jax.ShapeDtypeStruct((B, T, G), BF16),
        compiler_params=_cparams(("parallel", "parallel")),
        name="rwkv_output",
    )(yf, yb, bon, g, seg, vecs)


def _seg_matrix(width):
    i = np.arange(width) // HEAD_DIM
    return jnp.asarray((i[:, None] == i[None, :]).astype(np.float32), BF16)


def _scan_constants():
    C = CHUNK
    t = np.arange(4 * C)
    head, tok = t // C, t % C
    same = head[:, None] == head[None, :]
    lower = same & (tok[:, None] > tok[None, :])
    upper = same & (tok[:, None] < tok[None, :])
    eye = np.eye(4 * C, dtype=bool)
    tri = jnp.asarray(np.stack([lower, upper, eye]).astype(np.float32))
    i = np.arange(C)
    cum = jnp.asarray(np.stack([i[:, None] >= i[None, :], i[:, None] <= i[None, :]]).astype(np.float32), BF16)
    hm = jnp.asarray((head[:, None] == (np.arange(GROUP_W) // HEAD_DIM)[None, :]).astype(np.float32))
    return cum, tri, hm


def _pad_vec(v, lo, total):
    return jnp.zeros((total,), F32).at[lo:lo + v.shape[0]].set(v)


def kernel(x, c, ctx, c_ctx, w_mod, b_mod, norm_ffn1, ffn1_w_gu, ffn1_w_down, norm_mix, w_in, w_out, mla_q_norm, mla_kv_norm, mla_w_uq, mla_w_ukv, mla_qn, mla_kn, rwkv_shift, rwkv_w0, rwkv_w2, rwkv_a0, rwkv_a2, rwkv_g2, rwkv_k_k, rwkv_k_a, rwkv_r_k, rwkv_lnx_g, rwkv_lnx_b, na_qn, na_kn, na_rpb, gqa_qn, gqa_kn, norm_ffn2, ffn2_w_gu, ffn2_w_down):
    B, S, D = x.shape
    Lc = ctx.shape[1]
    depth = w_mod.shape[0]
    assert B + 1 <= SUBLANES and S % 512 == 0 and Lc % CHUNK == 0 and S % GRID_W == 0

    cvec = jnp.zeros((SUBLANES, D), F32).at[:B].set(c).at[B].set(c_ctx)
    mods = _modulation(cvec, w_mod, b_mod).reshape(depth, SUBLANES, 9, D)

    t = jnp.arange(S)
    pos = (t // GRID_W, t % GRID_W)
    zero = jnp.zeros((Lc,), jnp.int32)
    mla_tab, mla_tab_c = [
        [_pad_lanes(p, MLA_NOPE, LANES, fill) for p, fill in zip(_rope_tables(pp, MLA_ROPE, MLA_ROPE // 2, 1), (1.0, 0.0, 0.0))]
        for pp in (pos, (zero, zero))]
    gqa_tab, gqa_tab_c = [_rope_tables(pp, 4 * HEAD_DIM, HEAD_DIM // 2, 4) for pp in (pos, (zero, zero))]

    seg = _seg_matrix(GROUP_W)
    cum, tri, hm = _scan_constants()
    mla_scale = MLA_QK ** -0.5
    dh_scale = HEAD_DIM ** -0.5

    h, hc = x, ctx
    for l in range(depth):
        ctx_out = l < depth - 1
        m_lat = mods[l, :B]
        m_ctx = jnp.broadcast_to(mods[l, B], (B, 9, D))

        w_gu1, w_dn1 = ffn1_w_gu[l].astype(BF16), ffn1_w_down[l].astype(BF16)
        w_gu2, w_dn2 = ffn2_w_gu[l].astype(BF16), ffn2_w_down[l].astype(BF16)
        wi = w_in[l]
        zc = lambda n: jnp.zeros((D, n), F32)
        o_b = MLA_COLS
        o_n = o_b + RWKV_COLS
        o_d = o_n + NA_COLS
        w_in_p = jnp.concatenate([
            wi[:, :MLA_Q_RANK + MLA_KV_RANK], zc(64), wi[:, MLA_Q_RANK + MLA_KV_RANK:MLA_COLS], zc(32),
            wi[:, o_b:o_n], zc(UB_W - RWKV_COLS),
            wi[:, o_n:o_d],
            wi[:, o_d:o_d + GQA_COLS]], axis=1).astype(BF16)
        w_out_b = w_out[l].astype(BF16)

        wuq = mla_w_uq[l].reshape(MLA_Q_RANK, MLA_HEADS, MLA_QK)
        wuq_p = jnp.pad(wuq, ((0, 0), (0, 0), (0, LANES - MLA_QK))).reshape(MLA_Q_RANK, MLA_HEADS * LANES).astype(BF16)
        wukv = mla_w_ukv[l].reshape(MLA_KV_RANK, MLA_HEADS, MLA_NOPE + MLA_V)
        wuk_p = jnp.pad(wukv[..., :MLA_NOPE], ((0, 0), (0, 0), (0, LANES - MLA_NOPE))).reshape(MLA_KV_RANK, MLA_HEADS * LANES).astype(BF16)
        wuv = wukv[..., MLA_NOPE:].reshape(MLA_KV_RANK, MLA_HEADS * MLA_V).astype(BF16)
        mla_small = (mla_q_norm[l].reshape(1, -1), mla_kv_norm[l].reshape(1, -1),
                     _pad_vec(mla_qn[l] * mla_scale, 0, LANES).reshape(1, LANES),
                     _pad_vec(mla_kn[l], 0, LANES).reshape(1, LANES), wuq_p, wuk_p, wuv)

        na_gq = jnp.tile(na_qn[l] * dh_scale, 4).reshape(1, -1)
        na_gk = jnp.tile(na_kn[l], 4).reshape(1, -1)
        gqa_gq = jnp.tile(gqa_qn[l] * dh_scale, 4).reshape(1, -1)
        gqa_gk = jnp.tile(gqa_kn[l], 2).reshape(1, -1)
        na_tb = _na_bias_tables(na_rpb[l])

        mu_p = jnp.zeros((SUBLANES, UB_W), F32).at[:2, :RWKV_COLS].set(rwkv_shift[l])
        lo = 3 * GROUP_W

        def lora_w(w, off):
            return jnp.zeros((GROUP_W, GROUP_W), F32).at[off:off + w.shape[0]].set(w).astype(BF16)

        w2_p = jnp.stack([lora_w(rwkv_w2[l, 0], 0), lora_w(rwkv_w2[l, 1], 32)])
        a2_p = jnp.stack([lora_w(rwkv_a2[l, 0], 64), lora_w(rwkv_a2[l, 1], 96)])
        g2_p = lora_w(rwkv_g2[l], 128)
        feat_vecs = jnp.zeros((SUBLANES, GROUP_W), F32).at[0].set(rwkv_k_k[l]).at[1].set(rwkv_k_a[l]) \
            .at[2].set(rwkv_r_k[l].reshape(-1)).at[3:5].set(rwkv_w0[l]).at[5:7].set(rwkv_a0[l])
        post_vecs = jnp.zeros((SUBLANES, GROUP_W), F32).at[0].set(rwkv_lnx_g[l]).at[1].set(rwkv_lnx_b[l])

        h = _ffn(h, m_lat, norm_ffn1[l], w_gu1, w_dn1, 0)
        hc = _ffn(hc, m_ctx, norm_ffn1[l], w_gu1, w_dn1, 0)

        ua, ub, un, ud = _inproj(h, m_lat, norm_mix[l], w_in_p)
        uca, ucb, ucn, ucd = _inproj(hc, m_ctx, norm_mix[l], w_in_p)

        qa, ka, va = _mla_prep(ua, mla_tab, *mla_small)
        qca, kca, vca = _mla_prep(uca, mla_tab_c, *mla_small)
        oa = _attention(qa, ka, va, kca, vca)

        qn_, kn_, vn_ = _qkv_prep(un, seg, na_gq, na_gk, 4, 4, 4)
        qcn, kcn, vcn = _qkv_prep(ucn, seg, na_gq, na_gk, 4, 4, 4)
        on = _na_attention(qn_, kn_, vn_, kcn, vcn, na_tb)

        qd, kd, vd = _qkv_prep(ud, seg, gqa_gq, gqa_gk, 4, 2, 2, gqa_tab)
        qcd, kcd, vcd = _qkv_prep(ucd, seg, gqa_gq, gqa_gk, 4, 2, 2, gqa_tab_c)
        od = _attention(qd, kd, vd, kcd, vcd)

        feats_c = _rwkv_features(ucb, mu_p, seg, w2_p, a2_p, g2_p, feat_vecs)
        feats = _rwkv_features(ub, mu_p, seg, w2_p, a2_p, g2_p, feat_vecs)
        trans_c = _rwkv_chunks(feats_c[:9], cum, tri, hm)
        trans = _rwkv_chunks(feats[:9], cum, tri, hm)
        s0 = jnp.zeros((B, 2, GROUP_W, GROUP_W), F32)
        yfc, ybc, s_ctx = _rwkv_scan(s0, *trans_c)
        yf, yb, _ = _rwkv_scan(s_ctx, *trans)
        ob = _rwkv_post(yf, yb, feats[9], feats[10], seg, post_vecs)

        h = _outproj(h, m_lat, oa, ob, on, od, w_out_b)
        if ctx_out:
            oca = _attention(qca, kca, vca)
            ocn = _attention(qcn, kcn, vcn)
            ocd = _attention(qcd, kcd, vcd)
            ocb = _rwkv_post(yfc, ybc, feats_c[9], feats_c[10], seg, post_vecs)
            hc = _outproj(hc, m_ctx, oca, ocb, ocn, ocd, w_out_b)

        h = _ffn(h, m_lat, norm_ffn2[l], w_gu2, w_dn2, 6)
        if ctx_out:
            hc = _ffn(hc, m_ctx, norm_ffn2[l], w_gu2, w_dn2, 6)
    return h
```

```python
import functools

import numpy as np
import jax
import jax.numpy as jnp
from jax import lax
from jax.experimental import pallas as pl
from jax.experimental.pallas import tpu as pltpu

F32 = jnp.float32
BF16 = jnp.bfloat16

GRID_W = 64
HEAD_DIM = 64
GROUP_W = 256
ROPE_THETA = 10000.0
NORM_EPS = 1e-6
D_FF = 2816
MLA_HEADS = 4
MLA_Q_RANK = 256
MLA_KV_RANK = 128
MLA_NOPE = 64
MLA_ROPE = 32
MLA_V = 64
MLA_QK = MLA_NOPE + MLA_ROPE
RWKV_COLS = 960
RWKV_LNX_EPS = 64e-5
NA_WIN_R = 8
NA_WIN_C = 16
MLA_COLS = 416
NA_COLS = 768
GQA_COLS = 512

LANES = 128
SUBLANES = 8
VMEM_LIMIT_BYTES = 56 * 1024 * 1024

UA_W = 512
UB_W = 1024
UN_W = 768
UD_W = 512

VT_ROWS = 128
CHUNK = 64
NEG_INF = -1e30
LOG2E = 1.4426950408889634


def _dot(a, b):
    return jnp.dot(a, b, preferred_element_type=F32)


def _dot_nt(a, b):
    return lax.dot_general(a, b, (((1,), (1,)), ((), ())), preferred_element_type=F32)


def _dot_tn(a, b):
    return lax.dot_general(a, b, (((0,), (0,)), ((), ())), preferred_element_type=F32)


def _split2(x):
    hi = x.astype(BF16)
    lo = (x - hi.astype(F32)).astype(BF16)
    return hi, lo


def _dot_exact_rhs(x, m):
    h1 = x.astype(BF16)
    r1 = x - h1.astype(F32)
    h2 = r1.astype(BF16)
    h3 = (r1 - h2.astype(F32)).astype(BF16)
    return _dot(h1, m) + _dot(h2, m) + _dot(h3, m)


def _dot_exact_lhs(m, x):
    h1 = x.astype(BF16)
    r1 = x - h1.astype(F32)
    h2 = r1.astype(BF16)
    h3 = (r1 - h2.astype(F32)).astype(BF16)
    return _dot(m, h1) + _dot(m, h2) + _dot(m, h3)


def _rms(x, eps=NORM_EPS):
    return x * lax.rsqrt(jnp.mean(x * x, axis=-1, keepdims=True) + eps)


def _cparams(sem):
    return pltpu.CompilerParams(dimension_semantics=sem, vmem_limit_bytes=VMEM_LIMIT_BYTES)


def _mod_kernel(c_ref, w_ref, b_ref, o_ref):
    c = c_ref[...]
    s = c * jax.nn.sigmoid(c)
    o_ref[...] = jnp.dot(s, w_ref[...], preferred_element_type=F32,
                         precision=lax.Precision.HIGHEST) + b_ref[...]


def _modulation(cvec, w_mod, b_mod):
    L, D, N = w_mod.shape
    tn = 1024
    return pl.pallas_call(
        _mod_kernel,
        grid=(L, N // tn),
        in_specs=[
            pl.BlockSpec((SUBLANES, D), lambda l, n: (0, 0)),
            pl.BlockSpec((None, D, tn), lambda l, n: (l, 0, n)),
            pl.BlockSpec((None, 1, tn), lambda l, n: (l, 0, n)),
        ],
        out_specs=pl.BlockSpec((None, SUBLANES, tn), lambda l, n: (l, 0, n)),
        out_shape=jax.ShapeDtypeStruct((L, SUBLANES, N), F32),
        compiler_params=_cparams(("parallel", "parallel")),
        name="adaln_mod",
    )(cvec, w_mod, b_mod.reshape(L, 1, N))


def _ffn_kernel(h_ref, mod_ref, g_ref, wg_ref, wu_ref, wd_ref, o_ref, xn_ref, acc_ref, *, base, nf):
    f = pl.program_id(2)

    @pl.when(f == 0)
    def _():
        y = _rms(h_ref[...]) * g_ref[...]
        xm = y * (1.0 + mod_ref[base + 1:base + 2, :]) + mod_ref[base:base + 1, :]
        xn_ref[...] = xm.astype(BF16)
        acc_ref[...] = jnp.zeros_like(acc_ref)

    xn = xn_ref[...]
    g = _dot(xn, wg_ref[...])
    u = _dot(xn, wu_ref[...])
    a = (g * jax.nn.sigmoid(g) * u).astype(BF16)
    acc_ref[...] += _dot(a, wd_ref[...])

    @pl.when(f == nf - 1)
    def _():
        o_ref[...] = h_ref[...] + 0.5 * mod_ref[base + 2:base + 3, :] * acc_ref[...]


def _ffn(h, mod, gain, w_gu, w_down, base):
    B, T, D = h.shape
    F = w_down.shape[0]
    tm = min(T, 512)
    tf = F // 2
    nf = F // tf
    return pl.pallas_call(
        functools.partial(_ffn_kernel, base=base, nf=nf),
        grid=(B, T // tm, nf),
        in_specs=[
            pl.BlockSpec((None, tm, D), lambda b, i, f: (b, i, 0)),
            pl.BlockSpec((None, 9, D), lambda b, i, f: (b, 0, 0)),
            pl.BlockSpec((1, D), lambda b, i, f: (0, 0)),
            pl.BlockSpec((D, tf), lambda b, i, f: (0, f)),
            pl.BlockSpec((D, tf), lambda b, i, f: (0, nf + f)),
            pl.BlockSpec((tf, D), lambda b, i, f: (f, 0)),
        ],
        out_specs=pl.BlockSpec((None, tm, D), lambda b, i, f: (b, i, 0)),
        out_shape=jax.ShapeDtypeStruct((B, T, D), F32),
        scratch_shapes=[pltpu.VMEM((tm, D), BF16), pltpu.VMEM((tm, D), F32)],
        compiler_params=_cparams(("parallel", "parallel", "arbitrary")),
        name="swiglu_halfstep",
    )(h, mod, gain.reshape(1, D), w_gu, w_gu, w_down)


def _inproj_kernel(h_ref, mod_ref, g_ref, w_ref, oa_ref, ob_ref, on_ref, od_ref):
    y = _rms(h_ref[...]) * g_ref[...]
    xm = (y * (1.0 + mod_ref[4:5, :]) + mod_ref[3:4, :]).astype(BF16)
    o = 0
    for ref, w in ((oa_ref, UA_W), (ob_ref, UB_W), (on_ref, UN_W), (od_ref, UD_W)):
        ref[...] = _dot(xm, w_ref[:, o:o + w])
        o += w


def _inproj(h, mod, gain, w_in_p):
    B, T, D = h.shape
    tm = min(T, 512)
    W = w_in_p.shape[1]
    widths = (UA_W, UB_W, UN_W, UD_W)
    return pl.pallas_call(
        _inproj_kernel,
        grid=(B, T // tm),
        in_specs=[
            pl.BlockSpec((None, tm, D), lambda b, i: (b, i, 0)),
            pl.BlockSpec((None, 9, D), lambda b, i: (b, 0, 0)),
            pl.BlockSpec((1, D), lambda b, i: (0, 0)),
            pl.BlockSpec((D, W), lambda b, i: (0, 0)),
        ],
        out_specs=[pl.BlockSpec((None, tm, w), lambda b, i: (b, i, 0)) for w in widths],
        out_shape=[jax.ShapeDtypeStruct((B, T, w), F32) for w in widths],
        compiler_params=_cparams(("parallel", "parallel")),
        name="in_projection",
    )(h, mod, gain.reshape(1, D), w_in_p)


def _outproj_kernel(h_ref, mod_ref, oa_ref, ob_ref, on_ref, od_ref, w_ref, o_ref):
    acc = _dot(oa_ref[...], w_ref[0:GROUP_W, :])
    acc += _dot(ob_ref[...], w_ref[GROUP_W:2 * GROUP_W, :])
    acc += _dot(on_ref[...], w_ref[2 * GROUP_W:3 * GROUP_W, :])
    acc += _dot(od_ref[...], w_ref[3 * GROUP_W:4 * GROUP_W, :])
    o_ref[...] = h_ref[...] + mod_ref[5:6, :] * acc


def _outproj(h, mod, oa, ob, on, od, w_out):
    B, T, D = h.shape
    tm = min(T, 512)
    grp = pl.BlockSpec((None, tm, GROUP_W), lambda b, i: (b, i, 0))
    return pl.pallas_call(
        _outproj_kernel,
        grid=(B, T // tm),
        in_specs=[
            pl.BlockSpec((None, tm, D), lambda b, i: (b, i, 0)),
            pl.BlockSpec((None, 9, D), lambda b, i: (b, 0, 0)),
            grp, grp, grp, grp,
            pl.BlockSpec((4 * GROUP_W, D), lambda b, i: (0, 0)),
        ],
        out_specs=pl.BlockSpec((None, tm, D), lambda b, i: (b, i, 0)),
        out_shape=jax.ShapeDtypeStruct((B, T, D), F32),
        compiler_params=_cparams(("parallel", "parallel")),
        name="out_projection",
    )(h, mod, oa, ob, on, od, w_out)


def _rope_tables(positions, width, group, n_rep):
    half = group // 2
    inv_freq = ROPE_THETA ** (-jnp.arange(half, dtype=F32) / half)

    def axis(pos):
        ang = pos.astype(F32)[:, None] * inv_freq[None, :]
        c, s = jnp.cos(ang), jnp.sin(ang)
        return (jnp.concatenate([c, c], -1), jnp.concatenate([jnp.zeros_like(s), s], -1),
                jnp.concatenate([-s, jnp.zeros_like(s)], -1))

    row, col = positions
    parts = [jnp.concatenate([a, b], -1) for a, b in zip(axis(row), axis(col))]
    return [jnp.tile(p, (1, n_rep)) for p in parts]


def _pad_lanes(t, lo, total, fill):
    T = t.shape[0]
    return jnp.concatenate([jnp.full((T, lo), fill, F32), t,
                            jnp.full((T, total - lo - t.shape[1]), fill, F32)], -1)


def _vt_ext(v):
    tm = v.shape[0]
    lane = lax.broadcasted_iota(jnp.int32, (tm, VT_ROWS - HEAD_DIM), 1)
    aux = jnp.where(lane == 0, 1.0, 0.0)
    return jnp.concatenate([v, aux], axis=-1).T.astype(BF16)


def _mla_prep_kernel(ua_ref, c_ref, s1_ref, s2_ref, qnorm_ref, kvnorm_ref, qn_ref, kn_ref,
                     wuq_ref, wuk_ref, wuv_ref, q_out, k_out, vt_out):
    ua = ua_ref[...]
    qc = (_rms(ua[:, :MLA_Q_RANK]) * qnorm_ref[...]).astype(BF16)
    kvc = (_rms(ua[:, MLA_Q_RANK:MLA_Q_RANK + MLA_KV_RANK]) * kvnorm_ref[...]).astype(BF16)
    k_rope = ua[:, 3 * LANES:4 * LANES]
    q_all = _dot(qc, wuq_ref[...])
    k_all = _dot(kvc, wuk_ref[...])
    v_all = _dot(kvc, wuv_ref[...])
    cos, s_dn, s_up = c_ref[...], s1_ref[...], s2_ref[...]
    half = MLA_ROPE // 4

    def rope(x):
        return x * cos + pltpu.roll(x, half, 1) * s_dn + pltpu.roll(x, LANES - half, 1) * s_up

    def headnorm(x, gain):
        ms = jnp.sum(x * x, axis=-1, keepdims=True) * (1.0 / MLA_QK)
        return x * lax.rsqrt(ms + NORM_EPS) * gain

    for h in range(MLA_HEADS):
        qh = headnorm(q_all[:, h * LANES:(h + 1) * LANES], qn_ref[...])
        q_out[h] = rope(qh).astype(BF16)
        kh = headnorm(k_all[:, h * LANES:(h + 1) * LANES] + k_rope, kn_ref[...])
        k_out[h] = rope(kh).astype(BF16)
        vt_out[h] = _vt_ext(v_all[:, h * MLA_V:(h + 1) * MLA_V])


def _mla_prep(ua, tables, q_norm, kv_norm, qn_p, kn_p, wuq_p, wuk_p, wuv):
    B, T, _ = ua.shape
    tm = min(T, 512)
    tab = pl.BlockSpec((tm, LANES), lambda b, i: (i, 0))
    full = lambda a: pl.BlockSpec(a.shape, lambda b, i: (0,) * a.ndim)
    small = [q_norm, kv_norm, qn_p, kn_p, wuq_p, wuk_p, wuv]
    return pl.pallas_call(
        _mla_prep_kernel,
        grid=(B, T // tm),
        in_specs=[pl.BlockSpec((None, tm, UA_W), lambda b, i: (b, i, 0)), tab, tab, tab]
        + [full(a) for a in small],
        out_specs=[
            pl.BlockSpec((None, MLA_HEADS, tm, LANES), lambda b, i: (b, 0, i, 0)),
            pl.BlockSpec((None, MLA_HEADS, tm, LANES), lambda b, i: (b, 0, i, 0)),
            pl.BlockSpec((None, MLA_HEADS, VT_ROWS, tm), lambda b, i: (b, 0, 0, i)),
        ],
        out_shape=[
            jax.ShapeDtypeStruct((B, MLA_HEADS, T, LANES), BF16),
            jax.ShapeDtypeStruct((B, MLA_HEADS, T, LANES), BF16),
            jax.ShapeDtypeStruct((B, MLA_HEADS, VT_ROWS, T), BF16),
        ],
        compiler_params=_cparams(("parallel", "parallel")),
        name="mla_prep",
    )(ua, *tables, *small)


def _qkv_prep_kernel(*refs, q_off, nq, nk, nv, use_rope, v_rows, v_cols):
    n_in = 7 if use_rope else 4
    x_ref, seg_ref, gq_ref, gk_ref = refs[:4]
    if use_rope:
        c_ref, s1_ref, s2_ref = refs[4:7]
    q_out, k_out = refs[n_in:n_in + 2]
    v_outs = list(refs[n_in + 2:])
    v_out = v_outs.pop(0) if v_rows else None
    vt_out = v_outs.pop(0) if v_cols else None
    x = x_ref[...]
    wq, wk, wv = nq * HEAD_DIM, nk * HEAD_DIM, nv * HEAD_DIM
    q = x[:, q_off:q_off + wq]
    k = x[:, q_off + wq:q_off + wq + wk]
    v = x[:, q_off + wq + wk:q_off + wq + wk + wv]

    def headnorm(t, gain, w):
        hi, lo = _split2(t * t)
        seg = seg_ref[0:w, 0:w]
        ms = (_dot(hi, seg) + _dot(lo, seg)) * (1.0 / HEAD_DIM)
        return t * lax.rsqrt(ms + NORM_EPS) * gain

    q = headnorm(q, gq_ref[...], wq)
    k = headnorm(k, gk_ref[...], wk)
    if use_rope:
        half = HEAD_DIM // 4

        def rope(t, w):
            return (t * c_ref[:, 0:w] + pltpu.roll(t, half, 1) * s1_ref[:, 0:w]
                    + pltpu.roll(t, w - half, 1) * s2_ref[:, 0:w])

        q = rope(q, wq)
        k = rope(k, wk)
    for h in range(nq):
        q_out[h] = q[:, h * HEAD_DIM:(h + 1) * HEAD_DIM].astype(BF16)
    for h in range(nk):
        k_out[h] = k[:, h * HEAD_DIM:(h + 1) * HEAD_DIM].astype(BF16)
    for h in range(nv):
        vh = v[:, h * HEAD_DIM:(h + 1) * HEAD_DIM]
        if v_rows:
            v_out[h] = vh.astype(BF16)
        if v_cols:
            vt_out[h] = _vt_ext(vh)


def _qkv_prep(x, seg, gq, gk, nq, nk, nv, tables=None, v_rows=True, v_cols=False):
    B, T, W = x.shape
    tm = min(T, 512)
    use_rope = tables is not None
    full = lambda a: pl.BlockSpec(a.shape, lambda b, i: (0,) * a.ndim)
    in_specs = [pl.BlockSpec((None, tm, W), lambda b, i: (b, i, 0)), full(seg), full(gq), full(gk)]
    args = [x, seg, gq, gk]
    if use_rope:
        in_specs += [pl.BlockSpec((tm, tables[0].shape[1]), lambda b, i: (i, 0))] * 3
        args += list(tables)
    hm = lambda n: pl.BlockSpec((None, n, tm, HEAD_DIM), lambda b, i: (b, 0, i, 0))
    out_specs = [hm(nq), hm(nk)]
    out_shape = [jax.ShapeDtypeStruct((B, n, T, HEAD_DIM), BF16) for n in (nq, nk)]
    if v_rows:
        out_specs.append(hm(nv))
        out_shape.append(jax.ShapeDtypeStruct((B, nv, T, HEAD_DIM), BF16))
    if v_cols:
        out_specs.append(pl.BlockSpec((None, nv, VT_ROWS, tm), lambda b, i: (b, 0, 0, i)))
        out_shape.append(jax.ShapeDtypeStruct((B, nv, VT_ROWS, T), BF16))
    return pl.pallas_call(
        functools.partial(_qkv_prep_kernel, q_off=0, nq=nq, nk=nk, nv=nv, use_rope=use_rope,
                          v_rows=v_rows, v_cols=v_cols),
        grid=(B, T // tm),
        in_specs=in_specs,
        out_specs=out_specs,
        out_shape=out_shape,
        compiler_params=_cparams(("parallel", "parallel")),
        name="qkv_prep_rope" if use_rope else "qkv_prep",
    )(*args)


def _attn_kernel(*refs, n_heads, group, has_extra, nk):
    if has_extra:
        q_ref, k_ref, vt_ref, kx_ref, vtx_ref, o_ref, m_ref, acc_ref = refs
    else:
        q_ref, k_ref, vt_ref, o_ref, m_ref, acc_ref = refs
    j = pl.program_id(2)

    heads = range(n_heads)

    def first(kr, vr):
        s = [_dot_nt(kr[h // group], q_ref[h]) for h in heads]
        m = [jnp.max(s[h], axis=0, keepdims=True) for h in heads]
        p = [jnp.exp2(s[h] - m[h]).astype(BF16) for h in heads]
        for h in heads:
            m_ref[h] = m[h]
            acc_ref[h] = _dot(vr[h // group], p[h])

    def update(kr, vr):
        s = [_dot_nt(kr[h // group], q_ref[h]) for h in heads]
        m_prev = [m_ref[h] for h in heads]
        m_new = [jnp.maximum(m_prev[h], jnp.max(s[h], axis=0, keepdims=True)) for h in heads]
        p = [jnp.exp2(s[h] - m_new[h]).astype(BF16) for h in heads]
        for h in heads:
            m_ref[h] = m_new[h]
            acc_ref[h] = jnp.exp2(m_prev[h] - m_new[h]) * acc_ref[h] + _dot(vr[h // group], p[h])

    if has_extra:
        @pl.when(j == 0)
        def _():
            first(kx_ref, vtx_ref)

        update(k_ref, vt_ref)
    else:
        @pl.when(j == 0)
        def _():
            first(k_ref, vt_ref)

        @pl.when(j > 0)
        def _():
            update(k_ref, vt_ref)

    @pl.when(j == nk - 1)
    def _():
        outs = []
        for h in range(n_heads):
            a = acc_ref[h].T
            outs.append(a[:, :HEAD_DIM] / a[:, HEAD_DIM:HEAD_DIM + 1])
        o_ref[...] = jnp.concatenate(outs, axis=-1).astype(o_ref.dtype)


def _attention(q, k, vt, kx=None, vtx=None):
    B, H, Tq, dq = q.shape
    Hk, Tk = k.shape[1], k.shape[2]
    group = H // Hk
    tq = min(Tq, 512)
    tk = min(Tk, 1024)
    nk = Tk // tk
    has_extra = kx is not None
    in_specs = [
        pl.BlockSpec((None, H, tq, dq), lambda b, i, j: (b, 0, i, 0)),
        pl.BlockSpec((None, Hk, tk, dq), lambda b, i, j: (b, 0, j, 0)),
        pl.BlockSpec((None, Hk, VT_ROWS, tk), lambda b, i, j: (b, 0, 0, j)),
    ]
    args = [q, k, vt]
    if has_extra:
        Tx = kx.shape[2]
        in_specs += [pl.BlockSpec((None, Hk, Tx, dq), lambda b, i, j: (b, 0, 0, 0)),
                     pl.BlockSpec((None, Hk, VT_ROWS, Tx), lambda b, i, j: (b, 0, 0, 0))]
        args += [kx, vtx]
    return pl.pallas_call(
        functools.partial(_attn_kernel, n_heads=H, group=group, has_extra=has_extra, nk=nk),
        grid=(B, Tq // tq, nk),
        in_specs=in_specs,
        out_specs=pl.BlockSpec((None, tq, H * HEAD_DIM), lambda b, i, j: (b, i, 0)),
        out_shape=jax.ShapeDtypeStruct((B, Tq, H * HEAD_DIM), BF16),
        scratch_shapes=[pltpu.VMEM((H, 1, tq), F32), pltpu.VMEM((H, VT_ROWS, tq), F32)],
        compiler_params=_cparams(("parallel", "parallel", "arbitrary")),
        name="flash_attention_ctx" if has_extra else "flash_attention",
    )(*args)


def _na_bias_kernel(r_ref, e_ref, m_ref, o_ref):
    o_ref[...] = _dot_exact_rhs(r_ref[...], e_ref[...]) + m_ref[...]


def _na_bias_tables(rpb):
    H, n_dr, n_dc = rpb.shape
    W = GRID_W
    qc = np.arange(W)[:, None]
    kc = np.arange(W)[None, :]
    c0 = np.clip(qc - NA_WIN_C // 2, 0, W - NA_WIN_C)
    valid = (kc >= c0) & (kc < c0 + NA_WIN_C)
    d = kc - qc + NA_WIN_C - 1
    onehot = np.zeros((LANES, W * W), np.float32)
    for a in range(W):
        for b in range(W):
            if valid[a, b]:
                onehot[d[a, b], a * W + b] = 1.0
    mask = np.where(valid, 0.0, NEG_INF).astype(np.float32).reshape(1, W * W)
    rows = H * n_dr
    rows_p = -(-rows // SUBLANES) * SUBLANES
    r2 = jnp.zeros((rows_p, LANES), F32).at[:rows, :n_dc].set(rpb.reshape(rows, n_dc) * LOG2E)
    tn = 1024
    tiles = pl.pallas_call(
        _na_bias_kernel,
        grid=(W * W // tn,),
        in_specs=[pl.BlockSpec((rows_p, LANES), lambda n: (0, 0)),
                  pl.BlockSpec((LANES, tn), lambda n: (0, n)),
                  pl.BlockSpec((1, tn), lambda n: (0, n))],
        out_specs=pl.BlockSpec((rows_p, tn), lambda n: (0, n)),
        out_shape=jax.ShapeDtypeStruct((rows_p, W * W), F32),
        compiler_params=_cparams(("parallel",)),
        name="na_bias_expand",
    )(r2, jnp.asarray(onehot, BF16), jnp.asarray(mask))
    tiles = tiles[:rows].reshape(H, n_dr, W, W)
    slabs = [tiles[:, base:base + NA_WIN_R].transpose(0, 2, 1, 3).reshape(H, W, NA_WIN_R * W)
             for base in range(NA_WIN_R)]
    return jnp.stack(slabs, 0)


def _na_kernel(q_ref, k_ref, v_ref, kc_ref, vc_ref, tb_ref, o_ref, *, rb, rows, n_heads):
    blk = pl.program_id(1)
    W = GRID_W
    win = NA_WIN_R * W

    def row_body(i, carry):
        r = blk * rb + i
        r0 = jnp.clip(r - NA_WIN_R // 2, 0, rows - NA_WIN_R)
        base = r0 - r + NA_WIN_R - 1
        k0 = pl.multiple_of(r0 * W, W)
        q0 = pl.multiple_of(i * W, W)
        heads = range(n_heads)
        q = [q_ref[h, pl.ds(q0, W), :] for h in heads]
        s_nb = [_dot_nt(q[h], k_ref[h, pl.ds(k0, win), :]) + tb_ref[base, h] for h in heads]
        s_cx = [_dot_nt(q[h], kc_ref[h]) for h in heads]
        m = [jnp.maximum(jnp.max(s_nb[h], axis=-1, keepdims=True),
                         jnp.max(s_cx[h], axis=-1, keepdims=True)) for h in heads]
        p_nb = [jnp.exp2(s_nb[h] - m[h]) for h in heads]
        p_cx = [jnp.exp2(s_cx[h] - m[h]) for h in heads]
        l = [jnp.sum(p_nb[h], axis=-1, keepdims=True) + jnp.sum(p_cx[h], axis=-1, keepdims=True)
             for h in heads]
        o = [_dot(p_nb[h].astype(BF16), v_ref[h, pl.ds(k0, win), :])
             + _dot(p_cx[h].astype(BF16), vc_ref[h]) for h in heads]
        o_ref[pl.ds(q0, W), :] = jnp.concatenate(
            [o[h] / l[h] for h in heads], axis=-1).astype(o_ref.dtype)
        return carry

    lax.fori_loop(0, rb, row_body, 0)


def _na_attention(q, k, v, kc, vc, tb):
    B, H, S, dh = q.shape
    L = kc.shape[2]
    rows = S // GRID_W
    assert rows >= NA_WIN_R
    rb = 8
    return pl.pallas_call(
        functools.partial(_na_kernel, rb=rb, rows=rows, n_heads=H),
        grid=(B, rows // rb),
        in_specs=[
            pl.BlockSpec((None, H, rb * GRID_W, dh), lambda b, i: (b, 0, i, 0)),
            pl.BlockSpec((None, H, S, dh), lambda b, i: (b, 0, 0, 0)),
            pl.BlockSpec((None, H, S, dh), lambda b, i: (b, 0, 0, 0)),
            pl.BlockSpec((None, H, L, dh), lambda b, i: (b, 0, 0, 0)),
            pl.BlockSpec((None, H, L, dh), lambda b, i: (b, 0, 0, 0)),
            pl.BlockSpec(tb.shape, lambda b, i: (0, 0, 0, 0)),
        ],
        out_specs=pl.BlockSpec((None, rb * GRID_W, H * dh), lambda b, i: (b, i, 0)),
        out_shape=jax.ShapeDtypeStruct((B, S, H * dh), BF16),
        compiler_params=_cparams(("parallel", "arbitrary")),
        name="neighbourhood_attention",
    )(q, k, v, kc, vc, tb)


def _rwkv_feat_kernel(x_ref, xp_ref, xn_ref, mu_ref, seg_ref, w2_ref, a2_ref, g2_ref, vec_ref,
                      r_out, v_out, kk_out, lwf_out, lwb_out, kdf_out, kdb_out, af_out, ab_out,
                      bon_out, g_out, *, nblk):
    i = pl.program_id(1)
    x = x_ref[...]
    tm = x.shape[0]
    has_prev = jnp.where(i > 0, 1.0, 0.0)
    has_next = jnp.where(i < nblk - 1, 1.0, 0.0)
    row = lax.broadcasted_iota(jnp.int32, (tm, 1), 0)
    prev = jnp.where(row == 0, xp_ref[SUBLANES - 1:SUBLANES, :] * has_prev, pltpu.roll(x, 1, 0))
    nxt = jnp.where(row == tm - 1, xn_ref[0:1, :] * has_next, pltpu.roll(x, tm - 1, 0))
    xs = x + mu_ref[0:1, :] * (prev - x) + mu_ref[1:2, :] * (nxt - x)

    G = GROUP_W
    r, k, v = xs[:, 0:G], xs[:, G:2 * G], xs[:, 2 * G:3 * G]
    lora = xs[:, 3 * G:4 * G]
    seg = seg_ref[...]

    def segsum(t):
        hi, lo = _split2(t)
        return _dot(hi, seg) + _dot(lo, seg)

    k_k, k_a, r_k = vec_ref[0:1, :], vec_ref[1:2, :], vec_ref[2:3, :]
    kk = k * k_k
    kk = kk * lax.rsqrt(segsum(kk * kk) + 1e-12)
    g_out[...] = _dot(jax.nn.sigmoid(lora).astype(BF16), g2_ref[...])
    tanh_l = jnp.tanh(lora).astype(BF16)
    lora_b = lora.astype(BF16)
    kd_sum = None
    for d, (lw_out, kd_out, a_out) in enumerate(((lwf_out, kdf_out, af_out), (lwb_out, kdb_out, ab_out))):
        w0, a0 = vec_ref[3 + d:4 + d, :], vec_ref[5 + d:6 + d, :]
        z = w0 + _dot(tanh_l, w2_ref[d])
        w_log = -(jnp.maximum(-z, 0.0) + jnp.log(1.0 + jnp.exp(-jnp.abs(z)))) - 0.5
        lw_out[...] = -jnp.exp(w_log)
        a = jax.nn.sigmoid(a0 + _dot(lora_b, a2_ref[d]))
        kd = k * (1.0 + (a - 1.0) * k_a)
        kd_out[...] = kd
        a_out[...] = a
        kd_sum = kd if kd_sum is None else kd_sum + kd
    r_out[...] = r
    v_out[...] = v
    kk_out[...] = kk
    bon_out[...] = segsum(r * kd_sum * r_k) * v


def _rwkv_features(ub, mu_p, seg, w2_p, a2_p, g2_p, vecs):
    B, T, W = ub.shape
    tm = min(T, 256)
    nblk = T // tm
    hb = tm // SUBLANES
    nh = T // SUBLANES
    full = lambda a: pl.BlockSpec(a.shape, lambda b, i: (0,) * a.ndim)
    out = pl.BlockSpec((None, tm, GROUP_W), lambda b, i: (b, i, 0))
    return pl.pallas_call(
        functools.partial(_rwkv_feat_kernel, nblk=nblk),
        grid=(B, nblk),
        in_specs=[
            pl.BlockSpec((None, tm, W), lambda b, i: (b, i, 0)),
            pl.BlockSpec((None, SUBLANES, W), lambda b, i: (b, jnp.maximum(i * hb - 1, 0), 0)),
            pl.BlockSpec((None, SUBLANES, W), lambda b, i: (b, jnp.minimum((i + 1) * hb, nh - 1), 0)),
            full(mu_p), full(seg), full(w2_p), full(a2_p), full(g2_p), full(vecs),
        ],
        out_specs=[out] * 11,
        out_shape=[jax.ShapeDtypeStruct((B, T, GROUP_W), F32)] * 11,
        compiler_params=_cparams(("parallel", "parallel")),
        name="rwkv_features",
    )(ub, ub, ub, mu_p, seg, w2_p, a2_p, g2_p, vecs)


def _rwkv_chunk_kernel(r_ref, v_ref, kk_ref, lwf_ref, lwb_ref, kdf_ref, kdb_ref, af_ref, ab_ref,
                       cum_ref, tri_ref, hm_ref,
                       rh_out, y1_out, mp_out, np_out, pc_out, *, nsub):
    C = CHUNK
    hm = hm_ref[...]
    eye = tri_ref[2]
    lw_refs, kd_refs, a_refs = (lwf_ref, lwb_ref), (kdf_ref, kdb_ref), (af_ref, ab_ref)

    def stack(t):
        return (jnp.concatenate([t, t, t, t], axis=0) * hm).astype(BF16)

    def collapse(t):
        return t[0:C] + t[C:2 * C] + t[2 * C:3 * C] + t[3 * C:4 * C]

    chains = [(ci, d) for ci in range(nsub) for d in range(2)]
    n = len(chains)
    rows = [slice(ci * C, (ci + 1) * C) for ci, _ in chains]

    rt, a4, r4, b4, k4, bh4, kh4, v4, e_end = ([None] * n for _ in range(9))
    for i, (ci, d) in enumerate(chains):
        rs = rows[i]
        r, v, kk = r_ref[rs, :], v_ref[rs, :], kk_ref[rs, :]
        lw, kd, a = lw_refs[d][rs, :], kd_refs[d][rs, :], a_refs[d][rs, :]
        cl = _dot_exact_lhs(cum_ref[d], lw)
        last = C - 1 if d == 0 else 0
        e_end[i] = jnp.exp(cl[last:last + 1, :])
        e_neg = jnp.exp(-cl)
        bt = kk * a * e_neg
        kt = kd * e_neg
        rt[i] = r * jnp.exp(cl)
        a4[i], r4[i], b4[i], k4[i] = stack(-kk * jnp.exp(cl - lw)), stack(rt[i]), stack(bt), stack(kt)
        bh4[i], kh4[i], v4[i] = stack(bt * e_end[i]), stack(kt * e_end[i]), stack(v)

    strict = [tri_ref[d] for _, d in chains]
    a_ab = [_dot_nt(a4[i], b4[i]) * strict[i] for i in range(n)]
    a_ak = [(_dot_nt(a4[i], k4[i]) * strict[i]).astype(BF16) for i in range(n)]
    a_rb = [(_dot_nt(r4[i], b4[i]) * (strict[i] + eye)).astype(BF16) for i in range(n)]
    a_rk = [(_dot_nt(r4[i], k4[i]) * (strict[i] + eye)).astype(BF16) for i in range(n)]

    pw = [t.astype(BF16) for t in a_ab]
    tm = [eye + t for t in a_ab]
    for _ in range(5):
        pw = [_dot(pw[i], pw[i]).astype(BF16) for i in range(n)]
        tm = [tm[i] + _dot(tm[i].astype(BF16), pw[i]) for i in range(n)]
    tm = [t.astype(BF16) for t in tm]

    x4 = [_dot(a_ak[i], v4[i]).astype(BF16) for i in range(n)]
    wt4 = [_dot(tm[i], a4[i]).astype(BF16) for i in range(n)]
    u04 = [_dot(tm[i], x4[i]).astype(BF16) for i in range(n)]
    rh4 = [_dot(a_rb[i], wt4[i]) for i in range(n)]
    y14 = [_dot(a_rb[i], u04[i]) + _dot(a_rk[i], v4[i]) for i in range(n)]
    for i, (ci, d) in enumerate(chains):
        rh_out[d, rows[i], :] = (rt[i] + collapse(rh4[i])).astype(BF16)
        y1_out[d, rows[i], :] = collapse(y14[i])
        mp_out[d, ci] = _dot_tn(wt4[i], bh4[i]).astype(BF16)
        np_out[d, ci] = _dot_tn(u04[i], bh4[i]) + _dot_tn(v4[i], kh4[i])
        pc_out[d, ci] = jnp.broadcast_to(e_end[i], (SUBLANES, GROUP_W))


def _rwkv_chunks(feats, cum, tri, hm):
    r = feats[0]
    B, T, G = r.shape
    nch = T // CHUNK
    nsub = 2 if nch % 2 == 0 else 1
    tb = nsub * CHUNK
    tok = pl.BlockSpec((None, tb, G), lambda b, c: (b, c, 0))
    full = lambda a: pl.BlockSpec(a.shape, lambda b, c: (0,) * a.ndim)
    return pl.pallas_call(
        functools.partial(_rwkv_chunk_kernel, nsub=nsub),
        grid=(B, nch // nsub),
        in_specs=[tok] * 9 + [full(cum), full(tri), full(hm)],
        out_specs=[
            pl.BlockSpec((None, 2, tb, G), lambda b, c: (b, 0, c, 0)),
            pl.BlockSpec((None, 2, tb, G), lambda b, c: (b, 0, c, 0)),
            pl.BlockSpec((None, 2, nsub, G, G), lambda b, c: (b, 0, c, 0, 0)),
            pl.BlockSpec((None, 2, nsub, G, G), lambda b, c: (b, 0, c, 0, 0)),
            pl.BlockSpec((None, 2, nsub, SUBLANES, G), lambda b, c: (b, 0, c, 0, 0)),
        ],
        out_shape=[
            jax.ShapeDtypeStruct((B, 2, T, G), BF16),
            jax.ShapeDtypeStruct((B, 2, T, G), F32),
            jax.ShapeDtypeStruct((B, 2, nch, G, G), BF16),
            jax.ShapeDtypeStruct((B, 2, nch, G, G), F32),
            jax.ShapeDtypeStruct((B, 2, nch, SUBLANES, G), F32),
        ],
        compiler_params=_cparams(("parallel", "parallel")),
        name="rwkv_chunk_transitions",
    )(*feats, cum, tri, hm)


def _rwkv_scan_kernel(s0_ref, rhf_ref, y1f_ref, mpf_ref, npf_ref, pcf_ref,
                      rhb_ref, y1b_ref, mpb_ref, npb_ref, pcb_ref,
                      yf_out, yb_out, s_out, *, batch):
    c = pl.program_id(0)

    @pl.when(c == 0)
    def _():
        s_out[...] = s0_ref[...]

    dirs = ((rhf_ref, y1f_ref, mpf_ref, npf_ref, pcf_ref, yf_out),
            (rhb_ref, y1b_ref, mpb_ref, npb_ref, pcb_ref, yb_out))
    for b in range(batch):
        for d, (rh_ref, y1_ref, mp_ref, np_ref, pc_ref, y_out) in enumerate(dirs):
            s = s_out[b, d]
            s_b = s.astype(BF16)
            y_out[b] = _dot_nt(rh_ref[b], s_b) + y1_ref[b]
            s_out[b, d] = s * pc_ref[b, 0:1, :] + _dot(s_b, mp_ref[b]) + np_ref[b]


def _rwkv_scan(s0, rh, y1, mp, npm, pc):
    B, _, T, G = rh.shape
    nch = T // CHUNK
    fwd = lambda c: c
    bwd = lambda c: nch - 1 - c

    def specs(order, d):
        return [
            pl.BlockSpec((B, None, CHUNK, G), lambda c: (0, d, order(c), 0)),
            pl.BlockSpec((B, None, CHUNK, G), lambda c: (0, d, order(c), 0)),
            pl.BlockSpec((B, None, None, G, G), lambda c: (0, d, order(c), 0, 0)),
            pl.BlockSpec((B, None, None, G, G), lambda c: (0, d, order(c), 0, 0)),
            pl.BlockSpec((B, None, None, SUBLANES, G), lambda c: (0, d, order(c), 0, 0)),
        ]

    return pl.pallas_call(
        functools.partial(_rwkv_scan_kernel, batch=B),
        grid=(nch,),
        in_specs=[pl.BlockSpec(s0.shape, lambda c: (0, 0, 0, 0))] + specs(fwd, 0) + specs(bwd, 1),
        out_specs=[
            pl.BlockSpec((B, CHUNK, G), lambda c: (0, fwd(c), 0)),
            pl.BlockSpec((B, CHUNK, G), lambda c: (0, bwd(c), 0)),
            pl.BlockSpec(s0.shape, lambda c: (0, 0, 0, 0)),
        ],
        out_shape=[
            jax.ShapeDtypeStruct((B, T, G), F32),
            jax.ShapeDtypeStruct((B, T, G), F32),
            jax.ShapeDtypeStruct(s0.shape, F32),
        ],
        compiler_params=_cparams(("arbitrary",)),
        name="rwkv_state_scan",
    )(s0, rh, y1, mp, npm, pc, rh, y1, mp, npm, pc)


def _rwkv_post_kernel(yf_ref, yb_ref, bon_ref, g_ref, seg_ref, vec_ref, o_ref):
    y = yf_ref[...] + yb_ref[...]
    seg = seg_ref[...]

    def segmean(t):
        hi, lo = _split2(t)
        return (_dot(hi, seg) + _dot(lo, seg)) * (1.0 / HEAD_DIM)

    mu = segmean(y)
    yc = y - mu
    var = segmean(yc * yc)
    yn = yc * lax.rsqrt(var + RWKV_LNX_EPS) * vec_ref[0:1, :] + vec_ref[1:2, :]
    o_ref[...] = ((yn + bon_ref[...]) * g_ref[...]).astype(o_ref.dtype)


def _rwkv_post(yf, yb, bon, g, seg, vecs):
    B, T, G = yf.shape
    tm = min(T, 512)
    tok = pl.BlockSpec((None, tm, G), lambda b, i: (b, i, 0))
    full = lambda a: pl.BlockSpec(a.shape, lambda b, i: (0,) * a.ndim)
    return pl.pallas_call(
        _rwkv_post_kernel,
        grid=(B, T // tm),
        in_specs=[tok, tok, tok, tok, full(seg), full(vecs)],
        out_specs=tok,
        out_shape=jax.ShapeDtypeStruct((B, T, G), BF16),
        compiler_params=_cparams(("parallel", "parallel")),
        name="rwkv_output",
    )(yf, yb, bon, g, seg, vecs)


def _seg_matrix(width):
    i = np.arange(width) // HEAD_DIM
    return jnp.asarray((i[:, None] == i[None, :]).astype(np.float32), BF16)


def _scan_constants():
    C = CHUNK
    t = np.arange(4 * C)
    head, tok = t // C, t % C
    same = head[:, None] == head[None, :]
    lower = same & (tok[:, None] > tok[None, :])
    upper = same & (tok[:, None] < tok[None, :])
    eye = np.eye(4 * C, dtype=bool)
    tri = jnp.asarray(np.stack([lower, upper, eye]).astype(np.float32))
    i = np.arange(C)
    cum = jnp.asarray(np.stack([i[:, None] >= i[None, :], i[:, None] <= i[None, :]]).astype(np.float32), BF16)
    hm = jnp.asarray((head[:, None] == (np.arange(GROUP_W) // HEAD_DIM)[None, :]).astype(np.float32))
    return cum, tri, hm


def _pad_vec(v, lo, total):
    return jnp.zeros((total,), F32).at[lo:lo + v.shape[0]].set(v)


def kernel(x, c, ctx, c_ctx, w_mod, b_mod, norm_ffn1, ffn1_w_gu, ffn1_w_down, norm_mix, w_in, w_out, mla_q_norm, mla_kv_norm, mla_w_uq, mla_w_ukv, mla_qn, mla_kn, rwkv_shift, rwkv_w0, rwkv_w2, rwkv_a0, rwkv_a2, rwkv_g2, rwkv_k_k, rwkv_k_a, rwkv_r_k, rwkv_lnx_g, rwkv_lnx_b, na_qn, na_kn, na_rpb, gqa_qn, gqa_kn, norm_ffn2, ffn2_w_gu, ffn2_w_down):
    B, S, D = x.shape
    Lc = ctx.shape[1]
    depth = w_mod.shape[0]
    assert B + 1 <= SUBLANES and S % 512 == 0 and Lc % CHUNK == 0 and S % GRID_W == 0

    cvec = jnp.zeros((SUBLANES, D), F32).at[:B].set(c).at[B].set(c_ctx)
    mods = _modulation(cvec, w_mod, b_mod).reshape(depth, SUBLANES, 9, D)

    t = jnp.arange(S)
    pos = (t // GRID_W, t % GRID_W)
    zero = jnp.zeros((Lc,), jnp.int32)
    mla_tab, mla_tab_c = [
        [_pad_lanes(p, MLA_NOPE, LANES, fill) for p, fill in zip(_rope_tables(pp, MLA_ROPE, MLA_ROPE // 2, 1), (1.0, 0.0, 0.0))]
        for pp in (pos, (zero, zero))]
    gqa_tab, gqa_tab_c = [_rope_tables(pp, 4 * HEAD_DIM, HEAD_DIM // 2, 4) for pp in (pos, (zero, zero))]

    seg = _seg_matrix(GROUP_W)
    cum, tri, hm = _scan_constants()
    mla_scale = MLA_QK ** -0.5 * LOG2E
    dh_scale = HEAD_DIM ** -0.5 * LOG2E

    h, hc = x, ctx
    for l in range(depth):
        ctx_out = l < depth - 1
        m_lat = mods[l, :B]
        m_ctx = jnp.broadcast_to(mods[l, B], (B, 9, D))

        w_gu1, w_dn1 = ffn1_w_gu[l].astype(BF16), ffn1_w_down[l].astype(BF16)
        w_gu2, w_dn2 = ffn2_w_gu[l].astype(BF16), ffn2_w_down[l].astype(BF16)
        wi = w_in[l]
        zc = lambda n: jnp.zeros((D, n), F32)
        o_b = MLA_COLS
        o_n = o_b + RWKV_COLS
        o_d = o_n + NA_COLS
        w_in_p = jnp.concatenate([
            wi[:, :MLA_Q_RANK + MLA_KV_RANK], zc(64), wi[:, MLA_Q_RANK + MLA_KV_RANK:MLA_COLS], zc(32),
            wi[:, o_b:o_n], zc(UB_W - RWKV_COLS),
            wi[:, o_n:o_d],
            wi[:, o_d:o_d + GQA_COLS]], axis=1).astype(BF16)
        w_out_b = w_out[l].astype(BF16)

        wuq = mla_w_uq[l].reshape(MLA_Q_RANK, MLA_HEADS, MLA_QK)
        wuq_p = jnp.pad(wuq, ((0, 0), (0, 0), (0, LANES - MLA_QK))).reshape(MLA_Q_RANK, MLA_HEADS * LANES).astype(BF16)
        wukv = mla_w_ukv[l].reshape(MLA_KV_RANK, MLA_HEADS, MLA_NOPE + MLA_V)
        wuk_p = jnp.pad(wukv[..., :MLA_NOPE], ((0, 0), (0, 0), (0, LANES - MLA_NOPE))).reshape(MLA_KV_RANK, MLA_HEADS * LANES).astype(BF16)
        wuv = wukv[..., MLA_NOPE:].reshape(MLA_KV_RANK, MLA_HEADS * MLA_V).astype(BF16)
        mla_small = (mla_q_norm[l].reshape(1, -1), mla_kv_norm[l].reshape(1, -1),
                     _pad_vec(mla_qn[l] * mla_scale, 0, LANES).reshape(1, LANES),
                     _pad_vec(mla_kn[l], 0, LANES).reshape(1, LANES), wuq_p, wuk_p, wuv)

        na_gq = jnp.tile(na_qn[l] * dh_scale, 4).reshape(1, -1)
        na_gk = jnp.tile(na_kn[l], 4).reshape(1, -1)
        gqa_gq = jnp.tile(gqa_qn[l] * dh_scale, 4).reshape(1, -1)
        gqa_gk = jnp.tile(gqa_kn[l], 2).reshape(1, -1)
        na_tb = _na_bias_tables(na_rpb[l])

        mu_p = jnp.zeros((SUBLANES, UB_W), F32).at[:2, :RWKV_COLS].set(rwkv_shift[l])
        lo = 3 * GROUP_W

        def lora_w(w, off):
            return jnp.zeros((GROUP_W, GROUP_W), F32).at[off:off + w.shape[0]].set(w).astype(BF16)

        w2_p = jnp.stack([lora_w(rwkv_w2[l, 0], 0), lora_w(rwkv_w2[l, 1], 32)])
        a2_p = jnp.stack([lora_w(rwkv_a2[l, 0], 64), lora_w(rwkv_a2[l, 1], 96)])
        g2_p = lora_w(rwkv_g2[l], 128)
        feat_vecs = jnp.zeros((SUBLANES, GROUP_W), F32).at[0].set(rwkv_k_k[l]).at[1].set(rwkv_k_a[l]) \
            .at[2].set(rwkv_r_k[l].reshape(-1)).at[3:5].set(rwkv_w0[l]).at[5:7].set(rwkv_a0[l])
        post_vecs = jnp.zeros((SUBLANES, GROUP_W), F32).at[0].set(rwkv_lnx_g[l]).at[1].set(rwkv_lnx_b[l])

        h = _ffn(h, m_lat, norm_ffn1[l], w_gu1, w_dn1, 0)
        hc = _ffn(hc, m_ctx, norm_ffn1[l], w_gu1, w_dn1, 0)

        ua, ub, un, ud = _inproj(h, m_lat, norm_mix[l], w_in_p)
        uca, ucb, ucn, ucd = _inproj(hc, m_ctx, norm_mix[l], w_in_p)

        qa, ka, vta = _mla_prep(ua, mla_tab, *mla_small)
        qca, kca, vtca = _mla_prep(uca, mla_tab_c, *mla_small)
        oa = _attention(qa, ka, vta, kca, vtca)

        qn_, kn_, vn_ = _qkv_prep(un, seg, na_gq, na_gk, 4, 4, 4)
        qcn, kcn, vcn, vtcn = _qkv_prep(ucn, seg, na_gq, na_gk, 4, 4, 4, v_cols=True)
        on = _na_attention(qn_, kn_, vn_, kcn, vcn, na_tb)

        qd, kd, vtd = _qkv_prep(ud, seg, gqa_gq, gqa_gk, 4, 2, 2, gqa_tab, v_rows=False, v_cols=True)
        qcd, kcd, vtcd = _qkv_prep(ucd, seg, gqa_gq, gqa_gk, 4, 2, 2, gqa_tab_c, v_rows=False, v_cols=True)
        od = _attention(qd, kd, vtd, kcd, vtcd)

        feats_c = _rwkv_features(ucb, mu_p, seg, w2_p, a2_p, g2_p, feat_vecs)
        feats = _rwkv_features(ub, mu_p, seg, w2_p, a2_p, g2_p, feat_vecs)
        trans_c = _rwkv_chunks(feats_c[:9], cum, tri, hm)
        trans = _rwkv_chunks(feats[:9], cum, tri, hm)
        s0 = jnp.zeros((B, 2, GROUP_W, GROUP_W), F32)
        yfc, ybc, s_ctx = _rwkv_scan(s0, *trans_c)
        yf, yb, _ = _rwkv_scan(s_ctx, *trans)
        ob = _rwkv_post(yf, yb, feats[9], feats[10], seg, post_vecs)

        h = _outproj(h, m_lat, oa, ob, on, od, w_out_b)
        if ctx_out:
            oca = _attention(qca, kca, vtca)
            ocn = _attention(qcn, kcn, vtcn)
            ocd = _attention(qcd, kcd, vtcd)
            ocb = _rwkv_post(yfc, ybc, feats_c[9], feats_c[10], seg, post_vecs)
            hc = _outproj(hc, m_ctx, oca, ocb, ocn, ocd, w_out_b)

        h = _ffn(h, m_lat, norm_ffn2[l], w_gu2, w_dn2, 6)
        if ctx_out:
            hc = _ffn(hc, m_ctx, norm_ffn2[l], w_gu2, w_dn2, 6)
    return h
```

```python
import functools

import numpy as np
import jax
import jax.numpy as jnp
from jax import lax
from jax.experimental import pallas as pl
from jax.experimental.pallas import tpu as pltpu

F32 = jnp.float32
BF16 = jnp.bfloat16

GRID_W = 64
HEAD_DIM = 64
GROUP_W = 256
ROPE_THETA = 10000.0
NORM_EPS = 1e-6
D_FF = 2816
MLA_HEADS = 4
MLA_Q_RANK = 256
MLA_KV_RANK = 128
MLA_NOPE = 64
MLA_ROPE = 32
MLA_V = 64
MLA_QK = MLA_NOPE + MLA_ROPE
RWKV_COLS = 960
RWKV_LNX_EPS = 64e-5
NA_WIN_R = 8
NA_WIN_C = 16
MLA_COLS = 416
NA_COLS = 768
GQA_COLS = 512

LANES = 128
SUBLANES = 8
VMEM_LIMIT_BYTES = 56 * 1024 * 1024

UA_W = 512
UB_W = 1024
UN_W = 768
UD_W = 512

VT_ROWS = 128
CHUNK = 64
NEG_INF = -1e30
LOG2E = 1.4426950408889634
MAX_JUMP = 24.0


def _dot(a, b):
    return jnp.dot(a, b, preferred_element_type=F32)


def _dot_nt(a, b):
    return lax.dot_general(a, b, (((1,), (1,)), ((), ())), preferred_element_type=F32)


def _dot_tn(a, b):
    return lax.dot_general(a, b, (((0,), (0,)), ((), ())), preferred_element_type=F32)


def _split2(x):
    hi = x.astype(BF16)
    lo = (x - hi.astype(F32)).astype(BF16)
    return hi, lo


def _dot_exact_rhs(x, m):
    h1 = x.astype(BF16)
    r1 = x - h1.astype(F32)
    h2 = r1.astype(BF16)
    h3 = (r1 - h2.astype(F32)).astype(BF16)
    return _dot(h1, m) + _dot(h2, m) + _dot(h3, m)


def _dot_exact_lhs(m, x):
    h1 = x.astype(BF16)
    r1 = x - h1.astype(F32)
    h2 = r1.astype(BF16)
    h3 = (r1 - h2.astype(F32)).astype(BF16)
    return _dot(m, h1) + _dot(m, h2) + _dot(m, h3)


def _rms(x, eps=NORM_EPS):
    return x * lax.rsqrt(jnp.mean(x * x, axis=-1, keepdims=True) + eps)


def _cparams(sem):
    return pltpu.CompilerParams(dimension_semantics=sem, vmem_limit_bytes=VMEM_LIMIT_BYTES)


def _mod_kernel(c_ref, w_ref, b_ref, o_ref):
    c = c_ref[...]
    s = c * jax.nn.sigmoid(c)
    o_ref[...] = jnp.dot(s, w_ref[...], preferred_element_type=F32,
                         precision=lax.Precision.HIGHEST) + b_ref[...]


def _modulation(cvec, w_mod, b_mod):
    L, D, N = w_mod.shape
    tn = 1024
    return pl.pallas_call(
        _mod_kernel,
        grid=(L, N // tn),
        in_specs=[
            pl.BlockSpec((SUBLANES, D), lambda l, n: (0, 0)),
            pl.BlockSpec((None, D, tn), lambda l, n: (l, 0, n)),
            pl.BlockSpec((None, 1, tn), lambda l, n: (l, 0, n)),
        ],
        out_specs=pl.BlockSpec((None, SUBLANES, tn), lambda l, n: (l, 0, n)),
        out_shape=jax.ShapeDtypeStruct((L, SUBLANES, N), F32),
        compiler_params=_cparams(("parallel", "parallel")),
        name="adaln_mod",
    )(cvec, w_mod, b_mod.reshape(L, 1, N))


def _ffn_kernel(h_ref, mod_ref, g_ref, wg_ref, wu_ref, wd_ref, o_ref, *, base):
    x = h_ref[...]
    y = _rms(x) * g_ref[...]
    xn = (y * (1.0 + mod_ref[base + 1:base + 2, :]) + mod_ref[base:base + 1, :]).astype(BF16)
    g = _dot(xn, wg_ref[...])
    u = _dot(xn, wu_ref[...])
    a = (g * jax.nn.sigmoid(g) * u).astype(BF16)
    o_ref[...] = x + 0.5 * mod_ref[base + 2:base + 3, :] * _dot(a, wd_ref[...])


def _ffn(h, mod, gain, w_gu, w_down, base):
    B, T, D = h.shape
    F = w_down.shape[0]
    tm = min(T, 512)
    resident = pl.Buffered(1)
    return pl.pallas_call(
        functools.partial(_ffn_kernel, base=base),
        grid=(B, T // tm),
        in_specs=[
            pl.BlockSpec((None, tm, D), lambda b, i: (b, i, 0)),
            pl.BlockSpec((None, 9, D), lambda b, i: (b, 0, 0)),
            pl.BlockSpec((1, D), lambda b, i: (0, 0)),
            pl.BlockSpec((D, F), lambda b, i: (0, 0), pipeline_mode=resident),
            pl.BlockSpec((D, F), lambda b, i: (0, 1), pipeline_mode=resident),
            pl.BlockSpec((F, D), lambda b, i: (0, 0), pipeline_mode=resident),
        ],
        out_specs=pl.BlockSpec((None, tm, D), lambda b, i: (b, i, 0)),
        out_shape=jax.ShapeDtypeStruct((B, T, D), F32),
        compiler_params=_cparams(("parallel", "parallel")),
        name="swiglu_halfstep",
    )(h, mod, gain.reshape(1, D), w_gu, w_gu, w_down)


def _inproj_kernel(h_ref, mod_ref, g_ref, w_ref, oa_ref, ob_ref, on_ref, od_ref):
    y = _rms(h_ref[...]) * g_ref[...]
    xm = (y * (1.0 + mod_ref[4:5, :]) + mod_ref[3:4, :]).astype(BF16)
    o = 0
    for ref, w in ((oa_ref, UA_W), (ob_ref, UB_W), (on_ref, UN_W), (od_ref, UD_W)):
        ref[...] = _dot(xm, w_ref[:, o:o + w])
        o += w


def _inproj(h, mod, gain, w_in_p):
    B, T, D = h.shape
    tm = min(T, 512)
    W = w_in_p.shape[1]
    widths = (UA_W, UB_W, UN_W, UD_W)
    return pl.pallas_call(
        _inproj_kernel,
        grid=(B, T // tm),
        in_specs=[
            pl.BlockSpec((None, tm, D), lambda b, i: (b, i, 0)),
            pl.BlockSpec((None, 9, D), lambda b, i: (b, 0, 0)),
            pl.BlockSpec((1, D), lambda b, i: (0, 0)),
            pl.BlockSpec((D, W), lambda b, i: (0, 0)),
        ],
        out_specs=[pl.BlockSpec((None, tm, w), lambda b, i: (b, i, 0)) for w in widths],
        out_shape=[jax.ShapeDtypeStruct((B, T, w), F32) for w in widths],
        compiler_params=_cparams(("parallel", "parallel")),
        name="in_projection",
    )(h, mod, gain.reshape(1, D), w_in_p)


def _outproj_kernel(h_ref, mod_ref, oa_ref, ob_ref, on_ref, od_ref, w_ref, o_ref):
    acc = _dot(oa_ref[...], w_ref[0:GROUP_W, :])
    acc += _dot(ob_ref[...], w_ref[GROUP_W:2 * GROUP_W, :])
    acc += _dot(on_ref[...], w_ref[2 * GROUP_W:3 * GROUP_W, :])
    acc += _dot(od_ref[...], w_ref[3 * GROUP_W:4 * GROUP_W, :])
    o_ref[...] = h_ref[...] + mod_ref[5:6, :] * acc


def _outproj(h, mod, oa, ob, on, od, w_out):
    B, T, D = h.shape
    tm = min(T, 512)
    grp = pl.BlockSpec((None, tm, GROUP_W), lambda b, i: (b, i, 0))
    return pl.pallas_call(
        _outproj_kernel,
        grid=(B, T // tm),
        in_specs=[
            pl.BlockSpec((None, tm, D), lambda b, i: (b, i, 0)),
            pl.BlockSpec((None, 9, D), lambda b, i: (b, 0, 0)),
            grp, grp, grp, grp,
            pl.BlockSpec((4 * GROUP_W, D), lambda b, i: (0, 0)),
        ],
        out_specs=pl.BlockSpec((None, tm, D), lambda b, i: (b, i, 0)),
        out_shape=jax.ShapeDtypeStruct((B, T, D), F32),
        compiler_params=_cparams(("parallel", "parallel")),
        name="out_projection",
    )(h, mod, oa, ob, on, od, w_out)


def _rope_tables(positions, width, group, n_rep):
    half = group // 2
    inv_freq = ROPE_THETA ** (-jnp.arange(half, dtype=F32) / half)

    def axis(pos):
        ang = pos.astype(F32)[:, None] * inv_freq[None, :]
        c, s = jnp.cos(ang), jnp.sin(ang)
        return (jnp.concatenate([c, c], -1), jnp.concatenate([jnp.zeros_like(s), s], -1),
                jnp.concatenate([-s, jnp.zeros_like(s)], -1))

    row, col = positions
    parts = [jnp.concatenate([a, b], -1) for a, b in zip(axis(row), axis(col))]
    return [jnp.tile(p, (1, n_rep)) for p in parts]


def _pad_lanes(t, lo, total, fill):
    T = t.shape[0]
    return jnp.concatenate([jnp.full((T, lo), fill, F32), t,
                            jnp.full((T, total - lo - t.shape[1]), fill, F32)], -1)


def _vt_ext(v):
    tm = v.shape[0]
    lane = lax.broadcasted_iota(jnp.int32, (tm, VT_ROWS - HEAD_DIM), 1)
    aux = jnp.where(lane == 0, 1.0, 0.0)
    return jnp.concatenate([v, aux], axis=-1).T.astype(BF16)


def _mla_prep_kernel(ua_ref, c_ref, s1_ref, s2_ref, qnorm_ref, kvnorm_ref, qn_ref, kn_ref,
                     wuq_ref, wuk_ref, wuv_ref, q_out, k_out, vt_out):
    ua = ua_ref[...]
    qc = (_rms(ua[:, :MLA_Q_RANK]) * qnorm_ref[...]).astype(BF16)
    kvc = (_rms(ua[:, MLA_Q_RANK:MLA_Q_RANK + MLA_KV_RANK]) * kvnorm_ref[...]).astype(BF16)
    k_rope = ua[:, 3 * LANES:4 * LANES]
    q_all = _dot(qc, wuq_ref[...])
    k_all = _dot(kvc, wuk_ref[...])
    v_all = _dot(kvc, wuv_ref[...])
    cos, s_dn, s_up = c_ref[...], s1_ref[...], s2_ref[...]
    half = MLA_ROPE // 4

    def rope(x):
        return x * cos + pltpu.roll(x, half, 1) * s_dn + pltpu.roll(x, LANES - half, 1) * s_up

    def headnorm(x, gain):
        ms = jnp.sum(x * x, axis=-1, keepdims=True) * (1.0 / MLA_QK)
        return x * lax.rsqrt(ms + NORM_EPS) * gain

    for h in range(MLA_HEADS):
        qh = headnorm(q_all[:, h * LANES:(h + 1) * LANES], qn_ref[...])
        q_out[h] = rope(qh).astype(BF16)
        kh = headnorm(k_all[:, h * LANES:(h + 1) * LANES] + k_rope, kn_ref[...])
        k_out[h] = rope(kh).astype(BF16)
        vt_out[h] = _vt_ext(v_all[:, h * MLA_V:(h + 1) * MLA_V])


def _mla_prep(ua, tables, q_norm, kv_norm, qn_p, kn_p, wuq_p, wuk_p, wuv):
    B, T, _ = ua.shape
    tm = min(T, 512)
    tab = pl.BlockSpec((tm, LANES), lambda b, i: (i, 0))
    full = lambda a: pl.BlockSpec(a.shape, lambda b, i: (0,) * a.ndim)
    small = [q_norm, kv_norm, qn_p, kn_p, wuq_p, wuk_p, wuv]
    return pl.pallas_call(
        _mla_prep_kernel,
        grid=(B, T // tm),
        in_specs=[pl.BlockSpec((None, tm, UA_W), lambda b, i: (b, i, 0)), tab, tab, tab]
        + [full(a) for a in small],
        out_specs=[
            pl.BlockSpec((None, MLA_HEADS, tm, LANES), lambda b, i: (b, 0, i, 0)),
            pl.BlockSpec((None, MLA_HEADS, tm, LANES), lambda b, i: (b, 0, i, 0)),
            pl.BlockSpec((None, MLA_HEADS, VT_ROWS, tm), lambda b, i: (b, 0, 0, i)),
        ],
        out_shape=[
            jax.ShapeDtypeStruct((B, MLA_HEADS, T, LANES), BF16),
            jax.ShapeDtypeStruct((B, MLA_HEADS, T, LANES), BF16),
            jax.ShapeDtypeStruct((B, MLA_HEADS, VT_ROWS, T), BF16),
        ],
        compiler_params=_cparams(("parallel", "parallel")),
        name="mla_prep",
    )(ua, *tables, *small)


def _qkv_prep_kernel(*refs, q_off, nq, nk, nv, use_rope, v_rows, v_cols):
    n_in = 7 if use_rope else 4
    x_ref, seg_ref, gq_ref, gk_ref = refs[:4]
    if use_rope:
        c_ref, s1_ref, s2_ref = refs[4:7]
    q_out, k_out = refs[n_in:n_in + 2]
    v_outs = list(refs[n_in + 2:])
    v_out = v_outs.pop(0) if v_rows else None
    vt_out = v_outs.pop(0) if v_cols else None
    x = x_ref[...]
    wq, wk, wv = nq * HEAD_DIM, nk * HEAD_DIM, nv * HEAD_DIM
    q = x[:, q_off:q_off + wq]
    k = x[:, q_off + wq:q_off + wq + wk]
    v = x[:, q_off + wq + wk:q_off + wq + wk + wv]

    def headnorm(t, gain, w):
        hi, lo = _split2(t * t)
        seg = seg_ref[0:w, 0:w]
        ms = (_dot(hi, seg) + _dot(lo, seg)) * (1.0 / HEAD_DIM)
        return t * lax.rsqrt(ms + NORM_EPS) * gain

    q = headnorm(q, gq_ref[...], wq)
    k = headnorm(k, gk_ref[...], wk)
    if use_rope:
        half = HEAD_DIM // 4

        def rope(t, w):
            return (t * c_ref[:, 0:w] + pltpu.roll(t, half, 1) * s1_ref[:, 0:w]
                    + pltpu.roll(t, w - half, 1) * s2_ref[:, 0:w])

        q = rope(q, wq)
        k = rope(k, wk)
    for h in range(nq):
        q_out[h] = q[:, h * HEAD_DIM:(h + 1) * HEAD_DIM].astype(BF16)
    for h in range(nk):
        k_out[h] = k[:, h * HEAD_DIM:(h + 1) * HEAD_DIM].astype(BF16)
    for h in range(nv):
        vh = v[:, h * HEAD_DIM:(h + 1) * HEAD_DIM]
        if v_rows:
            v_out[h] = vh.astype(BF16)
        if v_cols:
            vt_out[h] = _vt_ext(vh)


def _qkv_prep(x, seg, gq, gk, nq, nk, nv, tables=None, v_rows=True, v_cols=False):
    B, T, W = x.shape
    tm = min(T, 512)
    use_rope = tables is not None
    full = lambda a: pl.BlockSpec(a.shape, lambda b, i: (0,) * a.ndim)
    in_specs = [pl.BlockSpec((None, tm, W), lambda b, i: (b, i, 0)), full(seg), full(gq), full(gk)]
    args = [x, seg, gq, gk]
    if use_rope:
        in_specs += [pl.BlockSpec((tm, tables[0].shape[1]), lambda b, i: (i, 0))] * 3
        args += list(tables)
    hm = lambda n: pl.BlockSpec((None, n, tm, HEAD_DIM), lambda b, i: (b, 0, i, 0))
    out_specs = [hm(nq), hm(nk)]
    out_shape = [jax.ShapeDtypeStruct((B, n, T, HEAD_DIM), BF16) for n in (nq, nk)]
    if v_rows:
        out_specs.append(hm(nv))
        out_shape.append(jax.ShapeDtypeStruct((B, nv, T, HEAD_DIM), BF16))
    if v_cols:
        out_specs.append(pl.BlockSpec((None, nv, VT_ROWS, tm), lambda b, i: (b, 0, 0, i)))
        out_shape.append(jax.ShapeDtypeStruct((B, nv, VT_ROWS, T), BF16))
    return pl.pallas_call(
        functools.partial(_qkv_prep_kernel, q_off=0, nq=nq, nk=nk, nv=nv, use_rope=use_rope,
                          v_rows=v_rows, v_cols=v_cols),
        grid=(B, T // tm),
        in_specs=in_specs,
        out_specs=out_specs,
        out_shape=out_shape,
        compiler_params=_cparams(("parallel", "parallel")),
        name="qkv_prep_rope" if use_rope else "qkv_prep",
    )(*args)


def _attn_kernel(*refs, n_heads, group, has_extra, nk):
    if has_extra:
        q_ref, k_ref, vt_ref, kx_ref, vtx_ref, o_ref, m_ref, acc_ref = refs
    else:
        q_ref, k_ref, vt_ref, o_ref, m_ref, acc_ref = refs
    j = pl.program_id(2)

    heads = range(n_heads)

    def first(kr, vr):
        s = [_dot_nt(kr[h // group], q_ref[h]) for h in heads]
        m = [jnp.max(s[h], axis=0, keepdims=True) for h in heads]
        p = [jnp.exp2(s[h] - m[h]).astype(BF16) for h in heads]
        for h in heads:
            m_ref[h] = m[h]
            acc_ref[h] = _dot(vr[h // group], p[h])

    def update(kr, vr):
        m_prev = [m_ref[h] for h in heads]
        s = [_dot_nt(kr[h // group], q_ref[h]) for h in heads]
        pv = [_dot(vr[h // group], jnp.exp2(s[h] - m_prev[h]).astype(BF16)) for h in heads]
        m_blk = [jnp.max(s[h], axis=0, keepdims=True) for h in heads]
        jump = functools.reduce(jnp.maximum, [m_blk[h] - m_prev[h] for h in heads])
        safe = jnp.max(jump) <= MAX_JUMP

        @pl.when(safe)
        def _():
            for h in heads:
                m_new = jnp.maximum(m_prev[h], m_blk[h])
                m_ref[h] = m_new
                acc_ref[h] = (acc_ref[h] + pv[h]) * jnp.exp2(m_prev[h] - m_new)

        @pl.when(jnp.logical_not(safe))
        def _():
            s2 = [_dot_nt(kr[h // group], q_ref[h]) for h in heads]
            m_new = [jnp.maximum(m_ref[h], jnp.max(s2[h], axis=0, keepdims=True)) for h in heads]
            p2 = [jnp.exp2(s2[h] - m_new[h]).astype(BF16) for h in heads]
            for h in heads:
                acc_ref[h] = jnp.exp2(m_ref[h] - m_new[h]) * acc_ref[h] + _dot(vr[h // group], p2[h])
                m_ref[h] = m_new[h]

    if has_extra:
        @pl.when(j == 0)
        def _():
            first(kx_ref, vtx_ref)

        update(k_ref, vt_ref)
    else:
        @pl.when(j == 0)
        def _():
            first(k_ref, vt_ref)

        @pl.when(j > 0)
        def _():
            update(k_ref, vt_ref)

    @pl.when(j == nk - 1)
    def _():
        outs = []
        for h in range(n_heads):
            a = acc_ref[h].T
            outs.append(a[:, :HEAD_DIM] / a[:, HEAD_DIM:HEAD_DIM + 1])
        o_ref[...] = jnp.concatenate(outs, axis=-1).astype(o_ref.dtype)


def _attention(q, k, vt, kx=None, vtx=None):
    B, H, Tq, dq = q.shape
    Hk, Tk = k.shape[1], k.shape[2]
    group = H // Hk
    tq = min(Tq, 512)
    tk = min(Tk, 1024)
    nk = Tk // tk
    has_extra = kx is not None
    in_specs = [
        pl.BlockSpec((None, H, tq, dq), lambda b, i, j: (b, 0, i, 0)),
        pl.BlockSpec((None, Hk, tk, dq), lambda b, i, j: (b, 0, j, 0)),
        pl.BlockSpec((None, Hk, VT_ROWS, tk), lambda b, i, j: (b, 0, 0, j)),
    ]
    args = [q, k, vt]
    if has_extra:
        Tx = kx.shape[2]
        in_specs += [pl.BlockSpec((None, Hk, Tx, dq), lambda b, i, j: (b, 0, 0, 0)),
                     pl.BlockSpec((None, Hk, VT_ROWS, Tx), lambda b, i, j: (b, 0, 0, 0))]
        args += [kx, vtx]
    return pl.pallas_call(
        functools.partial(_attn_kernel, n_heads=H, group=group, has_extra=has_extra, nk=nk),
        grid=(B, Tq // tq, nk),
        in_specs=in_specs,
        out_specs=pl.BlockSpec((None, tq, H * HEAD_DIM), lambda b, i, j: (b, i, 0)),
        out_shape=jax.ShapeDtypeStruct((B, Tq, H * HEAD_DIM), BF16),
        scratch_shapes=[pltpu.VMEM((H, 1, tq), F32), pltpu.VMEM((H, VT_ROWS, tq), F32)],
        compiler_params=_cparams(("parallel", "parallel", "arbitrary")),
        name="flash_attention_ctx" if has_extra else "flash_attention",
    )(*args)


def _na_bias_kernel(r_ref, e_ref, m_ref, o_ref):
    o_ref[...] = _dot_exact_rhs(r_ref[...], e_ref[...]) + m_ref[...]


def _na_bias_tables(rpb):
    H, n_dr, n_dc = rpb.shape
    W = GRID_W
    qc = np.arange(W)[:, None]
    kc = np.arange(W)[None, :]
    c0 = np.clip(qc - NA_WIN_C // 2, 0, W - NA_WIN_C)
    valid = (kc >= c0) & (kc < c0 + NA_WIN_C)
    d = kc - qc + NA_WIN_C - 1
    onehot = np.zeros((LANES, W * W), np.float32)
    for a in range(W):
        for b in range(W):
            if valid[a, b]:
                onehot[d[a, b], a * W + b] = 1.0
    mask = np.where(valid, 0.0, NEG_INF).astype(np.float32).reshape(1, W * W)
    rows = H * n_dr
    rows_p = -(-rows // SUBLANES) * SUBLANES
    r2 = jnp.zeros((rows_p, LANES), F32).at[:rows, :n_dc].set(rpb.reshape(rows, n_dc) * LOG2E)
    tn = 1024
    tiles = pl.pallas_call(
        _na_bias_kernel,
        grid=(W * W // tn,),
        in_specs=[pl.BlockSpec((rows_p, LANES), lambda n: (0, 0)),
                  pl.BlockSpec((LANES, tn), lambda n: (0, n)),
                  pl.BlockSpec((1, tn), lambda n: (0, n))],
        out_specs=pl.BlockSpec((rows_p, tn), lambda n: (0, n)),
        out_shape=jax.ShapeDtypeStruct((rows_p, W * W), F32),
        compiler_params=_cparams(("parallel",)),
        name="na_bias_expand",
    )(r2, jnp.asarray(onehot, BF16), jnp.asarray(mask))
    tiles = tiles[:rows].reshape(H, n_dr, W, W)
    slabs = [tiles[:, base:base + NA_WIN_R].transpose(0, 2, 1, 3).reshape(H, W, NA_WIN_R * W)
             for base in range(NA_WIN_R)]
    return jnp.stack(slabs, 0)


def _na_kernel(q_ref, k_ref, v_ref, kc_ref, vc_ref, tb_ref, o_ref, *, rb, rows, n_heads):
    blk = pl.program_id(1)
    W = GRID_W
    win = NA_WIN_R * W

    def row_body(i, carry):
        r = blk * rb + i
        r0 = jnp.clip(r - NA_WIN_R // 2, 0, rows - NA_WIN_R)
        base = r0 - r + NA_WIN_R - 1
        k0 = pl.multiple_of(r0 * W, W)
        q0 = pl.multiple_of(i * W, W)
        heads = range(n_heads)
        q = [q_ref[h, pl.ds(q0, W), :] for h in heads]
        s_nb = [_dot_nt(q[h], k_ref[h, pl.ds(k0, win), :]) + tb_ref[base, h] for h in heads]
        s_cx = [_dot_nt(q[h], kc_ref[h]) for h in heads]
        m = [jnp.maximum(jnp.max(s_nb[h], axis=-1, keepdims=True),
                         jnp.max(s_cx[h], axis=-1, keepdims=True)) for h in heads]
        p_nb = [jnp.exp2(s_nb[h] - m[h]) for h in heads]
        p_cx = [jnp.exp2(s_cx[h] - m[h]) for h in heads]
        l = [jnp.sum(p_nb[h], axis=-1, keepdims=True) + jnp.sum(p_cx[h], axis=-1, keepdims=True)
             for h in heads]
        o = [_dot(p_nb[h].astype(BF16), v_ref[h, pl.ds(k0, win), :])
             + _dot(p_cx[h].astype(BF16), vc_ref[h]) for h in heads]
        o_ref[pl.ds(q0, W), :] = jnp.concatenate(
            [o[h] / l[h] for h in heads], axis=-1).astype(o_ref.dtype)
        return carry

    lax.fori_loop(0, rb, row_body, 0)


def _na_attention(q, k, v, kc, vc, tb):
    B, H, S, dh = q.shape
    L = kc.shape[2]
    rows = S // GRID_W
    assert rows >= NA_WIN_R
    rb = 8
    return pl.pallas_call(
        functools.partial(_na_kernel, rb=rb, rows=rows, n_heads=H),
        grid=(B, rows // rb),
        in_specs=[
            pl.BlockSpec((None, H, rb * GRID_W, dh), lambda b, i: (b, 0, i, 0)),
            pl.BlockSpec((None, H, S, dh), lambda b, i: (b, 0, 0, 0)),
            pl.BlockSpec((None, H, S, dh), lambda b, i: (b, 0, 0, 0)),
            pl.BlockSpec((None, H, L, dh), lambda b, i: (b, 0, 0, 0)),
            pl.BlockSpec((None, H, L, dh), lambda b, i: (b, 0, 0, 0)),
            pl.BlockSpec(tb.shape, lambda b, i: (0, 0, 0, 0)),
        ],
        out_specs=pl.BlockSpec((None, rb * GRID_W, H * dh), lambda b, i: (b, i, 0)),
        out_shape=jax.ShapeDtypeStruct((B, S, H * dh), BF16),
        compiler_params=_cparams(("parallel", "arbitrary")),
        name="neighbourhood_attention",
    )(q, k, v, kc, vc, tb)


def _rwkv_feat_kernel(x_ref, xp_ref, xn_ref, mu_ref, seg_ref, w2_ref, a2_ref, g2_ref, vec_ref,
                      r_out, v_out, kk_out, lwf_out, lwb_out, kdf_out, kdb_out, af_out, ab_out,
                      bon_out, g_out, *, nblk):
    i = pl.program_id(1)
    x = x_ref[...]
    tm = x.shape[0]
    has_prev = jnp.where(i > 0, 1.0, 0.0)
    has_next = jnp.where(i < nblk - 1, 1.0, 0.0)
    row = lax.broadcasted_iota(jnp.int32, (tm, 1), 0)
    prev = jnp.where(row == 0, xp_ref[SUBLANES - 1:SUBLANES, :] * has_prev, pltpu.roll(x, 1, 0))
    nxt = jnp.where(row == tm - 1, xn_ref[0:1, :] * has_next, pltpu.roll(x, tm - 1, 0))
    xs = x + mu_ref[0:1, :] * (prev - x) + mu_ref[1:2, :] * (nxt - x)

    G = GROUP_W
    r, k, v = xs[:, 0:G], xs[:, G:2 * G], xs[:, 2 * G:3 * G]
    lora = xs[:, 3 * G:4 * G]
    seg = seg_ref[...]

    def segsum(t):
        hi, lo = _split2(t)
        return _dot(hi, seg) + _dot(lo, seg)

    k_k, k_a, r_k = vec_ref[0:1, :], vec_ref[1:2, :], vec_ref[2:3, :]
    kk = k * k_k
    kk = kk * lax.rsqrt(segsum(kk * kk) + 1e-12)
    g_out[...] = _dot(jax.nn.sigmoid(lora).astype(BF16), g2_ref[...])
    tanh_l = jnp.tanh(lora).astype(BF16)
    lora_b = lora.astype(BF16)
    kd_sum = None
    for d, (lw_out, kd_out, a_out) in enumerate(((lwf_out, kdf_out, af_out), (lwb_out, kdb_out, ab_out))):
        w0, a0 = vec_ref[3 + d:4 + d, :], vec_ref[5 + d:6 + d, :]
        z = w0 + _dot(tanh_l, w2_ref[d])
        w_log = -(jnp.maximum(-z, 0.0) + jnp.log(1.0 + jnp.exp(-jnp.abs(z)))) - 0.5
        lw_out[...] = -jnp.exp(w_log)
        a = jax.nn.sigmoid(a0 + _dot(lora_b, a2_ref[d]))
        kd = k * (1.0 + (a - 1.0) * k_a)
        kd_out[...] = kd
        a_out[...] = a
        kd_sum = kd if kd_sum is None else kd_sum + kd
    r_out[...] = r
    v_out[...] = v
    kk_out[...] = kk
    bon_out[...] = segsum(r * kd_sum * r_k) * v


def _rwkv_features(ub, mu_p, seg, w2_p, a2_p, g2_p, vecs):
    B, T, W = ub.shape
    tm = min(T, 256)
    nblk = T // tm
    hb = tm // SUBLANES
    nh = T // SUBLANES
    full = lambda a: pl.BlockSpec(a.shape, lambda b, i: (0,) * a.ndim)
    out = pl.BlockSpec((None, tm, GROUP_W), lambda b, i: (b, i, 0))
    return pl.pallas_call(
        functools.partial(_rwkv_feat_kernel, nblk=nblk),
        grid=(B, nblk),
        in_specs=[
            pl.BlockSpec((None, tm, W), lambda b, i: (b, i, 0)),
            pl.BlockSpec((None, SUBLANES, W), lambda b, i: (b, jnp.maximum(i * hb - 1, 0), 0)),
            pl.BlockSpec((None, SUBLANES, W), lambda b, i: (b, jnp.minimum((i + 1) * hb, nh - 1), 0)),
            full(mu_p), full(seg), full(w2_p), full(a2_p), full(g2_p), full(vecs),
        ],
        out_specs=[out] * 11,
        out_shape=[jax.ShapeDtypeStruct((B, T, GROUP_W), F32)] * 11,
        compiler_params=_cparams(("parallel", "parallel")),
        name="rwkv_features",
    )(ub, ub, ub, mu_p, seg, w2_p, a2_p, g2_p, vecs)


def _rwkv_chunk_kernel(r_ref, v_ref, kk_ref, lwf_ref, lwb_ref, kdf_ref, kdb_ref, af_ref, ab_ref,
                       cum_ref, tri_ref, hm_ref,
                       rh_out, y1_out, mp_out, np_out, pc_out, *, nsub):
    C = CHUNK
    hm = hm_ref[...]
    eye = tri_ref[2]
    lw_refs, kd_refs, a_refs = (lwf_ref, lwb_ref), (kdf_ref, kdb_ref), (af_ref, ab_ref)

    def stack(t):
        return (jnp.concatenate([t, t, t, t], axis=0) * hm).astype(BF16)

    def collapse(t):
        return t[0:C] + t[C:2 * C] + t[2 * C:3 * C] + t[3 * C:4 * C]

    chains = [(ci, d) for ci in range(nsub) for d in range(2)]
    n = len(chains)
    rows = [slice(ci * C, (ci + 1) * C) for ci, _ in chains]

    rt, a4, r4, b4, k4, bh4, kh4, v4, e_end = ([None] * n for _ in range(9))
    for i, (ci, d) in enumerate(chains):
        rs = rows[i]
        r, v, kk = r_ref[rs, :], v_ref[rs, :], kk_ref[rs, :]
        lw, kd, a = lw_refs[d][rs, :], kd_refs[d][rs, :], a_refs[d][rs, :]
        cl = _dot_exact_lhs(cum_ref[d], lw)
        last = C - 1 if d == 0 else 0
        e_end[i] = jnp.exp(cl[last:last + 1, :])
        e_neg = jnp.exp(-cl)
        bt = kk * a * e_neg
        kt = kd * e_neg
        rt[i] = r * jnp.exp(cl)
        a4[i], r4[i], b4[i], k4[i] = stack(-kk * jnp.exp(cl - lw)), stack(rt[i]), stack(bt), stack(kt)
        bh4[i], kh4[i], v4[i] = stack(bt * e_end[i]), stack(kt * e_end[i]), stack(v)

    strict = [tri_ref[d] for _, d in chains]
    a_ab = [_dot_nt(a4[i], b4[i]) * strict[i] for i in range(n)]
    a_ak = [(_dot_nt(a4[i], k4[i]) * strict[i]).astype(BF16) for i in range(n)]
    a_rb = [(_dot_nt(r4[i], b4[i]) * (strict[i] + eye)).astype(BF16) for i in range(n)]
    a_rk = [(_dot_nt(r4[i], k4[i]) * (strict[i] + eye)).astype(BF16) for i in range(n)]

    pw = [t.astype(BF16) for t in a_ab]
    tm = [eye + t for t in a_ab]
    for _ in range(5):
        pw = [_dot(pw[i], pw[i]).astype(BF16) for i in range(n)]
        tm = [tm[i] + _dot(tm[i].astype(BF16), pw[i]) for i in range(n)]
    tm = [t.astype(BF16) for t in tm]

    x4 = [_dot(a_ak[i], v4[i]).astype(BF16) for i in range(n)]
    wt4 = [_dot(tm[i], a4[i]).astype(BF16) for i in range(n)]
    u04 = [_dot(tm[i], x4[i]).astype(BF16) for i in range(n)]
    rh4 = [_dot(a_rb[i], wt4[i]) for i in range(n)]
    y14 = [_dot(a_rb[i], u04[i]) + _dot(a_rk[i], v4[i]) for i in range(n)]
    for i, (ci, d) in enumerate(chains):
        rh_out[d, rows[i], :] = (rt[i] + collapse(rh4[i])).astype(BF16)
        y1_out[d, rows[i], :] = collapse(y14[i])
        mp_out[d, ci] = _dot_tn(wt4[i], bh4[i]).astype(BF16)
        np_out[d, ci] = _dot_tn(u04[i], bh4[i]) + _dot_tn(v4[i], kh4[i])
        pc_out[d, ci] = jnp.broadcast_to(e_end[i], (SUBLANES, GROUP_W))


def _rwkv_chunks(feats, cum, tri, hm):
    r = feats[0]
    B, T, G = r.shape
    nch = T // CHUNK
    nsub = 2 if nch % 2 == 0 else 1
    tb = nsub * CHUNK
    tok = pl.BlockSpec((None, tb, G), lambda b, c: (b, c, 0))
    full = lambda a: pl.BlockSpec(a.shape, lambda b, c: (0,) * a.ndim)
    return pl.pallas_call(
        functools.partial(_rwkv_chunk_kernel, nsub=nsub),
        grid=(B, nch // nsub),
        in_specs=[tok] * 9 + [full(cum), full(tri), full(hm)],
        out_specs=[
            pl.BlockSpec((None, 2, tb, G), lambda b, c: (b, 0, c, 0)),
            pl.BlockSpec((None, 2, tb, G), lambda b, c: (b, 0, c, 0)),
            pl.BlockSpec((None, 2, nsub, G, G), lambda b, c: (b, 0, c, 0, 0)),
            pl.BlockSpec((None, 2, nsub, G, G), lambda b, c: (b, 0, c, 0, 0)),
            pl.BlockSpec((None, 2, nsub, SUBLANES, G), lambda b, c: (b, 0, c, 0, 0)),
        ],
        out_shape=[
            jax.ShapeDtypeStruct((B, 2, T, G), BF16),
            jax.ShapeDtypeStruct((B, 2, T, G), F32),
            jax.ShapeDtypeStruct((B, 2, nch, G, G), BF16),
            jax.ShapeDtypeStruct((B, 2, nch, G, G), F32),
            jax.ShapeDtypeStruct((B, 2, nch, SUBLANES, G), F32),
        ],
        compiler_params=_cparams(("parallel", "parallel")),
        name="rwkv_chunk_transitions",
    )(*feats, cum, tri, hm)


def _rwkv_scan_kernel(s0_ref, rhf_ref, y1f_ref, mpf_ref, npf_ref, pcf_ref,
                      rhb_ref, y1b_ref, mpb_ref, npb_ref, pcb_ref,
                      yf_out, yb_out, s_out, *, batch):
    c = pl.program_id(0)

    @pl.when(c == 0)
    def _():
        s_out[...] = s0_ref[...]

    dirs = ((rhf_ref, y1f_ref, mpf_ref, npf_ref, pcf_ref, yf_out),
            (rhb_ref, y1b_ref, mpb_ref, npb_ref, pcb_ref, yb_out))
    for b in range(batch):
        for d, (rh_ref, y1_ref, mp_ref, np_ref, pc_ref, y_out) in enumerate(dirs):
            s = s_out[b, d]
            s_b = s.astype(BF16)
            y_out[b] = _dot_nt(rh_ref[b], s_b) + y1_ref[b]
            s_out[b, d] = s * pc_ref[b, 0:1, :] + _dot(s_b, mp_ref[b]) + np_ref[b]


def _rwkv_scan(s0, rh, y1, mp, npm, pc):
    B, _, T, G = rh.shape
    nch = T // CHUNK
    fwd = lambda c: c
    bwd = lambda c: nch - 1 - c

    def specs(order, d):
        return [
            pl.BlockSpec((B, None, CHUNK, G), lambda c: (0, d, order(c), 0)),
            pl.BlockSpec((B, None, CHUNK, G), lambda c: (0, d, order(c), 0)),
            pl.BlockSpec((B, None, None, G, G), lambda c: (0, d, order(c), 0, 0)),
            pl.BlockSpec((B, None, None, G, G), lambda c: (0, d, order(c), 0, 0)),
            pl.BlockSpec((B, None, None, SUBLANES, G), lambda c: (0, d, order(c), 0, 0)),
        ]

    return pl.pallas_call(
        functools.partial(_rwkv_scan_kernel, batch=B),
        grid=(nch,),
        in_specs=[pl.BlockSpec(s0.shape, lambda c: (0, 0, 0, 0))] + specs(fwd, 0) + specs(bwd, 1),
        out_specs=[
            pl.BlockSpec((B, CHUNK, G), lambda c: (0, fwd(c), 0)),
            pl.BlockSpec((B, CHUNK, G), lambda c: (0, bwd(c), 0)),
            pl.BlockSpec(s0.shape, lambda c: (0, 0, 0, 0)),
        ],
        out_shape=[
            jax.ShapeDtypeStruct((B, T, G), F32),
            jax.ShapeDtypeStruct((B, T, G), F32),
            jax.ShapeDtypeStruct(s0.shape, F32),
        ],
        compiler_params=_cparams(("arbitrary",)),
        name="rwkv_state_scan",
    )(s0, rh, y1, mp, npm, pc, rh, y1, mp, npm, pc)


def _rwkv_post_kernel(yf_ref, yb_ref, bon_ref, g_ref, seg_ref, vec_ref, o_ref):
    y = yf_ref[...] + yb_ref[...]
    seg = seg_ref[...]

    def segmean(t):
        hi, lo = _split2(t)
        return (_dot(hi, seg) + _dot(lo, seg)) * (1.0 / HEAD_DIM)

    mu = segmean(y)
    yc = y - mu
    var = segmean(yc * yc)
    yn = yc * lax.rsqrt(var + RWKV_LNX_EPS) * vec_ref[0:1, :] + vec_ref[1:2, :]
    o_ref[...] = ((yn + bon_ref[...]) * g_ref[...]).astype(o_ref.dtype)


def _rwkv_post(yf, yb, bon, g, seg, vecs):
    B, T, G = yf.shape
    tm = min(T, 512)
    tok = pl.BlockSpec((None, tm, G), lambda b, i: (b, i, 0))
    full = lambda a: pl.BlockSpec(a.shape, lambda b, i: (0,) * a.ndim)
    return pl.pallas_call(
        _rwkv_post_kernel,
        grid=(B, T // tm),
        in_specs=[tok, tok, tok, tok, full(seg), full(vecs)],
        out_specs=tok,
        out_shape=jax.ShapeDtypeStruct((B, T, G), BF16),
        compiler_params=_cparams(("parallel", "parallel")),
        name="rwkv_output",
    )(yf, yb, bon, g, seg, vecs)


def _seg_matrix(width):
    i = np.arange(width) // HEAD_DIM
    return jnp.asarray((i[:, None] == i[None, :]).astype(np.float32), BF16)


def _scan_constants():
    C = CHUNK
    t = np.arange(4 * C)
    head, tok = t // C, t % C
    same = head[:, None] == head[None, :]
    lower = same & (tok[:, None] > tok[None, :])
    upper = same & (tok[:, None] < tok[None, :])
    eye = np.eye(4 * C, dtype=bool)
    tri = jnp.asarray(np.stack([lower, upper, eye]).astype(np.float32))
    i = np.arange(C)
    cum = jnp.asarray(np.stack([i[:, None] >= i[None, :], i[:, None] <= i[None, :]]).astype(np.float32), BF16)
    hm = jnp.asarray((head[:, None] == (np.arange(GROUP_W) // HEAD_DIM)[None, :]).astype(np.float32))
    return cum, tri, hm


def _pad_vec(v, lo, total):
    return jnp.zeros((total,), F32).at[lo:lo + v.shape[0]].set(v)


def kernel(x, c, ctx, c_ctx, w_mod, b_mod, norm_ffn1, ffn1_w_gu, ffn1_w_down, norm_mix, w_in, w_out, mla_q_norm, mla_kv_norm, mla_w_uq, mla_w_ukv, mla_qn, mla_kn, rwkv_shift, rwkv_w0, rwkv_w2, rwkv_a0, rwkv_a2, rwkv_g2, rwkv_k_k, rwkv_k_a, rwkv_r_k, rwkv_lnx_g, rwkv_lnx_b, na_qn, na_kn, na_rpb, gqa_qn, gqa_kn, norm_ffn2, ffn2_w_gu, ffn2_w_down):
    B, S, D = x.shape
    Lc = ctx.shape[1]
    depth = w_mod.shape[0]
    assert B + 1 <= SUBLANES and S % 512 == 0 and Lc % CHUNK == 0 and S % GRID_W == 0

    cvec = jnp.zeros((SUBLANES, D), F32).at[:B].set(c).at[B].set(c_ctx)
    mods = _modulation(cvec, w_mod, b_mod).reshape(depth, SUBLANES, 9, D)

    t = jnp.arange(S)
    pos = (t // GRID_W, t % GRID_W)
    zero = jnp.zeros((Lc,), jnp.int32)
    mla_tab, mla_tab_c = [
        [_pad_lanes(p, MLA_NOPE, LANES, fill) for p, fill in zip(_rope_tables(pp, MLA_ROPE, MLA_ROPE // 2, 1), (1.0, 0.0, 0.0))]
        for pp in (pos, (zero, zero))]
    gqa_tab, gqa_tab_c = [_rope_tables(pp, 4 * HEAD_DIM, HEAD_DIM // 2, 4) for pp in (pos, (zero, zero))]

    seg = _seg_matrix(GROUP_W)
    cum, tri, hm = _scan_constants()
    mla_scale = MLA_QK ** -0.5 * LOG2E
    dh_scale = HEAD_DIM ** -0.5 * LOG2E

    h, hc = x, ctx
    for l in range(depth):
        ctx_out = l < depth - 1
        m_lat = mods[l, :B]
        m_ctx = jnp.broadcast_to(mods[l, B], (B, 9, D))

        w_gu1, w_dn1 = ffn1_w_gu[l].astype(BF16), ffn1_w_down[l].astype(BF16)
        w_gu2, w_dn2 = ffn2_w_gu[l].astype(BF16), ffn2_w_down[l].astype(BF16)
        wi = w_in[l]
        zc = lambda n: jnp.zeros((D, n), F32)
        o_b = MLA_COLS
        o_n = o_b + RWKV_COLS
        o_d = o_n + NA_COLS
        w_in_p = jnp.concatenate([
            wi[:, :MLA_Q_RANK + MLA_KV_RANK], zc(64), wi[:, MLA_Q_RANK + MLA_KV_RANK:MLA_COLS], zc(32),
            wi[:, o_b:o_n], zc(UB_W - RWKV_COLS),
            wi[:, o_n:o_d],
            wi[:, o_d:o_d + GQA_COLS]], axis=1).astype(BF16)
        w_out_b = w_out[l].astype(BF16)

        wuq = mla_w_uq[l].reshape(MLA_Q_RANK, MLA_HEADS, MLA_QK)
        wuq_p = jnp.pad(wuq, ((0, 0), (0, 0), (0, LANES - MLA_QK))).reshape(MLA_Q_RANK, MLA_HEADS * LANES).astype(BF16)
        wukv = mla_w_ukv[l].reshape(MLA_KV_RANK, MLA_HEADS, MLA_NOPE + MLA_V)
        wuk_p = jnp.pad(wukv[..., :MLA_NOPE], ((0, 0), (0, 0), (0, LANES - MLA_NOPE))).reshape(MLA_KV_RANK, MLA_HEADS * LANES).astype(BF16)
        wuv = wukv[..., MLA_NOPE:].reshape(MLA_KV_RANK, MLA_HEADS * MLA_V).astype(BF16)
        mla_small = (mla_q_norm[l].reshape(1, -1), mla_kv_norm[l].reshape(1, -1),
                     _pad_vec(mla_qn[l] * mla_scale, 0, LANES).reshape(1, LANES),
                     _pad_vec(mla_kn[l], 0, LANES).reshape(1, LANES), wuq_p, wuk_p, wuv)

        na_gq = jnp.tile(na_qn[l] * dh_scale, 4).reshape(1, -1)
        na_gk = jnp.tile(na_kn[l], 4).reshape(1, -1)
        gqa_gq = jnp.tile(gqa_qn[l] * dh_scale, 4).reshape(1, -1)
        gqa_gk = jnp.tile(gqa_kn[l], 2).reshape(1, -1)
        na_tb = _na_bias_tables(na_rpb[l])

        mu_p = jnp.zeros((SUBLANES, UB_W), F32).at[:2, :RWKV_COLS].set(rwkv_shift[l])
        lo = 3 * GROUP_W

        def lora_w(w, off):
            return jnp.zeros((GROUP_W, GROUP_W), F32).at[off:off + w.shape[0]].set(w).astype(BF16)

        w2_p = jnp.stack([lora_w(rwkv_w2[l, 0], 0), lora_w(rwkv_w2[l, 1], 32)])
        a2_p = jnp.stack([lora_w(rwkv_a2[l, 0], 64), lora_w(rwkv_a2[l, 1], 96)])
        g2_p = lora_w(rwkv_g2[l], 128)
        feat_vecs = jnp.zeros((SUBLANES, GROUP_W), F32).at[0].set(rwkv_k_k[l]).at[1].set(rwkv_k_a[l]) \
            .at[2].set(rwkv_r_k[l].reshape(-1)).at[3:5].set(rwkv_w0[l]).at[5:7].set(rwkv_a0[l])
        post_vecs = jnp.zeros((SUBLANES, GROUP_W), F32).at[0].set(rwkv_lnx_g[l]).at[1].set(rwkv_lnx_b[l])

        h = _ffn(h, m_lat, norm_ffn1[l], w_gu1, w_dn1, 0)
        hc = _ffn(hc, m_ctx, norm_ffn1[l], w_gu1, w_dn1, 0)

        ua, ub, un, ud = _inproj(h, m_lat, norm_mix[l], w_in_p)
        uca, ucb, ucn, ucd = _inproj(hc, m_ctx, norm_mix[l], w_in_p)

        qa, ka, vta = _mla_prep(ua, mla_tab, *mla_small)
        qca, kca, vtca = _mla_prep(uca, mla_tab_c, *mla_small)
        oa = _attention(qa, ka, vta, kca, vtca)

        qn_, kn_, vn_ = _qkv_prep(un, seg, na_gq, na_gk, 4, 4, 4)
        qcn, kcn, vcn, vtcn = _qkv_prep(ucn, seg, na_gq, na_gk, 4, 4, 4, v_cols=True)
        on = _na_attention(qn_, kn_, vn_, kcn, vcn, na_tb)

        qd, kd, vtd = _qkv_prep(ud, seg, gqa_gq, gqa_gk, 4, 2, 2, gqa_tab, v_rows=False, v_cols=True)
        qcd, kcd, vtcd = _qkv_prep(ucd, seg, gqa_gq, gqa_gk, 4, 2, 2, gqa_tab_c, v_rows=False, v_cols=True)
        od = _attention(qd, kd, vtd, kcd, vtcd)

        feats_c = _rwkv_features(ucb, mu_p, seg, w2_p, a2_p, g2_p, feat_vecs)
        feats = _rwkv_features(ub, mu_p, seg, w2_p, a2_p, g2_p, feat_vecs)
        trans_c = _rwkv_chunks(feats_c[:9], cum, tri, hm)
        trans = _rwkv_chunks(feats[:9], cum, tri, hm)
        s0 = jnp.zeros((B, 2, GROUP_W, GROUP_W), F32)
        yfc, ybc, s_ctx = _rwkv_scan(s0, *trans_c)
        yf, yb, _ = _rwkv_scan(s_ctx, *trans)
        ob = _rwkv_post(yf, yb, feats[9], feats[10], seg, post_vecs)

        h = _outproj(h, m_lat, oa, ob, on, od, w_out_b)
        if ctx_out:
            oca = _attention(qca, kca, vtca)
            ocn = _attention(qcn, kcn, vtcn)
            ocd = _attention(qcd, kcd, vtcd)
            ocb = _rwkv_post(yfc, ybc, feats_c[9], feats_c[10], seg, post_vecs)
            hc = _outproj(hc, m_ctx, oca, ocb, ocn, ocd, w_out_b)

        h = _ffn(h, m_lat, norm_ffn2[l], w_gu2, w_dn2, 6)
        if ctx_out:
            hc = _ffn(hc, m_ctx, norm_ffn2[l], w_gu2, w_dn2, 6)
    return h
```

```python
import functools

import numpy as np
import jax
import jax.numpy as jnp
from jax import lax
from jax.experimental import pallas as pl
from jax.experimental.pallas import tpu as pltpu

F32 = jnp.float32
BF16 = jnp.bfloat16

GRID_W = 64
HEAD_DIM = 64
GROUP_W = 256
ROPE_THETA = 10000.0
NORM_EPS = 1e-6
D_FF = 2816
MLA_HEADS = 4
MLA_Q_RANK = 256
MLA_KV_RANK = 128
MLA_NOPE = 64
MLA_ROPE = 32
MLA_V = 64
MLA_QK = MLA_NOPE + MLA_ROPE
RWKV_COLS = 960
RWKV_LNX_EPS = 64e-5
NA_WIN_R = 8
NA_WIN_C = 16
MLA_COLS = 416
NA_COLS = 768
GQA_COLS = 512

LANES = 128
SUBLANES = 8
VMEM_LIMIT_BYTES = 56 * 1024 * 1024

UA_W = 512
UB_W = 1024
UN_W = 768
UD_W = 512

VT_ROWS = 128
CHUNK = 64
NEG_INF = -1e30
LOG2E = 1.4426950408889634
MAX_JUMP = 24.0


def _dot(a, b):
    return jnp.dot(a, b, preferred_element_type=F32)


def _dot_nt(a, b):
    return lax.dot_general(a, b, (((1,), (1,)), ((), ())), preferred_element_type=F32)


def _dot_tn(a, b):
    return lax.dot_general(a, b, (((0,), (0,)), ((), ())), preferred_element_type=F32)


def _split2(x):
    hi = x.astype(BF16)
    lo = (x - hi.astype(F32)).astype(BF16)
    return hi, lo


def _dot_exact_rhs(x, m):
    h1 = x.astype(BF16)
    r1 = x - h1.astype(F32)
    h2 = r1.astype(BF16)
    h3 = (r1 - h2.astype(F32)).astype(BF16)
    return _dot(h1, m) + _dot(h2, m) + _dot(h3, m)


def _dot_exact_lhs(m, x):
    h1 = x.astype(BF16)
    r1 = x - h1.astype(F32)
    h2 = r1.astype(BF16)
    h3 = (r1 - h2.astype(F32)).astype(BF16)
    return _dot(m, h1) + _dot(m, h2) + _dot(m, h3)


def _rms(x, eps=NORM_EPS):
    return x * lax.rsqrt(jnp.mean(x * x, axis=-1, keepdims=True) + eps)


def _cparams(sem):
    return pltpu.CompilerParams(dimension_semantics=sem, vmem_limit_bytes=VMEM_LIMIT_BYTES)


def _mod_kernel(c_ref, w_ref, b_ref, o_ref):
    c = c_ref[...]
    s = c * jax.nn.sigmoid(c)
    o_ref[...] = jnp.dot(s, w_ref[...], preferred_element_type=F32,
                         precision=lax.Precision.HIGHEST) + b_ref[...]


def _modulation(cvec, w_mod, b_mod):
    L, D, N = w_mod.shape
    tn = 1024
    return pl.pallas_call(
        _mod_kernel,
        grid=(L, N // tn),
        in_specs=[
            pl.BlockSpec((SUBLANES, D), lambda l, n: (0, 0)),
            pl.BlockSpec((None, D, tn), lambda l, n: (l, 0, n)),
            pl.BlockSpec((None, 1, tn), lambda l, n: (l, 0, n)),
        ],
        out_specs=pl.BlockSpec((None, SUBLANES, tn), lambda l, n: (l, 0, n)),
        out_shape=jax.ShapeDtypeStruct((L, SUBLANES, N), F32),
        compiler_params=_cparams(("parallel", "parallel")),
        name="adaln_mod",
    )(cvec, w_mod, b_mod.reshape(L, 1, N))


def _ffn_kernel(h_ref, mod_ref, g_ref, wg_ref, wu_ref, wd_ref, o_ref, *, base):
    x = h_ref[...]
    y = _rms(x) * g_ref[...]
    xn = (y * (1.0 + mod_ref[base + 1:base + 2, :]) + mod_ref[base:base + 1, :]).astype(BF16)
    g = _dot(xn, wg_ref[...])
    u = _dot(xn, wu_ref[...])
    a = (g * jax.nn.sigmoid(g) * u).astype(BF16)
    o_ref[...] = x + 0.5 * mod_ref[base + 2:base + 3, :] * _dot(a, wd_ref[...])


def _ffn(h, mod, gain, w_gu, w_down, base):
    B, T, D = h.shape
    F = w_down.shape[0]
    tm = min(T, 512)
    resident = pl.Buffered(1)
    return pl.pallas_call(
        functools.partial(_ffn_kernel, base=base),
        grid=(B, T // tm),
        in_specs=[
            pl.BlockSpec((None, tm, D), lambda b, i: (b, i, 0)),
            pl.BlockSpec((None, 9, D), lambda b, i: (b, 0, 0)),
            pl.BlockSpec((1, D), lambda b, i: (0, 0)),
            pl.BlockSpec((D, F), lambda b, i: (0, 0), pipeline_mode=resident),
            pl.BlockSpec((D, F), lambda b, i: (0, 1), pipeline_mode=resident),
            pl.BlockSpec((F, D), lambda b, i: (0, 0), pipeline_mode=resident),
        ],
        out_specs=pl.BlockSpec((None, tm, D), lambda b, i: (b, i, 0)),
        out_shape=jax.ShapeDtypeStruct((B, T, D), F32),
        compiler_params=_cparams(("parallel", "parallel")),
        name="swiglu_halfstep",
    )(h, mod, gain.reshape(1, D), w_gu, w_gu, w_down)


def _inproj_kernel(h_ref, mod_ref, g_ref, w_ref, oa_ref, ob_ref, on_ref, od_ref):
    y = _rms(h_ref[...]) * g_ref[...]
    xm = (y * (1.0 + mod_ref[4:5, :]) + mod_ref[3:4, :]).astype(BF16)
    o = 0
    for ref, w in ((oa_ref, UA_W), (ob_ref, UB_W), (on_ref, UN_W), (od_ref, UD_W)):
        ref[...] = _dot(xm, w_ref[:, o:o + w])
        o += w


def _inproj(h, mod, gain, w_in_p):
    B, T, D = h.shape
    tm = min(T, 512)
    W = w_in_p.shape[1]
    widths = (UA_W, UB_W, UN_W, UD_W)
    return pl.pallas_call(
        _inproj_kernel,
        grid=(B, T // tm),
        in_specs=[
            pl.BlockSpec((None, tm, D), lambda b, i: (b, i, 0)),
            pl.BlockSpec((None, 9, D), lambda b, i: (b, 0, 0)),
            pl.BlockSpec((1, D), lambda b, i: (0, 0)),
            pl.BlockSpec((D, W), lambda b, i: (0, 0)),
        ],
        out_specs=[pl.BlockSpec((None, tm, w), lambda b, i: (b, i, 0)) for w in widths],
        out_shape=[jax.ShapeDtypeStruct((B, T, w), F32) for w in widths],
        compiler_params=_cparams(("parallel", "parallel")),
        name="in_projection",
    )(h, mod, gain.reshape(1, D), w_in_p)


def _outproj_kernel(h_ref, mod_ref, oa_ref, ob_ref, on_ref, od_ref, w_ref, o_ref):
    acc = _dot(oa_ref[...], w_ref[0:GROUP_W, :])
    acc += _dot(ob_ref[...], w_ref[GROUP_W:2 * GROUP_W, :])
    acc += _dot(on_ref[...], w_ref[2 * GROUP_W:3 * GROUP_W, :])
    acc += _dot(od_ref[...], w_ref[3 * GROUP_W:4 * GROUP_W, :])
    o_ref[...] = h_ref[...] + mod_ref[5:6, :] * acc


def _outproj(h, mod, oa, ob, on, od, w_out):
    B, T, D = h.shape
    tm = min(T, 512)
    grp = pl.BlockSpec((None, tm, GROUP_W), lambda b, i: (b, i, 0))
    return pl.pallas_call(
        _outproj_kernel,
        grid=(B, T // tm),
        in_specs=[
            pl.BlockSpec((None, tm, D), lambda b, i: (b, i, 0)),
            pl.BlockSpec((None, 9, D), lambda b, i: (b, 0, 0)),
            grp, grp, grp, grp,
            pl.BlockSpec((4 * GROUP_W, D), lambda b, i: (0, 0)),
        ],
        out_specs=pl.BlockSpec((None, tm, D), lambda b, i: (b, i, 0)),
        out_shape=jax.ShapeDtypeStruct((B, T, D), F32),
        compiler_params=_cparams(("parallel", "parallel")),
        name="out_projection",
    )(h, mod, oa, ob, on, od, w_out)


def _rope_tables(positions, width, group, n_rep):
    half = group // 2
    inv_freq = ROPE_THETA ** (-jnp.arange(half, dtype=F32) / half)

    def axis(pos):
        ang = pos.astype(F32)[:, None] * inv_freq[None, :]
        c, s = jnp.cos(ang), jnp.sin(ang)
        return (jnp.concatenate([c, c], -1), jnp.concatenate([jnp.zeros_like(s), s], -1),
                jnp.concatenate([-s, jnp.zeros_like(s)], -1))

    row, col = positions
    parts = [jnp.concatenate([a, b], -1) for a, b in zip(axis(row), axis(col))]
    return [jnp.tile(p, (1, n_rep)) for p in parts]


def _pad_lanes(t, lo, total, fill):
    T = t.shape[0]
    return jnp.concatenate([jnp.full((T, lo), fill, F32), t,
                            jnp.full((T, total - lo - t.shape[1]), fill, F32)], -1)


def _vt_ext(v):
    tm = v.shape[0]
    lane = lax.broadcasted_iota(jnp.int32, (tm, VT_ROWS - HEAD_DIM), 1)
    aux = jnp.where(lane == 0, 1.0, 0.0)
    return jnp.concatenate([v, aux], axis=-1).T.astype(BF16)


def _mla_prep_kernel(ua_ref, c_ref, s1_ref, s2_ref, qnorm_ref, kvnorm_ref, qn_ref, kn_ref,
                     wuq_ref, wuk_ref, wuv_ref, q_out, k_out, vt_out):
    ua = ua_ref[...]
    qc = (_rms(ua[:, :MLA_Q_RANK]) * qnorm_ref[...]).astype(BF16)
    kvc = (_rms(ua[:, MLA_Q_RANK:MLA_Q_RANK + MLA_KV_RANK]) * kvnorm_ref[...]).astype(BF16)
    k_rope = ua[:, 3 * LANES:4 * LANES]
    q_all = _dot(qc, wuq_ref[...])
    k_all = _dot(kvc, wuk_ref[...])
    v_all = _dot(kvc, wuv_ref[...])
    cos, s_dn, s_up = c_ref[...], s1_ref[...], s2_ref[...]
    half = MLA_ROPE // 4

    def rope(x):
        return x * cos + pltpu.roll(x, half, 1) * s_dn + pltpu.roll(x, LANES - half, 1) * s_up

    def headnorm(x, gain):
        ms = jnp.sum(x * x, axis=-1, keepdims=True) * (1.0 / MLA_QK)
        return x * lax.rsqrt(ms + NORM_EPS) * gain

    for h in range(MLA_HEADS):
        qh = headnorm(q_all[:, h * LANES:(h + 1) * LANES], qn_ref[...])
        q_out[h] = rope(qh).astype(BF16)
        kh = headnorm(k_all[:, h * LANES:(h + 1) * LANES] + k_rope, kn_ref[...])
        k_out[h] = rope(kh).astype(BF16)
        vt_out[h] = _vt_ext(v_all[:, h * MLA_V:(h + 1) * MLA_V])


def _mla_prep(ua, tables, q_norm, kv_norm, qn_p, kn_p, wuq_p, wuk_p, wuv):
    B, T, _ = ua.shape
    tm = min(T, 512)
    tab = pl.BlockSpec((tm, LANES), lambda b, i: (i, 0))
    full = lambda a: pl.BlockSpec(a.shape, lambda b, i: (0,) * a.ndim)
    small = [q_norm, kv_norm, qn_p, kn_p, wuq_p, wuk_p, wuv]
    return pl.pallas_call(
        _mla_prep_kernel,
        grid=(B, T // tm),
        in_specs=[pl.BlockSpec((None, tm, UA_W), lambda b, i: (b, i, 0)), tab, tab, tab]
        + [full(a) for a in small],
        out_specs=[
            pl.BlockSpec((None, MLA_HEADS, tm, LANES), lambda b, i: (b, 0, i, 0)),
            pl.BlockSpec((None, MLA_HEADS, tm, LANES), lambda b, i: (b, 0, i, 0)),
            pl.BlockSpec((None, MLA_HEADS, VT_ROWS, tm), lambda b, i: (b, 0, 0, i)),
        ],
        out_shape=[
            jax.ShapeDtypeStruct((B, MLA_HEADS, T, LANES), BF16),
            jax.ShapeDtypeStruct((B, MLA_HEADS, T, LANES), BF16),
            jax.ShapeDtypeStruct((B, MLA_HEADS, VT_ROWS, T), BF16),
        ],
        compiler_params=_cparams(("parallel", "parallel")),
        name="mla_prep",
    )(ua, *tables, *small)


def _qkv_prep_kernel(*refs, q_off, nq, nk, nv, use_rope, v_rows, v_cols):
    n_in = 7 if use_rope else 4
    x_ref, seg_ref, gq_ref, gk_ref = refs[:4]
    if use_rope:
        c_ref, s1_ref, s2_ref = refs[4:7]
    q_out, k_out = refs[n_in:n_in + 2]
    v_outs = list(refs[n_in + 2:])
    v_out = v_outs.pop(0) if v_rows else None
    vt_out = v_outs.pop(0) if v_cols else None
    x = x_ref[...]
    wq, wk, wv = nq * HEAD_DIM, nk * HEAD_DIM, nv * HEAD_DIM
    q = x[:, q_off:q_off + wq]
    k = x[:, q_off + wq:q_off + wq + wk]
    v = x[:, q_off + wq + wk:q_off + wq + wk + wv]

    def headnorm(t, gain, w):
        hi, lo = _split2(t * t)
        seg = seg_ref[0:w, 0:w]
        ms = (_dot(hi, seg) + _dot(lo, seg)) * (1.0 / HEAD_DIM)
        return t * lax.rsqrt(ms + NORM_EPS) * gain

    q = headnorm(q, gq_ref[...], wq)
    k = headnorm(k, gk_ref[...], wk)
    if use_rope:
        half = HEAD_DIM // 4

        def rope(t, w):
            return (t * c_ref[:, 0:w] + pltpu.roll(t, half, 1) * s1_ref[:, 0:w]
                    + pltpu.roll(t, w - half, 1) * s2_ref[:, 0:w])

        q = rope(q, wq)
        k = rope(k, wk)
    for h in range(nq):
        q_out[h] = q[:, h * HEAD_DIM:(h + 1) * HEAD_DIM].astype(BF16)
    for h in range(nk):
        k_out[h] = k[:, h * HEAD_DIM:(h + 1) * HEAD_DIM].astype(BF16)
    for h in range(nv):
        vh = v[:, h * HEAD_DIM:(h + 1) * HEAD_DIM]
        if v_rows:
            v_out[h] = vh.astype(BF16)
        if v_cols:
            vt_out[h] = _vt_ext(vh)


def _qkv_prep(x, seg, gq, gk, nq, nk, nv, tables=None, v_rows=True, v_cols=False):
    B, T, W = x.shape
    tm = min(T, 512)
    use_rope = tables is not None
    full = lambda a: pl.BlockSpec(a.shape, lambda b, i: (0,) * a.ndim)
    in_specs = [pl.BlockSpec((None, tm, W), lambda b, i: (b, i, 0)), full(seg), full(gq), full(gk)]
    args = [x, seg, gq, gk]
    if use_rope:
        in_specs += [pl.BlockSpec((tm, tables[0].shape[1]), lambda b, i: (i, 0))] * 3
        args += list(tables)
    hm = lambda n: pl.BlockSpec((None, n, tm, HEAD_DIM), lambda b, i: (b, 0, i, 0))
    out_specs = [hm(nq), hm(nk)]
    out_shape = [jax.ShapeDtypeStruct((B, n, T, HEAD_DIM), BF16) for n in (nq, nk)]
    if v_rows:
        out_specs.append(hm(nv))
        out_shape.append(jax.ShapeDtypeStruct((B, nv, T, HEAD_DIM), BF16))
    if v_cols:
        out_specs.append(pl.BlockSpec((None, nv, VT_ROWS, tm), lambda b, i: (b, 0, 0, i)))
        out_shape.append(jax.ShapeDtypeStruct((B, nv, VT_ROWS, T), BF16))
    return pl.pallas_call(
        functools.partial(_qkv_prep_kernel, q_off=0, nq=nq, nk=nk, nv=nv, use_rope=use_rope,
                          v_rows=v_rows, v_cols=v_cols),
        grid=(B, T // tm),
        in_specs=in_specs,
        out_specs=out_specs,
        out_shape=out_shape,
        compiler_params=_cparams(("parallel", "parallel")),
        name="qkv_prep_rope" if use_rope else "qkv_prep",
    )(*args)


def _attn_kernel(*refs, n_heads, group, has_extra, nk):
    if has_extra:
        q_ref, k_ref, vt_ref, kx_ref, vtx_ref, o_ref, m_ref, acc_ref = refs
    else:
        q_ref, k_ref, vt_ref, o_ref, m_ref, acc_ref = refs
    j = pl.program_id(2)

    heads = range(n_heads)

    def first(kr, vr):
        s = [_dot_nt(kr[h // group], q_ref[h]) for h in heads]
        m = [jnp.max(s[h], axis=0, keepdims=True) for h in heads]
        p = [jnp.exp2(s[h] - m[h]).astype(BF16) for h in heads]
        for h in heads:
            m_ref[h] = m[h]
            acc_ref[h] = _dot(vr[h // group], p[h])

    def update(kr, vr):
        m_prev = [m_ref[h] for h in heads]
        s = [_dot_nt(kr[h // group], q_ref[h]) for h in heads]
        pv = [_dot(vr[h // group], jnp.exp2(s[h] - m_prev[h]).astype(BF16)) for h in heads]
        m_blk = [jnp.max(s[h], axis=0, keepdims=True) for h in heads]
        jump = functools.reduce(jnp.maximum, [m_blk[h] - m_prev[h] for h in heads])
        safe = jnp.max(jump) <= MAX_JUMP

        @pl.when(safe)
        def _():
            for h in heads:
                m_new = jnp.maximum(m_prev[h], m_blk[h])
                m_ref[h] = m_new
                acc_ref[h] = (acc_ref[h] + pv[h]) * jnp.exp2(m_prev[h] - m_new)

        @pl.when(jnp.logical_not(safe))
        def _():
            s2 = [_dot_nt(kr[h // group], q_ref[h]) for h in heads]
            m_new = [jnp.maximum(m_ref[h], jnp.max(s2[h], axis=0, keepdims=True)) for h in heads]
            p2 = [jnp.exp2(s2[h] - m_new[h]).astype(BF16) for h in heads]
            for h in heads:
                acc_ref[h] = jnp.exp2(m_ref[h] - m_new[h]) * acc_ref[h] + _dot(vr[h // group], p2[h])
                m_ref[h] = m_new[h]

    if has_extra:
        @pl.when(j == 0)
        def _():
            first(kx_ref, vtx_ref)

        update(k_ref, vt_ref)
    else:
        @pl.when(j == 0)
        def _():
            first(k_ref, vt_ref)

        @pl.when(j > 0)
        def _():
            update(k_ref, vt_ref)

    @pl.when(j == nk - 1)
    def _():
        outs = []
        for h in range(n_heads):
            a = acc_ref[h].T
            outs.append(a[:, :HEAD_DIM] / a[:, HEAD_DIM:HEAD_DIM + 1])
        o_ref[...] = jnp.concatenate(outs, axis=-1).astype(o_ref.dtype)


def _attention(q, k, vt, kx=None, vtx=None):
    B, H, Tq, dq = q.shape
    Hk, Tk = k.shape[1], k.shape[2]
    group = H // Hk
    tq = min(Tq, 512)
    tk = min(Tk, 1024)
    nk = Tk // tk
    has_extra = kx is not None
    in_specs = [
        pl.BlockSpec((None, H, tq, dq), lambda b, i, j: (b, 0, i, 0)),
        pl.BlockSpec((None, Hk, tk, dq), lambda b, i, j: (b, 0, j, 0)),
        pl.BlockSpec((None, Hk, VT_ROWS, tk), lambda b, i, j: (b, 0, 0, j)),
    ]
    args = [q, k, vt]
    if has_extra:
        Tx = kx.shape[2]
        in_specs += [pl.BlockSpec((None, Hk, Tx, dq), lambda b, i, j: (b, 0, 0, 0)),
                     pl.BlockSpec((None, Hk, VT_ROWS, Tx), lambda b, i, j: (b, 0, 0, 0))]
        args += [kx, vtx]
    return pl.pallas_call(
        functools.partial(_attn_kernel, n_heads=H, group=group, has_extra=has_extra, nk=nk),
        grid=(B, Tq // tq, nk),
        in_specs=in_specs,
        out_specs=pl.BlockSpec((None, tq, H * HEAD_DIM), lambda b, i, j: (b, i, 0)),
        out_shape=jax.ShapeDtypeStruct((B, Tq, H * HEAD_DIM), BF16),
        scratch_shapes=[pltpu.VMEM((H, 1, tq), F32), pltpu.VMEM((H, VT_ROWS, tq), F32)],
        compiler_params=_cparams(("parallel", "parallel", "arbitrary")),
        name="flash_attention_ctx" if has_extra else "flash_attention",
    )(*args)


def _na_bias_kernel(r_ref, e_ref, m_ref, o_ref):
    o_ref[...] = _dot_exact_rhs(r_ref[...], e_ref[...]) + m_ref[...]


def _na_bias_tables(rpb):
    H, n_dr, n_dc = rpb.shape
    W = GRID_W
    qc = np.arange(W)[:, None]
    kc = np.arange(W)[None, :]
    c0 = np.clip(qc - NA_WIN_C // 2, 0, W - NA_WIN_C)
    valid = (kc >= c0) & (kc < c0 + NA_WIN_C)
    d = kc - qc + NA_WIN_C - 1
    onehot = np.zeros((LANES, W * W), np.float32)
    for a in range(W):
        for b in range(W):
            if valid[a, b]:
                onehot[d[a, b], a * W + b] = 1.0
    mask = np.where(valid, 0.0, NEG_INF).astype(np.float32).reshape(1, W * W)
    rows = H * n_dr
    rows_p = -(-rows // SUBLANES) * SUBLANES
    r2 = jnp.zeros((rows_p, LANES), F32).at[:rows, :n_dc].set(rpb.reshape(rows, n_dc) * LOG2E)
    tn = 1024
    tiles = pl.pallas_call(
        _na_bias_kernel,
        grid=(W * W // tn,),
        in_specs=[pl.BlockSpec((rows_p, LANES), lambda n: (0, 0)),
                  pl.BlockSpec((LANES, tn), lambda n: (0, n)),
                  pl.BlockSpec((1, tn), lambda n: (0, n))],
        out_specs=pl.BlockSpec((rows_p, tn), lambda n: (0, n)),
        out_shape=jax.ShapeDtypeStruct((rows_p, W * W), F32),
        compiler_params=_cparams(("parallel",)),
        name="na_bias_expand",
    )(r2, jnp.asarray(onehot, BF16), jnp.asarray(mask))
    tiles = tiles[:rows].reshape(H, n_dr, W, W)
    slabs = [tiles[:, base:base + NA_WIN_R].transpose(0, 2, 1, 3).reshape(H, W, NA_WIN_R * W)
             for base in range(NA_WIN_R)]
    return jnp.stack(slabs, 0)


def _na_kernel(q_ref, k_ref, v_ref, kc_ref, vc_ref, tb_ref, o_ref, *, rb, rows, n_heads):
    blk = pl.program_id(1)
    W = GRID_W
    win = NA_WIN_R * W

    def row_body(i, carry):
        r = blk * rb + i
        r0 = jnp.clip(r - NA_WIN_R // 2, 0, rows - NA_WIN_R)
        base = r0 - r + NA_WIN_R - 1
        k0 = pl.multiple_of(r0 * W, W)
        q0 = pl.multiple_of(i * W, W)
        heads = range(n_heads)
        q = [q_ref[h, pl.ds(q0, W), :] for h in heads]
        s_nb = [_dot_nt(q[h], k_ref[h, pl.ds(k0, win), :]) + tb_ref[base, h] for h in heads]
        s_cx = [_dot_nt(q[h], kc_ref[h]) for h in heads]
        m = [jnp.maximum(jnp.max(s_nb[h], axis=-1, keepdims=True),
                         jnp.max(s_cx[h], axis=-1, keepdims=True)) for h in heads]
        p_nb = [jnp.exp2(s_nb[h] - m[h]) for h in heads]
        p_cx = [jnp.exp2(s_cx[h] - m[h]) for h in heads]
        l = [jnp.sum(p_nb[h], axis=-1, keepdims=True) + jnp.sum(p_cx[h], axis=-1, keepdims=True)
             for h in heads]
        o = [_dot(p_nb[h].astype(BF16), v_ref[h, pl.ds(k0, win), :])
             + _dot(p_cx[h].astype(BF16), vc_ref[h]) for h in heads]
        o_ref[pl.ds(q0, W), :] = jnp.concatenate(
            [o[h] / l[h] for h in heads], axis=-1).astype(o_ref.dtype)
        return carry

    lax.fori_loop(0, rb, row_body, 0)


def _na_attention(q, k, v, kc, vc, tb):
    B, H, S, dh = q.shape
    L = kc.shape[2]
    rows = S // GRID_W
    assert rows >= NA_WIN_R
    rb = 8
    return pl.pallas_call(
        functools.partial(_na_kernel, rb=rb, rows=rows, n_heads=H),
        grid=(B, rows // rb),
        in_specs=[
            pl.BlockSpec((None, H, rb * GRID_W, dh), lambda b, i: (b, 0, i, 0)),
            pl.BlockSpec((None, H, S, dh), lambda b, i: (b, 0, 0, 0)),
            pl.BlockSpec((None, H, S, dh), lambda b, i: (b, 0, 0, 0)),
            pl.BlockSpec((None, H, L, dh), lambda b, i: (b, 0, 0, 0)),
            pl.BlockSpec((None, H, L, dh), lambda b, i: (b, 0, 0, 0)),
            pl.BlockSpec(tb.shape, lambda b, i: (0, 0, 0, 0)),
        ],
        out_specs=pl.BlockSpec((None, rb * GRID_W, H * dh), lambda b, i: (b, i, 0)),
        out_shape=jax.ShapeDtypeStruct((B, S, H * dh), BF16),
        compiler_params=_cparams(("parallel", "arbitrary")),
        name="neighbourhood_attention",
    )(q, k, v, kc, vc, tb)


def _rwkv_feat_kernel(x_ref, xp_ref, xn_ref, mu_ref, seg_ref, w2_ref, a2_ref, g2_ref, vec_ref,
                      r_out, v_out, kk_out, lwf_out, lwb_out, kdf_out, kdb_out, af_out, ab_out,
                      bon_out, g_out, *, nblk):
    i = pl.program_id(1)
    x = x_ref[...]
    tm = x.shape[0]
    has_prev = jnp.where(i > 0, 1.0, 0.0)
    has_next = jnp.where(i < nblk - 1, 1.0, 0.0)
    row = lax.broadcasted_iota(jnp.int32, (tm, 1), 0)
    prev = jnp.where(row == 0, xp_ref[SUBLANES - 1:SUBLANES, :] * has_prev, pltpu.roll(x, 1, 0))
    nxt = jnp.where(row == tm - 1, xn_ref[0:1, :] * has_next, pltpu.roll(x, tm - 1, 0))
    xs = x + mu_ref[0:1, :] * (prev - x) + mu_ref[1:2, :] * (nxt - x)

    G = GROUP_W
    r, k, v = xs[:, 0:G], xs[:, G:2 * G], xs[:, 2 * G:3 * G]
    lora = xs[:, 3 * G:4 * G]
    seg = seg_ref[...]

    def segsum(t):
        hi, lo = _split2(t)
        return _dot(hi, seg) + _dot(lo, seg)

    k_k, k_a, r_k = vec_ref[0:1, :], vec_ref[1:2, :], vec_ref[2:3, :]
    kk = k * k_k
    kk = kk * lax.rsqrt(segsum(kk * kk) + 1e-12)
    g_out[...] = _dot(jax.nn.sigmoid(lora).astype(BF16), g2_ref[...])
    tanh_l = jnp.tanh(lora).astype(BF16)
    lora_b = lora.astype(BF16)
    kd_sum = None
    for d, (lw_out, kd_out, a_out) in enumerate(((lwf_out, kdf_out, af_out), (lwb_out, kdb_out, ab_out))):
        w0, a0 = vec_ref[3 + d:4 + d, :], vec_ref[5 + d:6 + d, :]
        z = w0 + _dot(tanh_l, w2_ref[d])
        w_log = -(jnp.maximum(-z, 0.0) + jnp.log(1.0 + jnp.exp(-jnp.abs(z)))) - 0.5
        lw_out[...] = -jnp.exp(w_log)
        a = jax.nn.sigmoid(a0 + _dot(lora_b, a2_ref[d]))
        kd = k * (1.0 + (a - 1.0) * k_a)
        kd_out[...] = kd
        a_out[...] = a
        kd_sum = kd if kd_sum is None else kd_sum + kd
    r_out[...] = r
    v_out[...] = v
    kk_out[...] = kk
    bon_out[...] = segsum(r * kd_sum * r_k) * v


def _rwkv_features(ub, mu_p, seg, w2_p, a2_p, g2_p, vecs):
    B, T, W = ub.shape
    tm = min(T, 256)
    nblk = T // tm
    hb = tm // SUBLANES
    nh = T // SUBLANES
    full = lambda a: pl.BlockSpec(a.shape, lambda b, i: (0,) * a.ndim)
    out = pl.BlockSpec((None, tm, GROUP_W), lambda b, i: (b, i, 0))
    return pl.pallas_call(
        functools.partial(_rwkv_feat_kernel, nblk=nblk),
        grid=(B, nblk),
        in_specs=[
            pl.BlockSpec((None, tm, W), lambda b, i: (b, i, 0)),
            pl.BlockSpec((None, SUBLANES, W), lambda b, i: (b, jnp.maximum(i * hb - 1, 0), 0)),
            pl.BlockSpec((None, SUBLANES, W), lambda b, i: (b, jnp.minimum((i + 1) * hb, nh - 1), 0)),
            full(mu_p), full(seg), full(w2_p), full(a2_p), full(g2_p), full(vecs),
        ],
        out_specs=[out] * 11,
        out_shape=[jax.ShapeDtypeStruct((B, T, GROUP_W), F32)] * 11,
        compiler_params=_cparams(("parallel", "parallel")),
        name="rwkv_features",
    )(ub, ub, ub, mu_p, seg, w2_p, a2_p, g2_p, vecs)


def _rwkv_chunk_kernel(r_ref, v_ref, kk_ref, lwf_ref, lwb_ref, kdf_ref, kdb_ref, af_ref, ab_ref,
                       cum_ref, tri_ref, hm_ref,
                       rh_out, y1_out, mp_out, np_out, pc_out, *, nsub):
    C = CHUNK
    hm = hm_ref[...]
    eye_w = tri_ref[2]
    lw_refs, kd_refs, a_refs = (lwf_ref, lwb_ref), (kdf_ref, kdb_ref), (af_ref, ab_ref)

    def diag(t):
        return (jnp.concatenate([t, t, t, t], axis=0) * hm).astype(BF16)

    def rows2(a, b):
        return jnp.concatenate([a, b], axis=0).astype(BF16)

    chains = [(ci, d) for ci in range(nsub) for d in range(2)]
    n = len(chains)
    rng = range(n)
    rows = [slice(ci * C, (ci + 1) * C) for ci, _ in chains]

    at, rt, vv, bh, kh, b_d, k_d, e_end = ([None] * n for _ in range(8))
    for i, (ci, d) in enumerate(chains):
        rs = rows[i]
        r, kk = r_ref[rs, :], kk_ref[rs, :]
        lw, kd, a = lw_refs[d][rs, :], kd_refs[d][rs, :], a_refs[d][rs, :]
        cl = _dot_exact_lhs(cum_ref[d], lw)
        last = C - 1 if d == 0 else 0
        e_end[i] = jnp.exp(cl[last:last + 1, :])
        e_neg = jnp.exp(-cl)
        bt = kk * a * e_neg
        kt = kd * e_neg
        vv[i] = v_ref[rs, :]
        at[i] = -kk * jnp.exp(cl - lw)
        rt[i] = r * jnp.exp(cl)
        bh[i], kh[i] = bt * e_end[i], kt * e_end[i]
        b_d[i], k_d[i] = diag(bt), diag(kt)
    a_d = [diag(at[i]) for i in rng]
    v_d = [diag(vv[i]) for i in rng]

    strict = [tri_ref[d] for _, d in chains]
    ar = [rows2(at[i], rt[i]) for i in rng]
    gb = [_dot_nt(ar[i], b_d[i]) for i in rng]
    gk = [_dot_nt(ar[i], k_d[i]) for i in rng]
    l_ab = [gb[i][:C] * strict[i] for i in rng]
    w_rb = [(gb[i][C:] * (strict[i] + eye_w)).astype(BF16) for i in rng]
    w_akrk = [rows2(gk[i][:C] * strict[i], gk[i][C:] * (strict[i] + eye_w)) for i in rng]

    pw = [_dot(l_ab[i].astype(BF16), diag(l_ab[i])) for i in rng]
    tm = [eye_w + l_ab[i] for i in rng]
    for _ in range(4):
        sq = [_dot(rows2(pw[i], tm[i]), diag(pw[i])) for i in rng]
        tm = [tm[i] + sq[i][C:] for i in rng]
        pw = [sq[i][:C] for i in rng]
    tm = [(tm[i] + _dot(tm[i].astype(BF16), diag(pw[i]))).astype(BF16) for i in rng]

    wt = [_dot(tm[i], a_d[i]) for i in rng]
    xy = [_dot(w_akrk[i], v_d[i]) for i in rng]
    u0 = [_dot(tm[i], diag(xy[i][:C])) for i in rng]
    rh = [rt[i] + _dot(w_rb[i], diag(wt[i])) for i in rng]
    y1 = [_dot(w_rb[i], diag(u0[i])) + xy[i][C:] for i in rng]
    for i, (ci, d) in enumerate(chains):
        rh_out[d, rows[i], :] = rh[i].astype(BF16)
        y1_out[d, rows[i], :] = y1[i]
        mp_out[d, ci] = (_dot_tn(wt[i].astype(BF16), bh[i].astype(BF16)) * hm).astype(BF16)
        np_bd = _dot_tn(rows2(u0[i], vv[i]), rows2(bh[i], kh[i])) * hm
        np_out[d, ci] = functools.reduce(
            jnp.add, [np_bd[h * HEAD_DIM:(h + 1) * HEAD_DIM] for h in range(4)])
        pc_out[d, ci] = jnp.broadcast_to(e_end[i], (SUBLANES, GROUP_W))


def _rwkv_chunks(feats, cum, tri, hm):
    r = feats[0]
    B, T, G = r.shape
    nch = T // CHUNK
    nsub = 4 if nch % 4 == 0 else 1
    tb = nsub * CHUNK
    tok = pl.BlockSpec((None, tb, G), lambda b, c: (b, c, 0))
    full = lambda a: pl.BlockSpec(a.shape, lambda b, c: (0,) * a.ndim)
    return pl.pallas_call(
        functools.partial(_rwkv_chunk_kernel, nsub=nsub),
        grid=(B, nch // nsub),
        in_specs=[tok] * 9 + [full(cum), full(tri), full(hm)],
        out_specs=[
            pl.BlockSpec((None, 2, tb, G), lambda b, c: (b, 0, c, 0)),
            pl.BlockSpec((None, 2, tb, G), lambda b, c: (b, 0, c, 0)),
            pl.BlockSpec((None, 2, nsub, G, G), lambda b, c: (b, 0, c, 0, 0)),
            pl.BlockSpec((None, 2, nsub, HEAD_DIM, G), lambda b, c: (b, 0, c, 0, 0)),
            pl.BlockSpec((None, 2, nsub, SUBLANES, G), lambda b, c: (b, 0, c, 0, 0)),
        ],
        out_shape=[
            jax.ShapeDtypeStruct((B, 2, T, G), BF16),
            jax.ShapeDtypeStruct((B, 2, T, G), F32),
            jax.ShapeDtypeStruct((B, 2, nch, G, G), BF16),
            jax.ShapeDtypeStruct((B, 2, nch, HEAD_DIM, G), F32),
            jax.ShapeDtypeStruct((B, 2, nch, SUBLANES, G), F32),
        ],
        compiler_params=_cparams(("parallel", "parallel")),
        name="rwkv_chunk_transitions",
    )(*feats, cum, tri, hm)


def _rwkv_scan_kernel(s0_ref, hm_ref, rhf_ref, y1f_ref, mpf_ref, npf_ref, pcf_ref,
                      rhb_ref, y1b_ref, mpb_ref, npb_ref, pcb_ref,
                      yf_out, yb_out, s_out, *, batch):
    c = pl.program_id(0)

    @pl.when(c == 0)
    def _():
        s_out[...] = s0_ref[...]

    hm = hm_ref[...]
    dirs = ((rhf_ref, y1f_ref, mpf_ref, npf_ref, pcf_ref, yf_out),
            (rhb_ref, y1b_ref, mpb_ref, npb_ref, pcb_ref, yb_out))
    chains = [(b, d) for b in range(batch) for d in range(2)]
    s = [s_out[b, d] for b, d in chains]
    s_d = [(jnp.concatenate([t, t, t, t], axis=0) * hm).astype(BF16) for t in s]
    for i, (b, d) in enumerate(chains):
        rh_ref, y1_ref, mp_ref, np_ref, pc_ref, y_out = dirs[d]
        y_out[b] = _dot_nt(rh_ref[b], s_d[i]) + y1_ref[b]
        s_out[b, d] = s[i] * pc_ref[b, 0:1, :] + _dot(s[i].astype(BF16), mp_ref[b]) + np_ref[b]


def _rwkv_scan(s0, hm, rh, y1, mp, npm, pc):
    B, _, T, G = rh.shape
    nch = T // CHUNK
    fwd = lambda c: c
    bwd = lambda c: nch - 1 - c

    def specs(order, d):
        return [
            pl.BlockSpec((B, None, CHUNK, G), lambda c: (0, d, order(c), 0)),
            pl.BlockSpec((B, None, CHUNK, G), lambda c: (0, d, order(c), 0)),
            pl.BlockSpec((B, None, None, G, G), lambda c: (0, d, order(c), 0, 0)),
            pl.BlockSpec((B, None, None, HEAD_DIM, G), lambda c: (0, d, order(c), 0, 0)),
            pl.BlockSpec((B, None, None, SUBLANES, G), lambda c: (0, d, order(c), 0, 0)),
        ]

    return pl.pallas_call(
        functools.partial(_rwkv_scan_kernel, batch=B),
        grid=(nch,),
        in_specs=[pl.BlockSpec(s0.shape, lambda c: (0, 0, 0, 0)), pl.BlockSpec(hm.shape, lambda c: (0, 0))]
        + specs(fwd, 0) + specs(bwd, 1),
        out_specs=[
            pl.BlockSpec((B, CHUNK, G), lambda c: (0, fwd(c), 0)),
            pl.BlockSpec((B, CHUNK, G), lambda c: (0, bwd(c), 0)),
            pl.BlockSpec(s0.shape, lambda c: (0, 0, 0, 0)),
        ],
        out_shape=[
            jax.ShapeDtypeStruct((B, T, G), F32),
            jax.ShapeDtypeStruct((B, T, G), F32),
            jax.ShapeDtypeStruct(s0.shape, F32),
        ],
        compiler_params=_cparams(("arbitrary",)),
        name="rwkv_state_scan",
    )(s0, hm, rh, y1, mp, npm, pc, rh, y1, mp, npm, pc)


def _rwkv_post_kernel(yf_ref, yb_ref, bon_ref, g_ref, seg_ref, vec_ref, o_ref):
    y = yf_ref[...] + yb_ref[...]
    seg = seg_ref[...]

    def segmean(t):
        hi, lo = _split2(t)
        return (_dot(hi, seg) + _dot(lo, seg)) * (1.0 / HEAD_DIM)

    mu = segmean(y)
    yc = y - mu
    var = segmean(yc * yc)
    yn = yc * lax.rsqrt(var + RWKV_LNX_EPS) * vec_ref[0:1, :] + vec_ref[1:2, :]
    o_ref[...] = ((yn + bon_ref[...]) * g_ref[...]).astype(o_ref.dtype)


def _rwkv_post(yf, yb, bon, g, seg, vecs):
    B, T, G = yf.shape
    tm = min(T, 512)
    tok = pl.BlockSpec((None, tm, G), lambda b, i: (b, i, 0))
    full = lambda a: pl.BlockSpec(a.shape, lambda b, i: (0,) * a.ndim)
    return pl.pallas_call(
        _rwkv_post_kernel,
        grid=(B, T // tm),
        in_specs=[tok, tok, tok, tok, full(seg), full(vecs)],
        out_specs=tok,
        out_shape=jax.ShapeDtypeStruct((B, T, G), BF16),
        compiler_params=_cparams(("parallel", "parallel")),
        name="rwkv_output",
    )(yf, yb, bon, g, seg, vecs)


def _seg_matrix(width):
    i = np.arange(width) // HEAD_DIM
    return jnp.asarray((i[:, None] == i[None, :]).astype(np.float32), BF16)


def _scan_constants():
    C = CHUNK
    t = np.arange(4 * C)
    head, tok = t // C, t % C
    i = np.arange(C)
    lower = i[:, None] > tok[None, :]
    upper = i[:, None] < tok[None, :]
    eye = i[:, None] == tok[None, :]
    tri = jnp.asarray(np.stack([lower, upper, eye]).astype(np.float32))
    cum = jnp.asarray(np.stack([i[:, None] >= i[None, :], i[:, None] <= i[None, :]]).astype(np.float32), BF16)
    hm = jnp.asarray((head[:, None] == (np.arange(GROUP_W) // HEAD_DIM)[None, :]).astype(np.float32))
    return cum, tri, hm


def _pad_vec(v, lo, total):
    return jnp.zeros((total,), F32).at[lo:lo + v.shape[0]].set(v)


def kernel(x, c, ctx, c_ctx, w_mod, b_mod, norm_ffn1, ffn1_w_gu, ffn1_w_down, norm_mix, w_in, w_out, mla_q_norm, mla_kv_norm, mla_w_uq, mla_w_ukv, mla_qn, mla_kn, rwkv_shift, rwkv_w0, rwkv_w2, rwkv_a0, rwkv_a2, rwkv_g2, rwkv_k_k, rwkv_k_a, rwkv_r_k, rwkv_lnx_g, rwkv_lnx_b, na_qn, na_kn, na_rpb, gqa_qn, gqa_kn, norm_ffn2, ffn2_w_gu, ffn2_w_down):
    B, S, D = x.shape
    Lc = ctx.shape[1]
    depth = w_mod.shape[0]
    assert B + 1 <= SUBLANES and S % 512 == 0 and Lc % CHUNK == 0 and S % GRID_W == 0

    cvec = jnp.zeros((SUBLANES, D), F32).at[:B].set(c).at[B].set(c_ctx)
    mods = _modulation(cvec, w_mod, b_mod).reshape(depth, SUBLANES, 9, D)

    t = jnp.arange(S)
    pos = (t // GRID_W, t % GRID_W)
    zero = jnp.zeros((Lc,), jnp.int32)
    mla_tab, mla_tab_c = [
        [_pad_lanes(p, MLA_NOPE, LANES, fill) for p, fill in zip(_rope_tables(pp, MLA_ROPE, MLA_ROPE // 2, 1), (1.0, 0.0, 0.0))]
        for pp in (pos, (zero, zero))]
    gqa_tab, gqa_tab_c = [_rope_tables(pp, 4 * HEAD_DIM, HEAD_DIM // 2, 4) for pp in (pos, (zero, zero))]

    seg = _seg_matrix(GROUP_W)
    cum, tri, hm = _scan_constants()
    mla_scale = MLA_QK ** -0.5 * LOG2E
    dh_scale = HEAD_DIM ** -0.5 * LOG2E

    h, hc = x, ctx
    for l in range(depth):
        ctx_out = l < depth - 1
        m_lat = mods[l, :B]
        m_ctx = jnp.broadcast_to(mods[l, B], (B, 9, D))

        w_gu1, w_dn1 = ffn1_w_gu[l].astype(BF16), ffn1_w_down[l].astype(BF16)
        w_gu2, w_dn2 = ffn2_w_gu[l].astype(BF16), ffn2_w_down[l].astype(BF16)
        wi = w_in[l]
        zc = lambda n: jnp.zeros((D, n), F32)
        o_b = MLA_COLS
        o_n = o_b + RWKV_COLS
        o_d = o_n + NA_COLS
        w_in_p = jnp.concatenate([
            wi[:, :MLA_Q_RANK + MLA_KV_RANK], zc(64), wi[:, MLA_Q_RANK + MLA_KV_RANK:MLA_COLS], zc(32),
            wi[:, o_b:o_n], zc(UB_W - RWKV_COLS),
            wi[:, o_n:o_d],
            wi[:, o_d:o_d + GQA_COLS]], axis=1).astype(BF16)
        w_out_b = w_out[l].astype(BF16)

        wuq = mla_w_uq[l].reshape(MLA_Q_RANK, MLA_HEADS, MLA_QK)
        wuq_p = jnp.pad(wuq, ((0, 0), (0, 0), (0, LANES - MLA_QK))).reshape(MLA_Q_RANK, MLA_HEADS * LANES).astype(BF16)
        wukv = mla_w_ukv[l].reshape(MLA_KV_RANK, MLA_HEADS, MLA_NOPE + MLA_V)
        wuk_p = jnp.pad(wukv[..., :MLA_NOPE], ((0, 0), (0, 0), (0, LANES - MLA_NOPE))).reshape(MLA_KV_RANK, MLA_HEADS * LANES).astype(BF16)
        wuv = wukv[..., MLA_NOPE:].reshape(MLA_KV_RANK, MLA_HEADS * MLA_V).astype(BF16)
        mla_small = (mla_q_norm[l].reshape(1, -1), mla_kv_norm[l].reshape(1, -1),
                     _pad_vec(mla_qn[l] * mla_scale, 0, LANES).reshape(1, LANES),
                     _pad_vec(mla_kn[l], 0, LANES).reshape(1, LANES), wuq_p, wuk_p, wuv)

        na_gq = jnp.tile(na_qn[l] * dh_scale, 4).reshape(1, -1)
        na_gk = jnp.tile(na_kn[l], 4).reshape(1, -1)
        gqa_gq = jnp.tile(gqa_qn[l] * dh_scale, 4).reshape(1, -1)
        gqa_gk = jnp.tile(gqa_kn[l], 2).reshape(1, -1)
        na_tb = _na_bias_tables(na_rpb[l])

        mu_p = jnp.zeros((SUBLANES, UB_W), F32).at[:2, :RWKV_COLS].set(rwkv_shift[l])
        lo = 3 * GROUP_W

        def lora_w(w, off):
            return jnp.zeros((GROUP_W, GROUP_W), F32).at[off:off + w.shape[0]].set(w).astype(BF16)

        w2_p = jnp.stack([lora_w(rwkv_w2[l, 0], 0), lora_w(rwkv_w2[l, 1], 32)])
        a2_p = jnp.stack([lora_w(rwkv_a2[l, 0], 64), lora_w(rwkv_a2[l, 1], 96)])
        g2_p = lora_w(rwkv_g2[l], 128)
        feat_vecs = jnp.zeros((SUBLANES, GROUP_W), F32).at[0].set(rwkv_k_k[l]).at[1].set(rwkv_k_a[l]) \
            .at[2].set(rwkv_r_k[l].reshape(-1)).at[3:5].set(rwkv_w0[l]).at[5:7].set(rwkv_a0[l])
        post_vecs = jnp.zeros((SUBLANES, GROUP_W), F32).at[0].set(rwkv_lnx_g[l]).at[1].set(rwkv_lnx_b[l])

        h = _ffn(h, m_lat, norm_ffn1[l], w_gu1, w_dn1, 0)
        hc = _ffn(hc, m_ctx, norm_ffn1[l], w_gu1, w_dn1, 0)

        ua, ub, un, ud = _inproj(h, m_lat, norm_mix[l], w_in_p)
        uca, ucb, ucn, ucd = _inproj(hc, m_ctx, norm_mix[l], w_in_p)

        qa, ka, vta = _mla_prep(ua, mla_tab, *mla_small)
        qca, kca, vtca = _mla_prep(uca, mla_tab_c, *mla_small)
        oa = _attention(qa, ka, vta, kca, vtca)

        qn_, kn_, vn_ = _qkv_prep(un, seg, na_gq, na_gk, 4, 4, 4)
        qcn, kcn, vcn, vtcn = _qkv_prep(ucn, seg, na_gq, na_gk, 4, 4, 4, v_cols=True)
        on = _na_attention(qn_, kn_, vn_, kcn, vcn, na_tb)

        qd, kd, vtd = _qkv_prep(ud, seg, gqa_gq, gqa_gk, 4, 2, 2, gqa_tab, v_rows=False, v_cols=True)
        qcd, kcd, vtcd = _qkv_prep(ucd, seg, gqa_gq, gqa_gk, 4, 2, 2, gqa_tab_c, v_rows=False, v_cols=True)
        od = _attention(qd, kd, vtd, kcd, vtcd)

        feats_c = _rwkv_features(ucb, mu_p, seg, w2_p, a2_p, g2_p, feat_vecs)
        feats = _rwkv_features(ub, mu_p, seg, w2_p, a2_p, g2_p, feat_vecs)
        trans_c = _rwkv_chunks(feats_c[:9], cum, tri, hm)
        trans = _rwkv_chunks(feats[:9], cum, tri, hm)
        s0 = jnp.zeros((B, 2, HEAD_DIM, GROUP_W), F32)
        yfc, ybc, s_ctx = _rwkv_scan(s0, hm, *trans_c)
        yf, yb, _ = _rwkv_scan(s_ctx, hm, *trans)
        ob = _rwkv_post(yf, yb, feats[9], feats[10], seg, post_vecs)

        h = _outproj(h, m_lat, oa, ob, on, od, w_out_b)
        if ctx_out:
            oca = _attention(qca, kca, vtca)
            ocn = _attention(qcn, kcn, vtcn)
            ocd = _attention(qcd, kcd, vtcd)
            ocb = _rwkv_post(yfc, ybc, feats_c[9], feats_c[10], seg, post_vecs)
            hc = _outproj(hc, m_ctx, oca, ocb, ocn, ocd, w_out_b)

        h = _ffn(h, m_lat, norm_ffn2[l], w_gu2, w_dn2, 6)
        if ctx_out:
            hc = _ffn(hc, m_ctx, norm_ffn2[l], w_gu2, w_dn2, 6)
    return h
```

```python
import functools

import numpy as np
import jax
import jax.numpy as jnp
from jax import lax
from jax.experimental import pallas as pl
from jax.experimental.pallas import tpu as pltpu

F32 = jnp.float32
BF16 = jnp.bfloat16

GRID_W = 64
HEAD_DIM = 64
GROUP_W = 256
ROPE_THETA = 10000.0
NORM_EPS = 1e-6
D_FF = 2816
MLA_HEADS = 4
MLA_Q_RANK = 256
MLA_KV_RANK = 128
MLA_NOPE = 64
MLA_ROPE = 32
MLA_V = 64
MLA_QK = MLA_NOPE + MLA_ROPE
RWKV_COLS = 960
RWKV_LNX_EPS = 64e-5
NA_WIN_R = 8
NA_WIN_C = 16
MLA_COLS = 416
NA_COLS = 768
GQA_COLS = 512

LANES = 128
SUBLANES = 8
VMEM_LIMIT_BYTES = 56 * 1024 * 1024

UA_W = 512
UB_W = 1024
UN_W = 768
UD_W = 512

VT_ROWS = 128
CHUNK = 64
NEG_INF = -1e30
LOG2E = 1.4426950408889634
NA_ROWS_PER_ITER = 4
MAX_JUMP = 24.0


def _dot(a, b):
    return jnp.dot(a, b, preferred_element_type=F32)


def _dot_nt(a, b):
    return lax.dot_general(a, b, (((1,), (1,)), ((), ())), preferred_element_type=F32)


def _dot_tn(a, b):
    return lax.dot_general(a, b, (((0,), (0,)), ((), ())), preferred_element_type=F32)


def _split2(x):
    hi = x.astype(BF16)
    lo = (x - hi.astype(F32)).astype(BF16)
    return hi, lo


def _dot_exact_rhs(x, m):
    h1 = x.astype(BF16)
    r1 = x - h1.astype(F32)
    h2 = r1.astype(BF16)
    h3 = (r1 - h2.astype(F32)).astype(BF16)
    return _dot(h1, m) + _dot(h2, m) + _dot(h3, m)


def _dot_exact_lhs(m, x):
    h1 = x.astype(BF16)
    r1 = x - h1.astype(F32)
    h2 = r1.astype(BF16)
    h3 = (r1 - h2.astype(F32)).astype(BF16)
    return _dot(m, h1) + _dot(m, h2) + _dot(m, h3)


def _rms(x, eps=NORM_EPS):
    return x * lax.rsqrt(jnp.mean(x * x, axis=-1, keepdims=True) + eps)


def _cparams(sem):
    return pltpu.CompilerParams(dimension_semantics=sem, vmem_limit_bytes=VMEM_LIMIT_BYTES)


def _mod_kernel(c_ref, w_ref, b_ref, o_ref):
    c = c_ref[...]
    s = c * jax.nn.sigmoid(c)
    o_ref[...] = jnp.dot(s, w_ref[...], preferred_element_type=F32,
                         precision=lax.Precision.HIGHEST) + b_ref[...]


def _modulation(cvec, w_mod, b_mod):
    L, D, N = w_mod.shape
    tn = 1024
    return pl.pallas_call(
        _mod_kernel,
        grid=(L, N // tn),
        in_specs=[
            pl.BlockSpec((SUBLANES, D), lambda l, n: (0, 0)),
            pl.BlockSpec((None, D, tn), lambda l, n: (l, 0, n)),
            pl.BlockSpec((None, 1, tn), lambda l, n: (l, 0, n)),
        ],
        out_specs=pl.BlockSpec((None, SUBLANES, tn), lambda l, n: (l, 0, n)),
        out_shape=jax.ShapeDtypeStruct((L, SUBLANES, N), F32),
        compiler_params=_cparams(("parallel", "parallel")),
        name="adaln_mod",
    )(cvec, w_mod, b_mod.reshape(L, 1, N))


def _ffn_kernel(h_ref, mod_ref, g_ref, wg_ref, wu_ref, wd_ref, o_ref, *, base):
    x = h_ref[...]
    y = _rms(x) * g_ref[...]
    xn = (y * (1.0 + mod_ref[base + 1:base + 2, :]) + mod_ref[base:base + 1, :]).astype(BF16)
    g = _dot(xn, wg_ref[...])
    u = _dot(xn, wu_ref[...])
    a = (g * jax.nn.sigmoid(g) * u).astype(BF16)
    o_ref[...] = x + 0.5 * mod_ref[base + 2:base + 3, :] * _dot(a, wd_ref[...])


def _ffn(h, mod, gain, w_gu, w_down, base):
    B, T, D = h.shape
    F = w_down.shape[0]
    tm = min(T, 512)
    resident = pl.Buffered(1)
    return pl.pallas_call(
        functools.partial(_ffn_kernel, base=base),
        grid=(B, T // tm),
        in_specs=[
            pl.BlockSpec((None, tm, D), lambda b, i: (b, i, 0)),
            pl.BlockSpec((None, 9, D), lambda b, i: (b, 0, 0)),
            pl.BlockSpec((1, D), lambda b, i: (0, 0)),
            pl.BlockSpec((D, F), lambda b, i: (0, 0), pipeline_mode=resident),
            pl.BlockSpec((D, F), lambda b, i: (0, 1), pipeline_mode=resident),
            pl.BlockSpec((F, D), lambda b, i: (0, 0), pipeline_mode=resident),
        ],
        out_specs=pl.BlockSpec((None, tm, D), lambda b, i: (b, i, 0)),
        out_shape=jax.ShapeDtypeStruct((B, T, D), F32),
        compiler_params=_cparams(("parallel", "parallel")),
        name="swiglu_halfstep",
    )(h, mod, gain.reshape(1, D), w_gu, w_gu, w_down)


def _inproj_kernel(h_ref, mod_ref, g_ref, w_ref, oa_ref, ob_ref, on_ref, od_ref):
    y = _rms(h_ref[...]) * g_ref[...]
    xm = (y * (1.0 + mod_ref[4:5, :]) + mod_ref[3:4, :]).astype(BF16)
    o = 0
    for ref, w in ((oa_ref, UA_W), (ob_ref, UB_W), (on_ref, UN_W), (od_ref, UD_W)):
        ref[...] = _dot(xm, w_ref[:, o:o + w])
        o += w


def _inproj(h, mod, gain, w_in_p):
    B, T, D = h.shape
    tm = min(T, 512)
    W = w_in_p.shape[1]
    widths = (UA_W, UB_W, UN_W, UD_W)
    return pl.pallas_call(
        _inproj_kernel,
        grid=(B, T // tm),
        in_specs=[
            pl.BlockSpec((None, tm, D), lambda b, i: (b, i, 0)),
            pl.BlockSpec((None, 9, D), lambda b, i: (b, 0, 0)),
            pl.BlockSpec((1, D), lambda b, i: (0, 0)),
            pl.BlockSpec((D, W), lambda b, i: (0, 0)),
        ],
        out_specs=[pl.BlockSpec((None, tm, w), lambda b, i: (b, i, 0)) for w in widths],
        out_shape=[jax.ShapeDtypeStruct((B, T, w), F32) for w in widths],
        compiler_params=_cparams(("parallel", "parallel")),
        name="in_projection",
    )(h, mod, gain.reshape(1, D), w_in_p)


def _outproj_kernel(h_ref, mod_ref, oa_ref, ob_ref, on_ref, od_ref, w_ref, o_ref, *, transposed):
    acc = None
    for g, (ref, t) in enumerate(zip((oa_ref, ob_ref, on_ref, od_ref), transposed)):
        w = w_ref[g * GROUP_W:(g + 1) * GROUP_W, :]
        part = _dot_tn(ref[...], w) if t else _dot(ref[...], w)
        acc = part if acc is None else acc + part
    o_ref[...] = h_ref[...] + mod_ref[5:6, :] * acc


def _outproj(h, mod, groups, transposed, w_out):
    B, T, D = h.shape
    tm = min(T, 512)
    grp = pl.BlockSpec((None, tm, GROUP_W), lambda b, i: (b, i, 0))
    grp_t = pl.BlockSpec((None, GROUP_W, tm), lambda b, i: (b, 0, i))
    return pl.pallas_call(
        functools.partial(_outproj_kernel, transposed=tuple(transposed)),
        grid=(B, T // tm),
        in_specs=[
            pl.BlockSpec((None, tm, D), lambda b, i: (b, i, 0)),
            pl.BlockSpec((None, 9, D), lambda b, i: (b, 0, 0)),
        ] + [grp_t if t else grp for t in transposed] + [
            pl.BlockSpec((4 * GROUP_W, D), lambda b, i: (0, 0)),
        ],
        out_specs=pl.BlockSpec((None, tm, D), lambda b, i: (b, i, 0)),
        out_shape=jax.ShapeDtypeStruct((B, T, D), F32),
        compiler_params=_cparams(("parallel", "parallel")),
        name="out_projection",
    )(h, mod, *groups, w_out)


def _rope_tables(positions, width, group, n_rep):
    half = group // 2
    inv_freq = ROPE_THETA ** (-jnp.arange(half, dtype=F32) / half)

    def axis(pos):
        ang = pos.astype(F32)[:, None] * inv_freq[None, :]
        c, s = jnp.cos(ang), jnp.sin(ang)
        return (jnp.concatenate([c, c], -1), jnp.concatenate([jnp.zeros_like(s), s], -1),
                jnp.concatenate([-s, jnp.zeros_like(s)], -1))

    row, col = positions
    parts = [jnp.concatenate([a, b], -1) for a, b in zip(axis(row), axis(col))]
    return [jnp.tile(p, (1, n_rep)) for p in parts]


def _pad_lanes(t, lo, total, fill):
    T = t.shape[0]
    return jnp.concatenate([jnp.full((T, lo), fill, F32), t,
                            jnp.full((T, total - lo - t.shape[1]), fill, F32)], -1)


def _vt_ext(v):
    tm = v.shape[0]
    lane = lax.broadcasted_iota(jnp.int32, (tm, VT_ROWS - HEAD_DIM), 1)
    aux = jnp.where(lane == 0, 1.0, 0.0)
    return jnp.concatenate([v, aux], axis=-1).T.astype(BF16)


def _mla_prep_kernel(ua_ref, c_ref, s1_ref, s2_ref, qnorm_ref, kvnorm_ref, qn_ref, kn_ref,
                     wuq_ref, wuk_ref, wuv_ref, q_out, k_out, vt_out):
    ua = ua_ref[...]
    qc = (_rms(ua[:, :MLA_Q_RANK]) * qnorm_ref[...]).astype(BF16)
    kvc = (_rms(ua[:, MLA_Q_RANK:MLA_Q_RANK + MLA_KV_RANK]) * kvnorm_ref[...]).astype(BF16)
    k_rope = ua[:, 3 * LANES:4 * LANES]
    q_all = _dot(qc, wuq_ref[...])
    k_all = _dot(kvc, wuk_ref[...])
    v_all = _dot(kvc, wuv_ref[...])
    cos, s_dn, s_up = c_ref[...], s1_ref[...], s2_ref[...]
    half = MLA_ROPE // 4

    def rope(x):
        return x * cos + pltpu.roll(x, half, 1) * s_dn + pltpu.roll(x, LANES - half, 1) * s_up

    def headnorm(x, gain):
        ms = jnp.sum(x * x, axis=-1, keepdims=True) * (1.0 / MLA_QK)
        return x * lax.rsqrt(ms + NORM_EPS) * gain

    for h in range(MLA_HEADS):
        qh = headnorm(q_all[:, h * LANES:(h + 1) * LANES], qn_ref[...])
        q_out[h] = rope(qh).astype(BF16)
        kh = headnorm(k_all[:, h * LANES:(h + 1) * LANES] + k_rope, kn_ref[...])
        k_out[h] = rope(kh).astype(BF16)
        vt_out[h] = _vt_ext(v_all[:, h * MLA_V:(h + 1) * MLA_V])


def _mla_prep(ua, tables, q_norm, kv_norm, qn_p, kn_p, wuq_p, wuk_p, wuv):
    B, T, _ = ua.shape
    tm = min(T, 512)
    tab = pl.BlockSpec((tm, LANES), lambda b, i: (i, 0))
    full = lambda a: pl.BlockSpec(a.shape, lambda b, i: (0,) * a.ndim)
    small = [q_norm, kv_norm, qn_p, kn_p, wuq_p, wuk_p, wuv]
    return pl.pallas_call(
        _mla_prep_kernel,
        grid=(B, T // tm),
        in_specs=[pl.BlockSpec((None, tm, UA_W), lambda b, i: (b, i, 0)), tab, tab, tab]
        + [full(a) for a in small],
        out_specs=[
            pl.BlockSpec((None, MLA_HEADS, tm, LANES), lambda b, i: (b, 0, i, 0)),
            pl.BlockSpec((None, MLA_HEADS, tm, LANES), lambda b, i: (b, 0, i, 0)),
            pl.BlockSpec((None, MLA_HEADS, VT_ROWS, tm), lambda b, i: (b, 0, 0, i)),
        ],
        out_shape=[
            jax.ShapeDtypeStruct((B, MLA_HEADS, T, LANES), BF16),
            jax.ShapeDtypeStruct((B, MLA_HEADS, T, LANES), BF16),
            jax.ShapeDtypeStruct((B, MLA_HEADS, VT_ROWS, T), BF16),
        ],
        compiler_params=_cparams(("parallel", "parallel")),
        name="mla_prep",
    )(ua, *tables, *small)


def _qkv_prep_kernel(*refs, q_off, nq, nk, nv, use_rope, v_rows, v_cols):
    n_in = 7 if use_rope else 4
    x_ref, seg_ref, gq_ref, gk_ref = refs[:4]
    if use_rope:
        c_ref, s1_ref, s2_ref = refs[4:7]
    q_out, k_out = refs[n_in:n_in + 2]
    v_outs = list(refs[n_in + 2:])
    v_out = v_outs.pop(0) if v_rows else None
    vt_out = v_outs.pop(0) if v_cols else None
    x = x_ref[...]
    wq, wk, wv = nq * HEAD_DIM, nk * HEAD_DIM, nv * HEAD_DIM
    q = x[:, q_off:q_off + wq]
    k = x[:, q_off + wq:q_off + wq + wk]
    v = x[:, q_off + wq + wk:q_off + wq + wk + wv]

    def headnorm(t, gain, w):
        hi, lo = _split2(t * t)
        seg = seg_ref[0:w, 0:w]
        ms = (_dot(hi, seg) + _dot(lo, seg)) * (1.0 / HEAD_DIM)
        return t * lax.rsqrt(ms + NORM_EPS) * gain

    q = headnorm(q, gq_ref[...], wq)
    k = headnorm(k, gk_ref[...], wk)
    if use_rope:
        half = HEAD_DIM // 4

        def rope(t, w):
            return (t * c_ref[:, 0:w] + pltpu.roll(t, half, 1) * s1_ref[:, 0:w]
                    + pltpu.roll(t, w - half, 1) * s2_ref[:, 0:w])

        q = rope(q, wq)
        k = rope(k, wk)
    for h in range(nq):
        q_out[h] = q[:, h * HEAD_DIM:(h + 1) * HEAD_DIM].astype(BF16)
    for h in range(nk):
        k_out[h] = k[:, h * HEAD_DIM:(h + 1) * HEAD_DIM].astype(BF16)
    for h in range(nv):
        vh = v[:, h * HEAD_DIM:(h + 1) * HEAD_DIM]
        if v_rows:
            v_out[h] = vh.astype(BF16)
        if v_cols:
            vt_out[h] = _vt_ext(vh)


def _qkv_prep(x, seg, gq, gk, nq, nk, nv, tables=None, v_rows=True, v_cols=False):
    B, T, W = x.shape
    tm = min(T, 512)
    use_rope = tables is not None
    full = lambda a: pl.BlockSpec(a.shape, lambda b, i: (0,) * a.ndim)
    in_specs = [pl.BlockSpec((None, tm, W), lambda b, i: (b, i, 0)), full(seg), full(gq), full(gk)]
    args = [x, seg, gq, gk]
    if use_rope:
        in_specs += [pl.BlockSpec((tm, tables[0].shape[1]), lambda b, i: (i, 0))] * 3
        args += list(tables)
    hm = lambda n: pl.BlockSpec((None, n, tm, HEAD_DIM), lambda b, i: (b, 0, i, 0))
    out_specs = [hm(nq), hm(nk)]
    out_shape = [jax.ShapeDtypeStruct((B, n, T, HEAD_DIM), BF16) for n in (nq, nk)]
    if v_rows:
        out_specs.append(hm(nv))
        out_shape.append(jax.ShapeDtypeStruct((B, nv, T, HEAD_DIM), BF16))
    if v_cols:
        out_specs.append(pl.BlockSpec((None, nv, VT_ROWS, tm), lambda b, i: (b, 0, 0, i)))
        out_shape.append(jax.ShapeDtypeStruct((B, nv, VT_ROWS, T), BF16))
    return pl.pallas_call(
        functools.partial(_qkv_prep_kernel, q_off=0, nq=nq, nk=nk, nv=nv, use_rope=use_rope,
                          v_rows=v_rows, v_cols=v_cols),
        grid=(B, T // tm),
        in_specs=in_specs,
        out_specs=out_specs,
        out_shape=out_shape,
        compiler_params=_cparams(("parallel", "parallel")),
        name="qkv_prep_rope" if use_rope else "qkv_prep",
    )(*args)


def _attn_kernel(*refs, n_heads, group, has_extra, nk):
    if has_extra:
        q_ref, k_ref, vt_ref, kx_ref, vtx_ref, o_ref, m_ref, acc_ref = refs
    else:
        q_ref, k_ref, vt_ref, o_ref, m_ref, acc_ref = refs
    j = pl.program_id(2)

    heads = range(n_heads)

    def first(kr, vr):
        s = [_dot_nt(kr[h // group], q_ref[h]) for h in heads]
        m = [jnp.max(s[h], axis=0, keepdims=True) for h in heads]
        p = [jnp.exp2(s[h] - m[h]).astype(BF16) for h in heads]
        for h in heads:
            m_ref[h] = m[h]
            acc_ref[h] = _dot(vr[h // group], p[h])

    def update(kr, vr):
        m_prev = [m_ref[h] for h in heads]
        s = [_dot_nt(kr[h // group], q_ref[h]) for h in heads]
        pv = [_dot(vr[h // group], jnp.exp2(s[h] - m_prev[h]).astype(BF16)) for h in heads]
        m_blk = [jnp.max(s[h], axis=0, keepdims=True) for h in heads]
        jump = functools.reduce(jnp.maximum, [m_blk[h] - m_prev[h] for h in heads])
        safe = jnp.max(jump) <= MAX_JUMP

        @pl.when(safe)
        def _():
            for h in heads:
                m_new = jnp.maximum(m_prev[h], m_blk[h])
                m_ref[h] = m_new
                acc_ref[h] = (acc_ref[h] + pv[h]) * jnp.exp2(m_prev[h] - m_new)

        @pl.when(jnp.logical_not(safe))
        def _():
            s2 = [_dot_nt(kr[h // group], q_ref[h]) for h in heads]
            m_new = [jnp.maximum(m_ref[h], jnp.max(s2[h], axis=0, keepdims=True)) for h in heads]
            p2 = [jnp.exp2(s2[h] - m_new[h]).astype(BF16) for h in heads]
            for h in heads:
                acc_ref[h] = jnp.exp2(m_ref[h] - m_new[h]) * acc_ref[h] + _dot(vr[h // group], p2[h])
                m_ref[h] = m_new[h]

    if has_extra:
        @pl.when(j == 0)
        def _():
            first(kx_ref, vtx_ref)

        update(k_ref, vt_ref)
    else:
        @pl.when(j == 0)
        def _():
            first(k_ref, vt_ref)

        @pl.when(j > 0)
        def _():
            update(k_ref, vt_ref)

    @pl.when(j == nk - 1)
    def _():
        outs = []
        for h in range(n_heads):
            a = acc_ref[h]
            outs.append(a[:HEAD_DIM] / a[HEAD_DIM:HEAD_DIM + 1])
        o_ref[...] = jnp.concatenate(outs, axis=0).astype(o_ref.dtype)


def _attention(q, k, vt, kx=None, vtx=None):
    B, H, Tq, dq = q.shape
    Hk, Tk = k.shape[1], k.shape[2]
    group = H // Hk
    tq = min(Tq, 512)
    tk = min(Tk, 1024)
    nk = Tk // tk
    has_extra = kx is not None
    in_specs = [
        pl.BlockSpec((None, H, tq, dq), lambda b, i, j: (b, 0, i, 0)),
        pl.BlockSpec((None, Hk, tk, dq), lambda b, i, j: (b, 0, j, 0)),
        pl.BlockSpec((None, Hk, VT_ROWS, tk), lambda b, i, j: (b, 0, 0, j)),
    ]
    args = [q, k, vt]
    if has_extra:
        Tx = kx.shape[2]
        in_specs += [pl.BlockSpec((None, Hk, Tx, dq), lambda b, i, j: (b, 0, 0, 0)),
                     pl.BlockSpec((None, Hk, VT_ROWS, Tx), lambda b, i, j: (b, 0, 0, 0))]
        args += [kx, vtx]
    return pl.pallas_call(
        functools.partial(_attn_kernel, n_heads=H, group=group, has_extra=has_extra, nk=nk),
        grid=(B, Tq // tq, nk),
        in_specs=in_specs,
        out_specs=pl.BlockSpec((None, H * HEAD_DIM, tq), lambda b, i, j: (b, 0, i)),
        out_shape=jax.ShapeDtypeStruct((B, H * HEAD_DIM, Tq), BF16),
        scratch_shapes=[pltpu.VMEM((H, 1, tq), F32), pltpu.VMEM((H, VT_ROWS, tq), F32)],
        compiler_params=_cparams(("parallel", "parallel", "arbitrary")),
        name="flash_attention_ctx" if has_extra else "flash_attention",
    )(*args)


def _na_bias_kernel(r_ref, e_ref, m_ref, o_ref):
    o_ref[...] = _dot_exact_rhs(r_ref[...], e_ref[...]) + m_ref[...]


def _na_bias_tables(rpb):
    H, n_dr, n_dc = rpb.shape
    W = GRID_W
    qc = np.arange(W)[:, None]
    kc = np.arange(W)[None, :]
    c0 = np.clip(qc - NA_WIN_C // 2, 0, W - NA_WIN_C)
    valid = (kc >= c0) & (kc < c0 + NA_WIN_C)
    d = kc - qc + NA_WIN_C - 1
    onehot = np.zeros((LANES, W * W), np.float32)
    for a in range(W):
        for b in range(W):
            if valid[a, b]:
                onehot[d[a, b], a * W + b] = 1.0
    mask = np.where(valid, 0.0, NEG_INF).astype(np.float32).reshape(1, W * W)
    rows = H * n_dr
    rows_p = -(-rows // SUBLANES) * SUBLANES
    r2 = jnp.zeros((rows_p, LANES), F32).at[:rows, :n_dc].set(rpb.reshape(rows, n_dc) * LOG2E)
    tn = 1024
    tiles = pl.pallas_call(
        _na_bias_kernel,
        grid=(W * W // tn,),
        in_specs=[pl.BlockSpec((rows_p, LANES), lambda n: (0, 0)),
                  pl.BlockSpec((LANES, tn), lambda n: (0, n)),
                  pl.BlockSpec((1, tn), lambda n: (0, n))],
        out_specs=pl.BlockSpec((rows_p, tn), lambda n: (0, n)),
        out_shape=jax.ShapeDtypeStruct((rows_p, W * W), F32),
        compiler_params=_cparams(("parallel",)),
        name="na_bias_expand",
    )(r2, jnp.asarray(onehot, BF16), jnp.asarray(mask))
    tiles = tiles[:rows].reshape(H, n_dr, W, W)
    slabs = [tiles[:, base:base + NA_WIN_R].transpose(0, 2, 1, 3).reshape(H, W, NA_WIN_R * W)
             for base in range(NA_WIN_R)]
    return jnp.stack(slabs, 0)


def _na_kernel(q_ref, k_ref, v_ref, kc_ref, vc_ref, tb_ref, o_ref, *, rb, rows, n_heads):
    blk = pl.program_id(1)
    W = GRID_W
    win = NA_WIN_R * W

    nr = NA_ROWS_PER_ITER

    def rows_body(i, carry):
        q0 = pl.multiple_of(i * nr * W, nr * W)
        heads = range(n_heads)
        units = [(j, h) for j in range(nr) for h in heads]
        k0, base = [], []
        for j in range(nr):
            r = blk * rb + nr * i + j
            r0 = jnp.clip(r - NA_WIN_R // 2, 0, rows - NA_WIN_R)
            base.append(r0 - r + NA_WIN_R - 1)
            k0.append(pl.multiple_of(r0 * W, W))
        qs = [q_ref[h, pl.ds(q0, nr * W), :] for h in heads]
        s_cxs = [_dot_nt(qs[h], kc_ref[h]) for h in heads]
        s_nb = [_dot_nt(qs[h][j * W:(j + 1) * W], k_ref[h, pl.ds(k0[j], win), :]) + tb_ref[base[j], h]
                for j, h in units]
        s_cx = [s_cxs[h][j * W:(j + 1) * W] for j, h in units]
        n = len(units)
        m = [jnp.maximum(jnp.max(s_nb[u], axis=-1, keepdims=True),
                         jnp.max(s_cx[u], axis=-1, keepdims=True)) for u in range(n)]
        p_nb = [jnp.exp2(s_nb[u] - m[u]) for u in range(n)]
        p_cx = [jnp.exp2(s_cx[u] - m[u]) for u in range(n)]
        l = [jnp.sum(p_nb[u], axis=-1, keepdims=True) + jnp.sum(p_cx[u], axis=-1, keepdims=True)
             for u in range(n)]
        o_cxs = [_dot(jnp.concatenate([p_cx[j * n_heads + h] for j in range(nr)], axis=0).astype(BF16),
                      vc_ref[h]) for h in heads]
        o = [_dot(p_nb[u].astype(BF16), v_ref[h, pl.ds(k0[j], win), :]) + o_cxs[h][j * W:(j + 1) * W]
             for u, (j, h) in enumerate(units)]
        o_ref[pl.ds(q0, nr * W), :] = jnp.concatenate(
            [jnp.concatenate([o[j * n_heads + h] / l[j * n_heads + h] for h in heads], axis=-1)
             for j in range(nr)], axis=0).astype(o_ref.dtype)
        return carry

    lax.fori_loop(0, rb // nr, rows_body, 0)


def _na_attention(q, k, v, kc, vc, tb):
    B, H, S, dh = q.shape
    L = kc.shape[2]
    rows = S // GRID_W
    assert rows >= NA_WIN_R
    rb = 8
    return pl.pallas_call(
        functools.partial(_na_kernel, rb=rb, rows=rows, n_heads=H),
        grid=(B, rows // rb),
        in_specs=[
            pl.BlockSpec((None, H, rb * GRID_W, dh), lambda b, i: (b, 0, i, 0)),
            pl.BlockSpec((None, H, S, dh), lambda b, i: (b, 0, 0, 0)),
            pl.BlockSpec((None, H, S, dh), lambda b, i: (b, 0, 0, 0)),
            pl.BlockSpec((None, H, L, dh), lambda b, i: (b, 0, 0, 0)),
            pl.BlockSpec((None, H, L, dh), lambda b, i: (b, 0, 0, 0)),
            pl.BlockSpec(tb.shape, lambda b, i: (0, 0, 0, 0)),
        ],
        out_specs=pl.BlockSpec((None, rb * GRID_W, H * dh), lambda b, i: (b, i, 0)),
        out_shape=jax.ShapeDtypeStruct((B, S, H * dh), BF16),
        compiler_params=_cparams(("parallel", "arbitrary")),
        name="neighbourhood_attention",
    )(q, k, v, kc, vc, tb)


def _rwkv_feat_kernel(x_ref, xp_ref, xn_ref, mu_ref, seg_ref, w2_ref, a2_ref, g2_ref, vec_ref,
                      r_out, v_out, kk_out, lwf_out, lwb_out, kdf_out, kdb_out, af_out, ab_out,
                      bon_out, g_out, *, nblk):
    i = pl.program_id(1)
    x = x_ref[...]
    tm = x.shape[0]
    has_prev = jnp.where(i > 0, 1.0, 0.0)
    has_next = jnp.where(i < nblk - 1, 1.0, 0.0)
    row = lax.broadcasted_iota(jnp.int32, (tm, 1), 0)
    prev = jnp.where(row == 0, xp_ref[SUBLANES - 1:SUBLANES, :] * has_prev, pltpu.roll(x, 1, 0))
    nxt = jnp.where(row == tm - 1, xn_ref[0:1, :] * has_next, pltpu.roll(x, tm - 1, 0))
    xs = x + mu_ref[0:1, :] * (prev - x) + mu_ref[1:2, :] * (nxt - x)

    G = GROUP_W
    r, k, v = xs[:, 0:G], xs[:, G:2 * G], xs[:, 2 * G:3 * G]
    lora = xs[:, 3 * G:4 * G]
    seg = seg_ref[...]

    def segsum(t):
        hi, lo = _split2(t)
        return _dot(hi, seg) + _dot(lo, seg)

    k_k, k_a, r_k = vec_ref[0:1, :], vec_ref[1:2, :], vec_ref[2:3, :]
    kk = k * k_k
    kk = kk * lax.rsqrt(segsum(kk * kk) + 1e-12)
    g_out[...] = _dot(jax.nn.sigmoid(lora).astype(BF16), g2_ref[...])
    tanh_l = jnp.tanh(lora).astype(BF16)
    lora_b = lora.astype(BF16)
    kd_sum = None
    for d, (lw_out, kd_out, a_out) in enumerate(((lwf_out, kdf_out, af_out), (lwb_out, kdb_out, ab_out))):
        w0, a0 = vec_ref[3 + d:4 + d, :], vec_ref[5 + d:6 + d, :]
        z = w0 + _dot(tanh_l, w2_ref[d])
        w_log = -(jnp.maximum(-z, 0.0) + jnp.log(1.0 + jnp.exp(-jnp.abs(z)))) - 0.5
        lw_out[...] = -jnp.exp(w_log)
        a = jax.nn.sigmoid(a0 + _dot(lora_b, a2_ref[d]))
        kd = k * (1.0 + (a - 1.0) * k_a)
        kd_out[...] = kd
        a_out[...] = a
        kd_sum = kd if kd_sum is None else kd_sum + kd
    r_out[...] = r
    v_out[...] = v
    kk_out[...] = kk
    bon_out[...] = segsum(r * kd_sum * r_k) * v


def _rwkv_features(ub, mu_p, seg, w2_p, a2_p, g2_p, vecs):
    B, T, W = ub.shape
    tm = min(T, 256)
    nblk = T // tm
    hb = tm // SUBLANES
    nh = T // SUBLANES
    full = lambda a: pl.BlockSpec(a.shape, lambda b, i: (0,) * a.ndim)
    out = pl.BlockSpec((None, tm, GROUP_W), lambda b, i: (b, i, 0))
    return pl.pallas_call(
        functools.partial(_rwkv_feat_kernel, nblk=nblk),
        grid=(B, nblk),
        in_specs=[
            pl.BlockSpec((None, tm, W), lambda b, i: (b, i, 0)),
            pl.BlockSpec((None, SUBLANES, W), lambda b, i: (b, jnp.maximum(i * hb - 1, 0), 0)),
            pl.BlockSpec((None, SUBLANES, W), lambda b, i: (b, jnp.minimum((i + 1) * hb, nh - 1), 0)),
            full(mu_p), full(seg), full(w2_p), full(a2_p), full(g2_p), full(vecs),
        ],
        out_specs=[out] * 11,
        out_shape=[jax.ShapeDtypeStruct((B, T, GROUP_W), F32)] * 11,
        compiler_params=_cparams(("parallel", "parallel")),
        name="rwkv_features",
    )(ub, ub, ub, mu_p, seg, w2_p, a2_p, g2_p, vecs)


def _rwkv_chunk_kernel(r_ref, v_ref, kk_ref, lwf_ref, lwb_ref, kdf_ref, kdb_ref, af_ref, ab_ref,
                       cum_ref, tri_ref, hm_ref,
                       rh_out, y1_out, mp_out, np_out, pc_out, *, nsub):
    C = CHUNK
    hm = hm_ref[...]
    eye_w = tri_ref[2]
    lw_refs, kd_refs, a_refs = (lwf_ref, lwb_ref), (kdf_ref, kdb_ref), (af_ref, ab_ref)

    def diag(t):
        return (jnp.concatenate([t, t, t, t], axis=0) * hm).astype(BF16)

    def rows2(a, b):
        return jnp.concatenate([a, b], axis=0).astype(BF16)

    chains = [(ci, d) for ci in range(nsub) for d in range(2)]
    n = len(chains)
    rng = range(n)
    rows = [slice(ci * C, (ci + 1) * C) for ci, _ in chains]

    at, rt, vv, bh, kh, b_d, k_d, e_end = ([None] * n for _ in range(8))
    for i, (ci, d) in enumerate(chains):
        rs = rows[i]
        r, kk = r_ref[rs, :], kk_ref[rs, :]
        lw, kd, a = lw_refs[d][rs, :], kd_refs[d][rs, :], a_refs[d][rs, :]
        cl = _dot_exact_lhs(cum_ref[d], lw)
        last = C - 1 if d == 0 else 0
        e_end[i] = jnp.exp(cl[last:last + 1, :])
        e_neg = jnp.exp(-cl)
        bt = kk * a * e_neg
        kt = kd * e_neg
        vv[i] = v_ref[rs, :]
        at[i] = -kk * jnp.exp(cl - lw)
        rt[i] = r * jnp.exp(cl)
        bh[i], kh[i] = bt * e_end[i], kt * e_end[i]
        b_d[i], k_d[i] = diag(bt), diag(kt)
    a_d = [diag(at[i]) for i in rng]
    v_d = [diag(vv[i]) for i in rng]

    strict = [tri_ref[d] for _, d in chains]
    ar = [rows2(at[i], rt[i]) for i in rng]
    gb = [_dot_nt(ar[i], b_d[i]) for i in rng]
    gk = [_dot_nt(ar[i], k_d[i]) for i in rng]
    l_ab = [gb[i][:C] * strict[i] for i in rng]
    w_rb = [(gb[i][C:] * (strict[i] + eye_w)).astype(BF16) for i in rng]
    w_akrk = [rows2(gk[i][:C] * strict[i], gk[i][C:] * (strict[i] + eye_w)) for i in rng]

    pw = [_dot(l_ab[i].astype(BF16), diag(l_ab[i])) for i in rng]
    tm = [eye_w + l_ab[i] for i in rng]
    for _ in range(4):
        sq = [_dot(rows2(pw[i], tm[i]), diag(pw[i])) for i in rng]
        tm = [tm[i] + sq[i][C:] for i in rng]
        pw = [sq[i][:C] for i in rng]
    tm = [(tm[i] + _dot(tm[i].astype(BF16), diag(pw[i]))).astype(BF16) for i in rng]

    wt = [_dot(tm[i], a_d[i]) for i in rng]
    xy = [_dot(w_akrk[i], v_d[i]) for i in rng]
    u0 = [_dot(tm[i], diag(xy[i][:C])) for i in rng]
    rh = [rt[i] + _dot(w_rb[i], diag(wt[i])) for i in rng]
    y1 = [_dot(w_rb[i], diag(u0[i])) + xy[i][C:] for i in rng]
    for i, (ci, d) in enumerate(chains):
        rh_out[d, rows[i], :] = rh[i].astype(BF16)
        y1_out[d, rows[i], :] = y1[i]
        mp_out[d, ci] = (_dot_tn(wt[i].astype(BF16), bh[i].astype(BF16)) * hm).astype(BF16)
        np_bd = _dot_tn(rows2(u0[i], vv[i]), rows2(bh[i], kh[i])) * hm
        np_out[d, ci] = functools.reduce(
            jnp.add, [np_bd[h * HEAD_DIM:(h + 1) * HEAD_DIM] for h in range(4)])
        pc_out[d, ci] = jnp.broadcast_to(e_end[i], (SUBLANES, GROUP_W))


def _rwkv_chunks(feats, cum, tri, hm):
    r = feats[0]
    B, T, G = r.shape
    nch = T // CHUNK
    nsub = 4 if nch % 4 == 0 else 1
    tb = nsub * CHUNK
    tok = pl.BlockSpec((None, tb, G), lambda b, c: (b, c, 0))
    full = lambda a: pl.BlockSpec(a.shape, lambda b, c: (0,) * a.ndim)
    return pl.pallas_call(
        functools.partial(_rwkv_chunk_kernel, nsub=nsub),
        grid=(B, nch // nsub),
        in_specs=[tok] * 9 + [full(cum), full(tri), full(hm)],
        out_specs=[
            pl.BlockSpec((None, 2, tb, G), lambda b, c: (b, 0, c, 0)),
            pl.BlockSpec((None, 2, tb, G), lambda b, c: (b, 0, c, 0)),
            pl.BlockSpec((None, 2, nsub, G, G), lambda b, c: (b, 0, c, 0, 0)),
            pl.BlockSpec((None, 2, nsub, HEAD_DIM, G), lambda b, c: (b, 0, c, 0, 0)),
            pl.BlockSpec((None, 2, nsub, SUBLANES, G), lambda b, c: (b, 0, c, 0, 0)),
        ],
        out_shape=[
            jax.ShapeDtypeStruct((B, 2, T, G), BF16),
            jax.ShapeDtypeStruct((B, 2, T, G), F32),
            jax.ShapeDtypeStruct((B, 2, nch, G, G), BF16),
            jax.ShapeDtypeStruct((B, 2, nch, HEAD_DIM, G), F32),
            jax.ShapeDtypeStruct((B, 2, nch, SUBLANES, G), F32),
        ],
        compiler_params=_cparams(("parallel", "parallel")),
        name="rwkv_chunk_transitions",
    )(*feats, cum, tri, hm)


def _rwkv_scan_kernel(s0_ref, hm_ref, rhf_ref, y1f_ref, mpf_ref, npf_ref, pcf_ref,
                      rhb_ref, y1b_ref, mpb_ref, npb_ref, pcb_ref,
                      yf_out, yb_out, s_out, *, batch, nsub):
    c = pl.program_id(0)

    @pl.when(c == 0)
    def _():
        s_out[...] = s0_ref[...]

    hm = hm_ref[...]
    dirs = ((rhf_ref, y1f_ref, mpf_ref, npf_ref, pcf_ref, yf_out),
            (rhb_ref, y1b_ref, mpb_ref, npb_ref, pcb_ref, yb_out))
    chains = [(b, d) for b in range(batch) for d in range(2)]
    s = [s_out[b, d] for b, d in chains]
    for step in range(nsub):
        s_d = [(jnp.concatenate([t, t, t, t], axis=0) * hm).astype(BF16) for t in s]
        s_b = [t.astype(BF16) for t in s]
        for i, (b, d) in enumerate(chains):
            rh_ref, y1_ref, mp_ref, np_ref, pc_ref, y_out = dirs[d]
            ci = step if d == 0 else nsub - 1 - step
            rows = slice(ci * CHUNK, (ci + 1) * CHUNK)
            y_out[b, rows, :] = _dot_nt(rh_ref[b, rows, :], s_d[i]) + y1_ref[b, rows, :]
            s[i] = s[i] * pc_ref[b, ci, 0:1, :] + _dot(s_b[i], mp_ref[b, ci]) + np_ref[b, ci]
    for i, (b, d) in enumerate(chains):
        s_out[b, d] = s[i]


def _rwkv_scan(s0, hm, rh, y1, mp, npm, pc):
    B, _, T, G = rh.shape
    nch = T // CHUNK
    nsub = 4 if nch % 4 == 0 else 1
    nblk = nch // nsub
    tb = nsub * CHUNK
    fwd = lambda c: c
    bwd = lambda c: nblk - 1 - c

    def specs(order, d):
        return [
            pl.BlockSpec((B, None, tb, G), lambda c: (0, d, order(c), 0)),
            pl.BlockSpec((B, None, tb, G), lambda c: (0, d, order(c), 0)),
            pl.BlockSpec((B, None, nsub, G, G), lambda c: (0, d, order(c), 0, 0)),
            pl.BlockSpec((B, None, nsub, HEAD_DIM, G), lambda c: (0, d, order(c), 0, 0)),
            pl.BlockSpec((B, None, nsub, SUBLANES, G), lambda c: (0, d, order(c), 0, 0)),
        ]

    return pl.pallas_call(
        functools.partial(_rwkv_scan_kernel, batch=B, nsub=nsub),
        grid=(nblk,),
        in_specs=[pl.BlockSpec(s0.shape, lambda c: (0, 0, 0, 0)), pl.BlockSpec(hm.shape, lambda c: (0, 0))]
        + specs(fwd, 0) + specs(bwd, 1),
        out_specs=[
            pl.BlockSpec((B, tb, G), lambda c: (0, fwd(c), 0)),
            pl.BlockSpec((B, tb, G), lambda c: (0, bwd(c), 0)),
            pl.BlockSpec(s0.shape, lambda c: (0, 0, 0, 0)),
        ],
        out_shape=[
            jax.ShapeDtypeStruct((B, T, G), F32),
            jax.ShapeDtypeStruct((B, T, G), F32),
            jax.ShapeDtypeStruct(s0.shape, F32),
        ],
        compiler_params=_cparams(("arbitrary",)),
        name="rwkv_state_scan",
    )(s0, hm, rh, y1, mp, npm, pc, rh, y1, mp, npm, pc)


def _rwkv_post_kernel(yf_ref, yb_ref, bon_ref, g_ref, seg_ref, vec_ref, o_ref):
    y = yf_ref[...] + yb_ref[...]
    seg = seg_ref[...]

    def segmean(t):
        hi, lo = _split2(t)
        return (_dot(hi, seg) + _dot(lo, seg)) * (1.0 / HEAD_DIM)

    mu = segmean(y)
    yc = y - mu
    var = segmean(yc * yc)
    yn = yc * lax.rsqrt(var + RWKV_LNX_EPS) * vec_ref[0:1, :] + vec_ref[1:2, :]
    o_ref[...] = ((yn + bon_ref[...]) * g_ref[...]).astype(o_ref.dtype)


def _rwkv_post(yf, yb, bon, g, seg, vecs):
    B, T, G = yf.shape
    tm = min(T, 512)
    tok = pl.BlockSpec((None, tm, G), lambda b, i: (b, i, 0))
    full = lambda a: pl.BlockSpec(a.shape, lambda b, i: (0,) * a.ndim)
    return pl.pallas_call(
        _rwkv_post_kernel,
        grid=(B, T // tm),
        in_specs=[tok, tok, tok, tok, full(seg), full(vecs)],
        out_specs=tok,
        out_shape=jax.ShapeDtypeStruct((B, T, G), BF16),
        compiler_params=_cparams(("parallel", "parallel")),
        name="rwkv_output",
    )(yf, yb, bon, g, seg, vecs)


def _seg_matrix(width):
    i = np.arange(width) // HEAD_DIM
    return jnp.asarray((i[:, None] == i[None, :]).astype(np.float32), BF16)


def _scan_constants():
    C = CHUNK
    t = np.arange(4 * C)
    head, tok = t // C, t % C
    i = np.arange(C)
    lower = i[:, None] > tok[None, :]
    upper = i[:, None] < tok[None, :]
    eye = i[:, None] == tok[None, :]
    tri = jnp.asarray(np.stack([lower, upper, eye]).astype(np.float32))
    cum = jnp.asarray(np.stack([i[:, None] >= i[None, :], i[:, None] <= i[None, :]]).astype(np.float32), BF16)
    hm = jnp.asarray((head[:, None] == (np.arange(GROUP_W) // HEAD_DIM)[None, :]).astype(np.float32))
    return cum, tri, hm


def _pad_vec(v, lo, total):
    return jnp.zeros((total,), F32).at[lo:lo + v.shape[0]].set(v)


def kernel(x, c, ctx, c_ctx, w_mod, b_mod, norm_ffn1, ffn1_w_gu, ffn1_w_down, norm_mix, w_in, w_out, mla_q_norm, mla_kv_norm, mla_w_uq, mla_w_ukv, mla_qn, mla_kn, rwkv_shift, rwkv_w0, rwkv_w2, rwkv_a0, rwkv_a2, rwkv_g2, rwkv_k_k, rwkv_k_a, rwkv_r_k, rwkv_lnx_g, rwkv_lnx_b, na_qn, na_kn, na_rpb, gqa_qn, gqa_kn, norm_ffn2, ffn2_w_gu, ffn2_w_down):
    B, S, D = x.shape
    Lc = ctx.shape[1]
    depth = w_mod.shape[0]
    assert B + 1 <= SUBLANES and S % 512 == 0 and Lc % CHUNK == 0 and S % GRID_W == 0

    cvec = jnp.zeros((SUBLANES, D), F32).at[:B].set(c).at[B].set(c_ctx)
    mods = _modulation(cvec, w_mod, b_mod).reshape(depth, SUBLANES, 9, D)

    t = jnp.arange(S)
    pos = (t // GRID_W, t % GRID_W)
    zero = jnp.zeros((Lc,), jnp.int32)
    mla_tab, mla_tab_c = [
        [_pad_lanes(p, MLA_NOPE, LANES, fill) for p, fill in zip(_rope_tables(pp, MLA_ROPE, MLA_ROPE // 2, 1), (1.0, 0.0, 0.0))]
        for pp in (pos, (zero, zero))]
    gqa_tab, gqa_tab_c = [_rope_tables(pp, 4 * HEAD_DIM, HEAD_DIM // 2, 4) for pp in (pos, (zero, zero))]

    seg = _seg_matrix(GROUP_W)
    cum, tri, hm = _scan_constants()
    mla_scale = MLA_QK ** -0.5 * LOG2E
    dh_scale = HEAD_DIM ** -0.5 * LOG2E

    h, hc = x, ctx
    for l in range(depth):
        ctx_out = l < depth - 1
        m_lat = mods[l, :B]
        m_ctx = jnp.broadcast_to(mods[l, B], (B, 9, D))

        w_gu1, w_dn1 = ffn1_w_gu[l].astype(BF16), ffn1_w_down[l].astype(BF16)
        w_gu2, w_dn2 = ffn2_w_gu[l].astype(BF16), ffn2_w_down[l].astype(BF16)
        wi = w_in[l]
        zc = lambda n: jnp.zeros((D, n), F32)
        o_b = MLA_COLS
        o_n = o_b + RWKV_COLS
        o_d = o_n + NA_COLS
        w_in_p = jnp.concatenate([
            wi[:, :MLA_Q_RANK + MLA_KV_RANK], zc(64), wi[:, MLA_Q_RANK + MLA_KV_RANK:MLA_COLS], zc(32),
            wi[:, o_b:o_n], zc(UB_W - RWKV_COLS),
            wi[:, o_n:o_d],
            wi[:, o_d:o_d + GQA_COLS]], axis=1).astype(BF16)
        w_out_b = w_out[l].astype(BF16)

        wuq = mla_w_uq[l].reshape(MLA_Q_RANK, MLA_HEADS, MLA_QK)
        wuq_p = jnp.pad(wuq, ((0, 0), (0, 0), (0, LANES - MLA_QK))).reshape(MLA_Q_RANK, MLA_HEADS * LANES).astype(BF16)
        wukv = mla_w_ukv[l].reshape(MLA_KV_RANK, MLA_HEADS, MLA_NOPE + MLA_V)
        wuk_p = jnp.pad(wukv[..., :MLA_NOPE], ((0, 0), (0, 0), (0, LANES - MLA_NOPE))).reshape(MLA_KV_RANK, MLA_HEADS * LANES).astype(BF16)
        wuv = wukv[..., MLA_NOPE:].reshape(MLA_KV_RANK, MLA_HEADS * MLA_V).astype(BF16)
        mla_small = (mla_q_norm[l].reshape(1, -1), mla_kv_norm[l].reshape(1, -1),
                     _pad_vec(mla_qn[l] * mla_scale, 0, LANES).reshape(1, LANES),
                     _pad_vec(mla_kn[l], 0, LANES).reshape(1, LANES), wuq_p, wuk_p, wuv)

        na_gq = jnp.tile(na_qn[l] * dh_scale, 4).reshape(1, -1)
        na_gk = jnp.tile(na_kn[l], 4).reshape(1, -1)
        gqa_gq = jnp.tile(gqa_qn[l] * dh_scale, 4).reshape(1, -1)
        gqa_gk = jnp.tile(gqa_kn[l], 2).reshape(1, -1)
        na_tb = _na_bias_tables(na_rpb[l])

        mu_p = jnp.zeros((SUBLANES, UB_W), F32).at[:2, :RWKV_COLS].set(rwkv_shift[l])
        lo = 3 * GROUP_W

        def lora_w(w, off):
            return jnp.zeros((GROUP_W, GROUP_W), F32).at[off:off + w.shape[0]].set(w).astype(BF16)

        w2_p = jnp.stack([lora_w(rwkv_w2[l, 0], 0), lora_w(rwkv_w2[l, 1], 32)])
        a2_p = jnp.stack([lora_w(rwkv_a2[l, 0], 64), lora_w(rwkv_a2[l, 1], 96)])
        g2_p = lora_w(rwkv_g2[l], 128)
        feat_vecs = jnp.zeros((SUBLANES, GROUP_W), F32).at[0].set(rwkv_k_k[l]).at[1].set(rwkv_k_a[l]) \
            .at[2].set(rwkv_r_k[l].reshape(-1)).at[3:5].set(rwkv_w0[l]).at[5:7].set(rwkv_a0[l])
        post_vecs = jnp.zeros((SUBLANES, GROUP_W), F32).at[0].set(rwkv_lnx_g[l]).at[1].set(rwkv_lnx_b[l])

        h = _ffn(h, m_lat, norm_ffn1[l], w_gu1, w_dn1, 0)
        hc = _ffn(hc, m_ctx, norm_ffn1[l], w_gu1, w_dn1, 0)

        ua, ub, un, ud = _inproj(h, m_lat, norm_mix[l], w_in_p)
        uca, ucb, ucn, ucd = _inproj(hc, m_ctx, norm_mix[l], w_in_p)

        qa, ka, vta = _mla_prep(ua, mla_tab, *mla_small)
        qca, kca, vtca = _mla_prep(uca, mla_tab_c, *mla_small)
        oa = _attention(qa, ka, vta, kca, vtca)

        qn_, kn_, vn_ = _qkv_prep(un, seg, na_gq, na_gk, 4, 4, 4)
        qcn, kcn, vcn, vtcn = _qkv_prep(ucn, seg, na_gq, na_gk, 4, 4, 4, v_cols=True)
        on = _na_attention(qn_, kn_, vn_, kcn, vcn, na_tb)

        qd, kd, vtd = _qkv_prep(ud, seg, gqa_gq, gqa_gk, 4, 2, 2, gqa_tab, v_rows=False, v_cols=True)
        qcd, kcd, vtcd = _qkv_prep(ucd, seg, gqa_gq, gqa_gk, 4, 2, 2, gqa_tab_c, v_rows=False, v_cols=True)
        od = _attention(qd, kd, vtd, kcd, vtcd)

        feats_c = _rwkv_features(ucb, mu_p, seg, w2_p, a2_p, g2_p, feat_vecs)
        feats = _rwkv_features(ub, mu_p, seg, w2_p, a2_p, g2_p, feat_vecs)
        trans_c = _rwkv_chunks(feats_c[:9], cum, tri, hm)
        trans = _rwkv_chunks(feats[:9], cum, tri, hm)
        s0 = jnp.zeros((B, 2, HEAD_DIM, GROUP_W), F32)
        yfc, ybc, s_ctx = _rwkv_scan(s0, hm, *trans_c)
        yf, yb, _ = _rwkv_scan(s_ctx, hm, *trans)
        ob = _rwkv_post(yf, yb, feats[9], feats[10], seg, post_vecs)

        h = _outproj(h, m_lat, (oa, ob, on, od), (True, False, False, True), w_out_b)
        if ctx_out:
            oca = _attention(qca, kca, vtca)
            ocn = _attention(qcn, kcn, vtcn)
            ocd = _attention(qcd, kcd, vtcd)
            ocb = _rwkv_post(yfc, ybc, feats_c[9], feats_c[10], seg, post_vecs)
            hc = _outproj(hc, m_ctx, (oca, ocb, ocn, ocd), (True, False, True, True), w_out_b)

        h = _ffn(h, m_lat, norm_ffn2[l], w_gu2, w_dn2, 6)
        if ctx_out:
            hc = _ffn(hc, m_ctx, norm_ffn2[l], w_gu2, w_dn2, 6)
    return h
```

```python
import functools

import numpy as np
import jax
import jax.numpy as jnp
from jax import lax
from jax.experimental import pallas as pl
from jax.experimental.pallas import tpu as pltpu

F32 = jnp.float32
BF16 = jnp.bfloat16

GRID_W = 64
HEAD_DIM = 64
GROUP_W = 256
ROPE_THETA = 10000.0
NORM_EPS = 1e-6
D_FF = 2816
MLA_HEADS = 4
MLA_Q_RANK = 256
MLA_KV_RANK = 128
MLA_NOPE = 64
MLA_ROPE = 32
MLA_V = 64
MLA_QK = MLA_NOPE + MLA_ROPE
RWKV_COLS = 960
RWKV_LNX_EPS = 64e-5
NA_WIN_R = 8
NA_WIN_C = 16
MLA_COLS = 416
NA_COLS = 768
GQA_COLS = 512

LANES = 128
SUBLANES = 8
VMEM_LIMIT_BYTES = 56 * 1024 * 1024

UA_W = 512
UB_W = 1024
UN_W = 768
UD_W = 512

VT_ROWS = 128
CHUNK = 64
NEG_INF = -1e30
LOG2E = 1.4426950408889634
NA_ROWS_PER_ITER = 4
MAX_JUMP = 24.0


def _dot(a, b):
    return jnp.dot(a, b, preferred_element_type=F32)


def _dot_nt(a, b):
    return lax.dot_general(a, b, (((1,), (1,)), ((), ())), preferred_element_type=F32)


def _dot_tn(a, b):
    return lax.dot_general(a, b, (((0,), (0,)), ((), ())), preferred_element_type=F32)


def _split2(x):
    hi = x.astype(BF16)
    lo = (x - hi.astype(F32)).astype(BF16)
    return hi, lo


def _dot_exact_rhs(x, m):
    h1 = x.astype(BF16)
    r1 = x - h1.astype(F32)
    h2 = r1.astype(BF16)
    h3 = (r1 - h2.astype(F32)).astype(BF16)
    return _dot(h1, m) + _dot(h2, m) + _dot(h3, m)


def _dot_exact_lhs(m, x):
    h1 = x.astype(BF16)
    r1 = x - h1.astype(F32)
    h2 = r1.astype(BF16)
    h3 = (r1 - h2.astype(F32)).astype(BF16)
    return _dot(m, h1) + _dot(m, h2) + _dot(m, h3)


def _rms(x, eps=NORM_EPS):
    return x * lax.rsqrt(jnp.mean(x * x, axis=-1, keepdims=True) + eps)


def _cparams(sem):
    return pltpu.CompilerParams(dimension_semantics=sem, vmem_limit_bytes=VMEM_LIMIT_BYTES)


def _mod_kernel(c_ref, w_ref, b_ref, o_ref):
    c = c_ref[...]
    s = c * jax.nn.sigmoid(c)
    o_ref[...] = jnp.dot(s, w_ref[...], preferred_element_type=F32,
                         precision=lax.Precision.HIGHEST) + b_ref[...]


def _modulation(cvec, w_mod, b_mod):
    L, D, N = w_mod.shape
    tn = 1024
    return pl.pallas_call(
        _mod_kernel,
        grid=(L, N // tn),
        in_specs=[
            pl.BlockSpec((SUBLANES, D), lambda l, n: (0, 0)),
            pl.BlockSpec((None, D, tn), lambda l, n: (l, 0, n)),
            pl.BlockSpec((None, 1, tn), lambda l, n: (l, 0, n)),
        ],
        out_specs=pl.BlockSpec((None, SUBLANES, tn), lambda l, n: (l, 0, n)),
        out_shape=jax.ShapeDtypeStruct((L, SUBLANES, N), F32),
        compiler_params=_cparams(("parallel", "parallel")),
        name="adaln_mod",
    )(cvec, w_mod, b_mod.reshape(L, 1, N))


def _ffn_kernel(h_ref, mod_ref, g_ref, wg_ref, wu_ref, wd_ref, o_ref, *, base):
    x = h_ref[...]
    y = _rms(x) * g_ref[...]
    xn = (y * (1.0 + mod_ref[base + 1:base + 2, :]) + mod_ref[base:base + 1, :]).astype(BF16)
    g = _dot(xn, wg_ref[...])
    u = _dot(xn, wu_ref[...])
    a = (g * jax.nn.sigmoid(g) * u).astype(BF16)
    o_ref[...] = x + 0.5 * mod_ref[base + 2:base + 3, :] * _dot(a, wd_ref[...])


def _ffn(h, mod, gain, w_gu, w_down, layer, base):
    B, T, D = h.shape
    F = w_down.shape[1]
    tm = min(T, 512)
    resident = pl.Buffered(1)
    return pl.pallas_call(
        functools.partial(_ffn_kernel, base=base),
        grid=(B, T // tm),
        in_specs=[
            pl.BlockSpec((None, tm, D), lambda b, i: (b, i, 0)),
            pl.BlockSpec((None, 9, D), lambda b, i: (b, 0, 0)),
            pl.BlockSpec((1, D), lambda b, i: (0, 0)),
            pl.BlockSpec((None, D, F), lambda b, i: (layer, 0, 0), pipeline_mode=resident),
            pl.BlockSpec((None, D, F), lambda b, i: (layer, 0, 1), pipeline_mode=resident),
            pl.BlockSpec((None, F, D), lambda b, i: (layer, 0, 0), pipeline_mode=resident),
        ],
        out_specs=pl.BlockSpec((None, tm, D), lambda b, i: (b, i, 0)),
        out_shape=jax.ShapeDtypeStruct((B, T, D), F32),
        compiler_params=_cparams(("parallel", "parallel")),
        name="swiglu_halfstep",
    )(h, mod, gain.reshape(1, D), w_gu, w_gu, w_down)


def _inproj_kernel(h_ref, mod_ref, g_ref, w_ref, oa_ref, ob_ref, on_ref, od_ref):
    y = _rms(h_ref[...]) * g_ref[...]
    xm = (y * (1.0 + mod_ref[4:5, :]) + mod_ref[3:4, :]).astype(BF16)
    o = 0
    for ref, w in ((oa_ref, UA_W), (ob_ref, UB_W), (on_ref, UN_W), (od_ref, UD_W)):
        ref[...] = _dot(xm, w_ref[:, o:o + w])
        o += w


def _inproj(h, mod, gain, w_in_p, layer):
    B, T, D = h.shape
    tm = min(T, 512)
    W = w_in_p.shape[2]
    widths = (UA_W, UB_W, UN_W, UD_W)
    return pl.pallas_call(
        _inproj_kernel,
        grid=(B, T // tm),
        in_specs=[
            pl.BlockSpec((None, tm, D), lambda b, i: (b, i, 0)),
            pl.BlockSpec((None, 9, D), lambda b, i: (b, 0, 0)),
            pl.BlockSpec((1, D), lambda b, i: (0, 0)),
            pl.BlockSpec((None, D, W), lambda b, i: (layer, 0, 0)),
        ],
        out_specs=[pl.BlockSpec((None, tm, w), lambda b, i: (b, i, 0)) for w in widths],
        out_shape=[jax.ShapeDtypeStruct((B, T, w), F32) for w in widths],
        compiler_params=_cparams(("parallel", "parallel")),
        name="in_projection",
    )(h, mod, gain.reshape(1, D), w_in_p)


def _rwkv_out(yf_ref, yb_ref, bon_ref, g_ref, seg_ref, vec_ref):
    y = yf_ref[...] + yb_ref[...]
    seg = seg_ref[...]

    def segmean(t):
        hi, lo = _split2(t)
        return (_dot(hi, seg) + _dot(lo, seg)) * (1.0 / HEAD_DIM)

    mu = segmean(y)
    yc = y - mu
    var = segmean(yc * yc)
    yn = yc * lax.rsqrt(var + RWKV_LNX_EPS) * vec_ref[0:1, :] + vec_ref[1:2, :]
    return ((yn + bon_ref[...]) * g_ref[...]).astype(BF16)


def _outproj_kernel(h_ref, mod_ref, oa_ref, yf_ref, yb_ref, bon_ref, g_ref, on_ref, od_ref,
                    seg_ref, vec_ref, w_ref, o_ref, *, transposed):
    ob = _rwkv_out(yf_ref, yb_ref, bon_ref, g_ref, seg_ref, vec_ref)
    groups = ((oa_ref, transposed[0]), (None, False), (on_ref, transposed[1]), (od_ref, transposed[2]))
    acc = None
    for g, (ref, t) in enumerate(groups):
        w = w_ref[g * GROUP_W:(g + 1) * GROUP_W, :]
        x = ob if ref is None else ref[...]
        part = _dot_tn(x, w) if t else _dot(x, w)
        acc = part if acc is None else acc + part
    o_ref[...] = h_ref[...] + mod_ref[5:6, :] * acc


def _outproj(h, mod, oa, rwkv, on, od, transposed, seg, post_vecs, w_out, layer):
    B, T, D = h.shape
    tm = min(T, 512)
    grp = pl.BlockSpec((None, tm, GROUP_W), lambda b, i: (b, i, 0))
    grp_t = pl.BlockSpec((None, GROUP_W, tm), lambda b, i: (b, 0, i))
    pick = lambda t: grp_t if t else grp
    full = lambda a: pl.BlockSpec(a.shape, lambda b, i: (0,) * a.ndim)
    return pl.pallas_call(
        functools.partial(_outproj_kernel, transposed=tuple(transposed)),
        grid=(B, T // tm),
        in_specs=[
            pl.BlockSpec((None, tm, D), lambda b, i: (b, i, 0)),
            pl.BlockSpec((None, 9, D), lambda b, i: (b, 0, 0)),
            pick(transposed[0]), grp, grp, grp, grp, pick(transposed[1]), pick(transposed[2]),
            full(seg), full(post_vecs),
            pl.BlockSpec((None, 4 * GROUP_W, D), lambda b, i: (layer, 0, 0)),
        ],
        out_specs=pl.BlockSpec((None, tm, D), lambda b, i: (b, i, 0)),
        out_shape=jax.ShapeDtypeStruct((B, T, D), F32),
        compiler_params=_cparams(("parallel", "parallel")),
        name="out_projection",
    )(h, mod, oa, *rwkv, on, od, seg, post_vecs, w_out)


def _rope_tables(positions, width, lead, group, n_rep):
    half = group // 2
    lane = np.arange(width)
    rel = (lane - lead) % (2 * group)
    inside = (lane >= lead) & (lane < lead + n_rep * 2 * group)
    is_row = (rel < group)[None, :]
    second = ((rel % group >= half) & inside)[None, :]
    first = ((rel % group < half) & inside)[None, :]
    inv_freq = ROPE_THETA ** (-jnp.arange(half, dtype=F32) / half)
    freq = jnp.where(inside, inv_freq[rel % half], 0.0)[None, :]
    row, col = positions
    ang = jnp.where(is_row, row.astype(F32)[:, None], col.astype(F32)[:, None]) * freq
    c, s = jnp.cos(ang), jnp.sin(ang)
    return [c, jnp.where(second, s, 0.0), jnp.where(first, -s, 0.0)]


def _identity_rope_tables(n_tokens, width):
    return [jnp.ones((n_tokens, width), F32), jnp.zeros((n_tokens, width), F32),
            jnp.zeros((n_tokens, width), F32)]


def _vt_ext(v):
    tm = v.shape[0]
    lane = lax.broadcasted_iota(jnp.int32, (tm, VT_ROWS - HEAD_DIM), 1)
    aux = jnp.where(lane == 0, 1.0, 0.0)
    return jnp.concatenate([v, aux], axis=-1).T.astype(BF16)


def _mla_prep_kernel(ua_ref, c_ref, s1_ref, s2_ref, qnorm_ref, kvnorm_ref, qn_ref, kn_ref,
                     wuq_ref, wuk_ref, wuv_ref, q_out, k_out, vt_out):
    ua = ua_ref[...]
    qc = (_rms(ua[:, :MLA_Q_RANK]) * qnorm_ref[...]).astype(BF16)
    kvc = (_rms(ua[:, MLA_Q_RANK:MLA_Q_RANK + MLA_KV_RANK]) * kvnorm_ref[...]).astype(BF16)
    k_rope = ua[:, 3 * LANES:4 * LANES]
    q_all = _dot(qc, wuq_ref[...])
    k_all = _dot(kvc, wuk_ref[...])
    v_all = _dot(kvc, wuv_ref[...])
    cos, s_dn, s_up = c_ref[...], s1_ref[...], s2_ref[...]
    half = MLA_ROPE // 4

    def rope(x):
        return x * cos + pltpu.roll(x, half, 1) * s_dn + pltpu.roll(x, LANES - half, 1) * s_up

    def headnorm(x, gain):
        ms = jnp.sum(x * x, axis=-1, keepdims=True) * (1.0 / MLA_QK)
        return x * lax.rsqrt(ms + NORM_EPS) * gain

    for h in range(MLA_HEADS):
        qh = headnorm(q_all[:, h * LANES:(h + 1) * LANES], qn_ref[...])
        q_out[h] = rope(qh).astype(BF16)
        kh = headnorm(k_all[:, h * LANES:(h + 1) * LANES] + k_rope, kn_ref[...])
        k_out[h] = rope(kh).astype(BF16)
        vt_out[h] = _vt_ext(v_all[:, h * MLA_V:(h + 1) * MLA_V])


def _mla_prep(ua, tables, q_norm, kv_norm, qn_p, kn_p, wuq_p, wuk_p, wuv):
    B, T, _ = ua.shape
    tm = min(T, 512)
    tab = pl.BlockSpec((tm, LANES), lambda b, i: (i, 0))
    full = lambda a: pl.BlockSpec(a.shape, lambda b, i: (0,) * a.ndim)
    small = [q_norm, kv_norm, qn_p, kn_p, wuq_p, wuk_p, wuv]
    return pl.pallas_call(
        _mla_prep_kernel,
        grid=(B, T // tm),
        in_specs=[pl.BlockSpec((None, tm, UA_W), lambda b, i: (b, i, 0)), tab, tab, tab]
        + [full(a) for a in small],
        out_specs=[
            pl.BlockSpec((None, MLA_HEADS, tm, LANES), lambda b, i: (b, 0, i, 0)),
            pl.BlockSpec((None, MLA_HEADS, tm, LANES), lambda b, i: (b, 0, i, 0)),
            pl.BlockSpec((None, MLA_HEADS, VT_ROWS, tm), lambda b, i: (b, 0, 0, i)),
        ],
        out_shape=[
            jax.ShapeDtypeStruct((B, MLA_HEADS, T, LANES), BF16),
            jax.ShapeDtypeStruct((B, MLA_HEADS, T, LANES), BF16),
            jax.ShapeDtypeStruct((B, MLA_HEADS, VT_ROWS, T), BF16),
        ],
        compiler_params=_cparams(("parallel", "parallel")),
        name="mla_prep",
    )(ua, *tables, *small)


def _qkv_prep_kernel(*refs, q_off, nq, nk, nv, use_rope, v_rows, v_cols):
    n_in = 7 if use_rope else 4
    x_ref, seg_ref, gq_ref, gk_ref = refs[:4]
    if use_rope:
        c_ref, s1_ref, s2_ref = refs[4:7]
    q_out, k_out = refs[n_in:n_in + 2]
    v_outs = list(refs[n_in + 2:])
    v_out = v_outs.pop(0) if v_rows else None
    vt_out = v_outs.pop(0) if v_cols else None
    x = x_ref[...]
    wq, wk, wv = nq * HEAD_DIM, nk * HEAD_DIM, nv * HEAD_DIM
    q = x[:, q_off:q_off + wq]
    k = x[:, q_off + wq:q_off + wq + wk]
    v = x[:, q_off + wq + wk:q_off + wq + wk + wv]

    def headnorm(t, gain, w):
        hi, lo = _split2(t * t)
        seg = seg_ref[0:w, 0:w]
        ms = (_dot(hi, seg) + _dot(lo, seg)) * (1.0 / HEAD_DIM)
        return t * lax.rsqrt(ms + NORM_EPS) * gain

    q = headnorm(q, gq_ref[...], wq)
    k = headnorm(k, gk_ref[...], wk)
    if use_rope:
        half = HEAD_DIM // 4

        def rope(t, w):
            return (t * c_ref[:, 0:w] + pltpu.roll(t, half, 1) * s1_ref[:, 0:w]
                    + pltpu.roll(t, w - half, 1) * s2_ref[:, 0:w])

        q = rope(q, wq)
        k = rope(k, wk)
    for h in range(nq):
        q_out[h] = q[:, h * HEAD_DIM:(h + 1) * HEAD_DIM].astype(BF16)
    for h in range(nk):
        k_out[h] = k[:, h * HEAD_DIM:(h + 1) * HEAD_DIM].astype(BF16)
    for h in range(nv):
        vh = v[:, h * HEAD_DIM:(h + 1) * HEAD_DIM]
        if v_rows:
            v_out[h] = vh.astype(BF16)
        if v_cols:
            vt_out[h] = _vt_ext(vh)


def _qkv_prep(x, seg, gq, gk, nq, nk, nv, tables=None, v_rows=True, v_cols=False):
    B, T, W = x.shape
    tm = min(T, 512)
    use_rope = tables is not None
    full = lambda a: pl.BlockSpec(a.shape, lambda b, i: (0,) * a.ndim)
    in_specs = [pl.BlockSpec((None, tm, W), lambda b, i: (b, i, 0)), full(seg), full(gq), full(gk)]
    args = [x, seg, gq, gk]
    if use_rope:
        in_specs += [pl.BlockSpec((tm, tables[0].shape[1]), lambda b, i: (i, 0))] * 3
        args += list(tables)
    hm = lambda n: pl.BlockSpec((None, n, tm, HEAD_DIM), lambda b, i: (b, 0, i, 0))
    out_specs = [hm(nq), hm(nk)]
    out_shape = [jax.ShapeDtypeStruct((B, n, T, HEAD_DIM), BF16) for n in (nq, nk)]
    if v_rows:
        out_specs.append(hm(nv))
        out_shape.append(jax.ShapeDtypeStruct((B, nv, T, HEAD_DIM), BF16))
    if v_cols:
        out_specs.append(pl.BlockSpec((None, nv, VT_ROWS, tm), lambda b, i: (b, 0, 0, i)))
        out_shape.append(jax.ShapeDtypeStruct((B, nv, VT_ROWS, T), BF16))
    return pl.pallas_call(
        functools.partial(_qkv_prep_kernel, q_off=0, nq=nq, nk=nk, nv=nv, use_rope=use_rope,
                          v_rows=v_rows, v_cols=v_cols),
        grid=(B, T // tm),
        in_specs=in_specs,
        out_specs=out_specs,
        out_shape=out_shape,
        compiler_params=_cparams(("parallel", "parallel")),
        name="qkv_prep_rope" if use_rope else "qkv_prep",
    )(*args)


def _attn_kernel(*refs, n_heads, group, has_extra, nk):
    if has_extra:
        q_ref, k_ref, vt_ref, kx_ref, vtx_ref, o_ref, m_ref, acc_ref = refs
    else:
        q_ref, k_ref, vt_ref, o_ref, m_ref, acc_ref = refs
    j = pl.program_id(2)

    heads = range(n_heads)

    def first(kr, vr):
        s = [_dot_nt(kr[h // group], q_ref[h]) for h in heads]
        m = [jnp.max(s[h], axis=0, keepdims=True) for h in heads]
        p = [jnp.exp2(s[h] - m[h]).astype(BF16) for h in heads]
        for h in heads:
            m_ref[h] = m[h]
            acc_ref[h] = _dot(vr[h // group], p[h])

    def update(kr, vr):
        m_prev = [m_ref[h] for h in heads]
        s = [_dot_nt(kr[h // group], q_ref[h]) for h in heads]
        pv = [_dot(vr[h // group], jnp.exp2(s[h] - m_prev[h]).astype(BF16)) for h in heads]
        m_blk = [jnp.max(s[h], axis=0, keepdims=True) for h in heads]
        jump = functools.reduce(jnp.maximum, [m_blk[h] - m_prev[h] for h in heads])
        safe = jnp.max(jump) <= MAX_JUMP

        @pl.when(safe)
        def _():
            for h in heads:
                m_new = jnp.maximum(m_prev[h], m_blk[h])
                m_ref[h] = m_new
                acc_ref[h] = (acc_ref[h] + pv[h]) * jnp.exp2(m_prev[h] - m_new)

        @pl.when(jnp.logical_not(safe))
        def _():
            s2 = [_dot_nt(kr[h // group], q_ref[h]) for h in heads]
            m_new = [jnp.maximum(m_ref[h], jnp.max(s2[h], axis=0, keepdims=True)) for h in heads]
            p2 = [jnp.exp2(s2[h] - m_new[h]).astype(BF16) for h in heads]
            for h in heads:
                acc_ref[h] = jnp.exp2(m_ref[h] - m_new[h]) * acc_ref[h] + _dot(vr[h // group], p2[h])
                m_ref[h] = m_new[h]

    if has_extra:
        @pl.when(j == 0)
        def _():
            first(kx_ref, vtx_ref)

        update(k_ref, vt_ref)
    else:
        @pl.when(j == 0)
        def _():
            first(k_ref, vt_ref)

        @pl.when(j > 0)
        def _():
            update(k_ref, vt_ref)

    @pl.when(j == nk - 1)
    def _():
        outs = []
        for h in range(n_heads):
            a = acc_ref[h]
            outs.append(a[:HEAD_DIM] / a[HEAD_DIM:HEAD_DIM + 1])
        o_ref[...] = jnp.concatenate(outs, axis=0).astype(o_ref.dtype)


def _attention(q, k, vt, kx=None, vtx=None):
    B, H, Tq, dq = q.shape
    Hk, Tk = k.shape[1], k.shape[2]
    group = H // Hk
    tq = min(Tq, 512)
    tk = min(Tk, 2048)
    nk = Tk // tk
    has_extra = kx is not None
    in_specs = [
        pl.BlockSpec((None, H, tq, dq), lambda b, i, j: (b, 0, i, 0)),
        pl.BlockSpec((None, Hk, tk, dq), lambda b, i, j: (b, 0, j, 0)),
        pl.BlockSpec((None, Hk, VT_ROWS, tk), lambda b, i, j: (b, 0, 0, j)),
    ]
    args = [q, k, vt]
    if has_extra:
        Tx = kx.shape[2]
        in_specs += [pl.BlockSpec((None, Hk, Tx, dq), lambda b, i, j: (b, 0, 0, 0)),
                     pl.BlockSpec((None, Hk, VT_ROWS, Tx), lambda b, i, j: (b, 0, 0, 0))]
        args += [kx, vtx]
    return pl.pallas_call(
        functools.partial(_attn_kernel, n_heads=H, group=group, has_extra=has_extra, nk=nk),
        grid=(B, Tq // tq, nk),
        in_specs=in_specs,
        out_specs=pl.BlockSpec((None, H * HEAD_DIM, tq), lambda b, i, j: (b, 0, i)),
        out_shape=jax.ShapeDtypeStruct((B, H * HEAD_DIM, Tq), BF16),
        scratch_shapes=[pltpu.VMEM((H, 1, tq), F32), pltpu.VMEM((H, VT_ROWS, tq), F32)],
        compiler_params=_cparams(("parallel", "parallel", "arbitrary")),
        name="flash_attention_ctx" if has_extra else "flash_attention",
    )(*args)


def _na_bias_kernel(r_ref, e_ref, m_ref, o_ref):
    o_ref[...] = _dot_exact_rhs(r_ref[...], e_ref[...]) + m_ref[...]


def _na_bias_tables(rpb):
    H, n_dr, n_dc = rpb.shape
    W = GRID_W
    qc = np.arange(W)[:, None]
    kc = np.arange(W)[None, :]
    c0 = np.clip(qc - NA_WIN_C // 2, 0, W - NA_WIN_C)
    valid = (kc >= c0) & (kc < c0 + NA_WIN_C)
    d = kc - qc + NA_WIN_C - 1
    onehot = np.zeros((LANES, W * W), np.float32)
    for a in range(W):
        for b in range(W):
            if valid[a, b]:
                onehot[d[a, b], a * W + b] = 1.0
    mask = np.where(valid, 0.0, NEG_INF).astype(np.float32).reshape(1, W * W)
    rows = H * n_dr
    rows_p = -(-rows // SUBLANES) * SUBLANES
    r2 = jnp.zeros((rows_p, LANES), F32).at[:rows, :n_dc].set(rpb.reshape(rows, n_dc) * LOG2E)
    tn = 1024
    tiles = pl.pallas_call(
        _na_bias_kernel,
        grid=(W * W // tn,),
        in_specs=[pl.BlockSpec((rows_p, LANES), lambda n: (0, 0)),
                  pl.BlockSpec((LANES, tn), lambda n: (0, n)),
                  pl.BlockSpec((1, tn), lambda n: (0, n))],
        out_specs=pl.BlockSpec((rows_p, tn), lambda n: (0, n)),
        out_shape=jax.ShapeDtypeStruct((rows_p, W * W), F32),
        compiler_params=_cparams(("parallel",)),
        name="na_bias_expand",
    )(r2, jnp.asarray(onehot, BF16), jnp.asarray(mask))
    tiles = tiles[:rows].reshape(H, n_dr, W, W)
    slabs = [tiles[:, base:base + NA_WIN_R].transpose(0, 2, 1, 3).reshape(H, W, NA_WIN_R * W)
             for base in range(NA_WIN_R)]
    return jnp.stack(slabs, 0)


def _na_kernel(q_ref, k_ref, v_ref, kc_ref, vc_ref, tb_ref, o_ref, *, rb, rows, n_heads):
    blk = pl.program_id(1)
    W = GRID_W
    win = NA_WIN_R * W

    nr = NA_ROWS_PER_ITER

    def rows_body(i, carry):
        q0 = pl.multiple_of(i * nr * W, nr * W)
        heads = range(n_heads)
        units = [(j, h) for j in range(nr) for h in heads]
        k0, base = [], []
        for j in range(nr):
            r = blk * rb + nr * i + j
            r0 = jnp.clip(r - NA_WIN_R // 2, 0, rows - NA_WIN_R)
            base.append(r0 - r + NA_WIN_R - 1)
            k0.append(pl.multiple_of(r0 * W, W))
        qs = [q_ref[h, pl.ds(q0, nr * W), :] for h in heads]
        s_cxs = [_dot_nt(qs[h], kc_ref[h]) for h in heads]
        s_nb = [_dot_nt(qs[h][j * W:(j + 1) * W], k_ref[h, pl.ds(k0[j], win), :]) + tb_ref[base[j], h]
                for j, h in units]
        s_cx = [s_cxs[h][j * W:(j + 1) * W] for j, h in units]
        n = len(units)
        m = [jnp.maximum(jnp.max(s_nb[u], axis=-1, keepdims=True),
                         jnp.max(s_cx[u], axis=-1, keepdims=True)) for u in range(n)]
        p_nb = [jnp.exp2(s_nb[u] - m[u]) for u in range(n)]
        p_cx = [jnp.exp2(s_cx[u] - m[u]) for u in range(n)]
        l = [jnp.sum(p_nb[u], axis=-1, keepdims=True) + jnp.sum(p_cx[u], axis=-1, keepdims=True)
             for u in range(n)]
        o_cxs = [_dot(jnp.concatenate([p_cx[j * n_heads + h] for j in range(nr)], axis=0).astype(BF16),
                      vc_ref[h]) for h in heads]
        o = [_dot(p_nb[u].astype(BF16), v_ref[h, pl.ds(k0[j], win), :]) + o_cxs[h][j * W:(j + 1) * W]
             for u, (j, h) in enumerate(units)]
        o_ref[pl.ds(q0, nr * W), :] = jnp.concatenate(
            [jnp.concatenate([o[j * n_heads + h] / l[j * n_heads + h] for h in heads], axis=-1)
             for j in range(nr)], axis=0).astype(o_ref.dtype)
        return carry

    lax.fori_loop(0, rb // nr, rows_body, 0)


def _na_attention(q, k, v, kc, vc, tb):
    B, H, S, dh = q.shape
    L = kc.shape[2]
    rows = S // GRID_W
    assert rows >= NA_WIN_R
    rb = 8
    return pl.pallas_call(
        functools.partial(_na_kernel, rb=rb, rows=rows, n_heads=H),
        grid=(B, rows // rb),
        in_specs=[
            pl.BlockSpec((None, H, rb * GRID_W, dh), lambda b, i: (b, 0, i, 0)),
            pl.BlockSpec((None, H, S, dh), lambda b, i: (b, 0, 0, 0)),
            pl.BlockSpec((None, H, S, dh), lambda b, i: (b, 0, 0, 0)),
            pl.BlockSpec((None, H, L, dh), lambda b, i: (b, 0, 0, 0)),
            pl.BlockSpec((None, H, L, dh), lambda b, i: (b, 0, 0, 0)),
            pl.BlockSpec(tb.shape, lambda b, i: (0, 0, 0, 0)),
        ],
        out_specs=pl.BlockSpec((None, rb * GRID_W, H * dh), lambda b, i: (b, i, 0)),
        out_shape=jax.ShapeDtypeStruct((B, S, H * dh), BF16),
        compiler_params=_cparams(("parallel", "arbitrary")),
        name="neighbourhood_attention",
    )(q, k, v, kc, vc, tb)


def _rwkv_feat_kernel(x_ref, xp_ref, xn_ref, mu_ref, seg_ref, w2_ref, a2_ref, g2_ref, vec_ref,
                      r_out, v_out, kk_out, lwf_out, lwb_out, kdf_out, kdb_out, af_out, ab_out,
                      bon_out, g_out, *, nblk):
    i = pl.program_id(1)
    x = x_ref[...]
    tm = x.shape[0]
    has_prev = jnp.where(i > 0, 1.0, 0.0)
    has_next = jnp.where(i < nblk - 1, 1.0, 0.0)
    row = lax.broadcasted_iota(jnp.int32, (tm, 1), 0)
    prev = jnp.where(row == 0, xp_ref[SUBLANES - 1:SUBLANES, :] * has_prev, pltpu.roll(x, 1, 0))
    nxt = jnp.where(row == tm - 1, xn_ref[0:1, :] * has_next, pltpu.roll(x, tm - 1, 0))
    xs = x + mu_ref[0:1, :] * (prev - x) + mu_ref[1:2, :] * (nxt - x)

    G = GROUP_W
    r, k, v = xs[:, 0:G], xs[:, G:2 * G], xs[:, 2 * G:3 * G]
    lora = xs[:, 3 * G:4 * G]
    seg = seg_ref[...]

    def segsum(t):
        hi, lo = _split2(t)
        return _dot(hi, seg) + _dot(lo, seg)

    k_k, k_a, r_k = vec_ref[0:1, :], vec_ref[1:2, :], vec_ref[2:3, :]
    kk = k * k_k
    kk = kk * lax.rsqrt(segsum(kk * kk) + 1e-12)
    g_out[...] = _dot(jax.nn.sigmoid(lora).astype(BF16), g2_ref[...])
    tanh_l = jnp.tanh(lora).astype(BF16)
    lora_b = lora.astype(BF16)
    kd_sum = None
    for d, (lw_out, kd_out, a_out) in enumerate(((lwf_out, kdf_out, af_out), (lwb_out, kdb_out, ab_out))):
        w0, a0 = vec_ref[3 + d:4 + d, :], vec_ref[5 + d:6 + d, :]
        z = w0 + _dot(tanh_l, w2_ref[d])
        w_log = -(jnp.maximum(-z, 0.0) + jnp.log(1.0 + jnp.exp(-jnp.abs(z)))) - 0.5
        lw_out[...] = -jnp.exp(w_log)
        a = jax.nn.sigmoid(a0 + _dot(lora_b, a2_ref[d]))
        kd = k * (1.0 + (a - 1.0) * k_a)
        kd_out[...] = kd
        a_out[...] = a
        kd_sum = kd if kd_sum is None else kd_sum + kd
    r_out[...] = r
    v_out[...] = v
    kk_out[...] = kk
    bon_out[...] = segsum(r * kd_sum * r_k) * v


def _rwkv_features(ub, mu_p, seg, w2_p, a2_p, g2_p, vecs):
    B, T, W = ub.shape
    tm = min(T, 256)
    nblk = T // tm
    hb = tm // SUBLANES
    nh = T // SUBLANES
    full = lambda a: pl.BlockSpec(a.shape, lambda b, i: (0,) * a.ndim)
    out = pl.BlockSpec((None, tm, GROUP_W), lambda b, i: (b, i, 0))
    return pl.pallas_call(
        functools.partial(_rwkv_feat_kernel, nblk=nblk),
        grid=(B, nblk),
        in_specs=[
            pl.BlockSpec((None, tm, W), lambda b, i: (b, i, 0)),
            pl.BlockSpec((None, SUBLANES, W), lambda b, i: (b, jnp.maximum(i * hb - 1, 0), 0)),
            pl.BlockSpec((None, SUBLANES, W), lambda b, i: (b, jnp.minimum((i + 1) * hb, nh - 1), 0)),
            full(mu_p), full(seg), full(w2_p), full(a2_p), full(g2_p), full(vecs),
        ],
        out_specs=[out] * 11,
        out_shape=[jax.ShapeDtypeStruct((B, T, GROUP_W), F32)] * 11,
        compiler_params=_cparams(("parallel", "parallel")),
        name="rwkv_features",
    )(ub, ub, ub, mu_p, seg, w2_p, a2_p, g2_p, vecs)


def _rwkv_chunk_kernel(r_ref, v_ref, kk_ref, lwf_ref, lwb_ref, kdf_ref, kdb_ref, af_ref, ab_ref,
                       cum_ref, tri_ref, hm_ref,
                       rh_out, y1_out, mp_out, np_out, pc_out, *, nsub):
    C = CHUNK
    hm = hm_ref[...]
    eye_w = tri_ref[2]
    lw_refs, kd_refs, a_refs = (lwf_ref, lwb_ref), (kdf_ref, kdb_ref), (af_ref, ab_ref)

    def diag(t):
        return (jnp.concatenate([t, t, t, t], axis=0) * hm).astype(BF16)

    def rows2(a, b):
        return jnp.concatenate([a, b], axis=0).astype(BF16)

    chains = [(ci, d) for ci in range(nsub) for d in range(2)]
    n = len(chains)
    rng = range(n)
    rows = [slice(ci * C, (ci + 1) * C) for ci, _ in chains]

    at, rt, vv, bh, kh, b_d, k_d, e_end = ([None] * n for _ in range(8))
    for i, (ci, d) in enumerate(chains):
        rs = rows[i]
        r, kk = r_ref[rs, :], kk_ref[rs, :]
        lw, kd, a = lw_refs[d][rs, :], kd_refs[d][rs, :], a_refs[d][rs, :]
        cl = _dot_exact_lhs(cum_ref[d], lw)
        last = C - 1 if d == 0 else 0
        e_end[i] = jnp.exp(cl[last:last + 1, :])
        e_neg = jnp.exp(-cl)
        bt = kk * a * e_neg
        kt = kd * e_neg
        vv[i] = v_ref[rs, :]
        at[i] = -kk * jnp.exp(cl - lw)
        rt[i] = r * jnp.exp(cl)
        bh[i], kh[i] = bt * e_end[i], kt * e_end[i]
        b_d[i], k_d[i] = diag(bt), diag(kt)
    a_d = [diag(at[i]) for i in rng]
    v_d = [diag(vv[i]) for i in rng]

    strict = [tri_ref[d] for _, d in chains]
    ar = [rows2(at[i], rt[i]) for i in rng]
    gb = [_dot_nt(ar[i], b_d[i]) for i in rng]
    gk = [_dot_nt(ar[i], k_d[i]) for i in rng]
    l_ab = [gb[i][:C] * strict[i] for i in rng]
    w_rb = [(gb[i][C:] * (strict[i] + eye_w)).astype(BF16) for i in rng]
    w_akrk = [rows2(gk[i][:C] * strict[i], gk[i][C:] * (strict[i] + eye_w)) for i in rng]

    pw = [_dot(l_ab[i].astype(BF16), diag(l_ab[i])) for i in rng]
    tm = [eye_w + l_ab[i] for i in rng]
    for _ in range(4):
        sq = [_dot(rows2(pw[i], tm[i]), diag(pw[i])) for i in rng]
        tm = [tm[i] + sq[i][C:] for i in rng]
        pw = [sq[i][:C] for i in rng]
    tm = [(tm[i] + _dot(tm[i].astype(BF16), diag(pw[i]))).astype(BF16) for i in rng]

    wt = [_dot(tm[i], a_d[i]) for i in rng]
    xy = [_dot(w_akrk[i], v_d[i]) for i in rng]
    u0 = [_dot(tm[i], diag(xy[i][:C])) for i in rng]
    rh = [rt[i] + _dot(w_rb[i], diag(wt[i])) for i in rng]
    y1 = [_dot(w_rb[i], diag(u0[i])) + xy[i][C:] for i in rng]
    for i, (ci, d) in enumerate(chains):
        rh_out[d, rows[i], :] = rh[i].astype(BF16)
        y1_out[d, rows[i], :] = y1[i]
        mp_out[d, ci] = (_dot_tn(wt[i].astype(BF16), bh[i].astype(BF16)) * hm).astype(BF16)
        np_bd = _dot_tn(rows2(u0[i], vv[i]), rows2(bh[i], kh[i])) * hm
        np_out[d, ci] = functools.reduce(
            jnp.add, [np_bd[h * HEAD_DIM:(h + 1) * HEAD_DIM] for h in range(4)])
        pc_out[d, ci] = jnp.broadcast_to(e_end[i], (SUBLANES, GROUP_W))


def _rwkv_chunks(feats, cum, tri, hm):
    r = feats[0]
    B, T, G = r.shape
    nch = T // CHUNK
    nsub = 4 if nch % 4 == 0 else 1
    tb = nsub * CHUNK
    tok = pl.BlockSpec((None, tb, G), lambda b, c: (b, c, 0))
    full = lambda a: pl.BlockSpec(a.shape, lambda b, c: (0,) * a.ndim)
    return pl.pallas_call(
        functools.partial(_rwkv_chunk_kernel, nsub=nsub),
        grid=(B, nch // nsub),
        in_specs=[tok] * 9 + [full(cum), full(tri), full(hm)],
        out_specs=[
            pl.BlockSpec((None, 2, tb, G), lambda b, c: (b, 0, c, 0)),
            pl.BlockSpec((None, 2, tb, G), lambda b, c: (b, 0, c, 0)),
            pl.BlockSpec((None, 2, nsub, G, G), lambda b, c: (b, 0, c, 0, 0)),
            pl.BlockSpec((None, 2, nsub, HEAD_DIM, G), lambda b, c: (b, 0, c, 0, 0)),
            pl.BlockSpec((None, 2, nsub, SUBLANES, G), lambda b, c: (b, 0, c, 0, 0)),
        ],
        out_shape=[
            jax.ShapeDtypeStruct((B, 2, T, G), BF16),
            jax.ShapeDtypeStruct((B, 2, T, G), F32),
            jax.ShapeDtypeStruct((B, 2, nch, G, G), BF16),
            jax.ShapeDtypeStruct((B, 2, nch, HEAD_DIM, G), F32),
            jax.ShapeDtypeStruct((B, 2, nch, SUBLANES, G), F32),
        ],
        compiler_params=_cparams(("parallel", "parallel")),
        name="rwkv_chunk_transitions",
    )(*feats, cum, tri, hm)


def _rwkv_scan_kernel(s0_ref, hm_ref, rhf_ref, y1f_ref, mpf_ref, npf_ref, pcf_ref,
                      rhb_ref, y1b_ref, mpb_ref, npb_ref, pcb_ref,
                      yf_out, yb_out, s_out, *, batch, nsub):
    c = pl.program_id(0)

    @pl.when(c == 0)
    def _():
        s_out[...] = s0_ref[...]

    hm = hm_ref[...]
    dirs = ((rhf_ref, y1f_ref, mpf_ref, npf_ref, pcf_ref, yf_out),
            (rhb_ref, y1b_ref, mpb_ref, npb_ref, pcb_ref, yb_out))
    chains = [(b, d) for b in range(batch) for d in range(2)]
    s = [s_out[b, d] for b, d in chains]
    for step in range(nsub):
        s_d = [(jnp.concatenate([t, t, t, t], axis=0) * hm).astype(BF16) for t in s]
        s_b = [t.astype(BF16) for t in s]
        for i, (b, d) in enumerate(chains):
            rh_ref, y1_ref, mp_ref, np_ref, pc_ref, y_out = dirs[d]
            ci = step if d == 0 else nsub - 1 - step
            rows = slice(ci * CHUNK, (ci + 1) * CHUNK)
            y_out[b, rows, :] = _dot_nt(rh_ref[b, rows, :], s_d[i]) + y1_ref[b, rows, :]
            s[i] = s[i] * pc_ref[b, ci, 0:1, :] + _dot(s_b[i], mp_ref[b, ci]) + np_ref[b, ci]
    for i, (b, d) in enumerate(chains):
        s_out[b, d] = s[i]


def _rwkv_scan(s0, hm, rh, y1, mp, npm, pc):
    B, _, T, G = rh.shape
    nch = T // CHUNK
    nsub = 4 if nch % 4 == 0 else 1
    nblk = nch // nsub
    tb = nsub * CHUNK
    fwd = lambda c: c
    bwd = lambda c: nblk - 1 - c

    def specs(order, d):
        return [
            pl.BlockSpec((B, None, tb, G), lambda c: (0, d, order(c), 0)),
            pl.BlockSpec((B, None, tb, G), lambda c: (0, d, order(c), 0)),
            pl.BlockSpec((B, None, nsub, G, G), lambda c: (0, d, order(c), 0, 0)),
            pl.BlockSpec((B, None, nsub, HEAD_DIM, G), lambda c: (0, d, order(c), 0, 0)),
            pl.BlockSpec((B, None, nsub, SUBLANES, G), lambda c: (0, d, order(c), 0, 0)),
        ]

    return pl.pallas_call(
        functools.partial(_rwkv_scan_kernel, batch=B, nsub=nsub),
        grid=(nblk,),
        in_specs=[pl.BlockSpec(s0.shape, lambda c: (0, 0, 0, 0)), pl.BlockSpec(hm.shape, lambda c: (0, 0))]
        + specs(fwd, 0) + specs(bwd, 1),
        out_specs=[
            pl.BlockSpec((B, tb, G), lambda c: (0, fwd(c), 0)),
            pl.BlockSpec((B, tb, G), lambda c: (0, bwd(c), 0)),
            pl.BlockSpec(s0.shape, lambda c: (0, 0, 0, 0)),
        ],
        out_shape=[
            jax.ShapeDtypeStruct((B, T, G), F32),
            jax.ShapeDtypeStruct((B, T, G), F32),
            jax.ShapeDtypeStruct(s0.shape, F32),
        ],
        compiler_params=_cparams(("arbitrary",)),
        name="rwkv_state_scan",
    )(s0, hm, rh, y1, mp, npm, pc, rh, y1, mp, npm, pc)


def _seg_matrix(width):
    i = np.arange(width) // HEAD_DIM
    return jnp.asarray((i[:, None] == i[None, :]).astype(np.float32), BF16)


def _scan_constants():
    C = CHUNK
    t = np.arange(4 * C)
    head, tok = t // C, t % C
    i = np.arange(C)
    lower = i[:, None] > tok[None, :]
    upper = i[:, None] < tok[None, :]
    eye = i[:, None] == tok[None, :]
    tri = jnp.asarray(np.stack([lower, upper, eye]).astype(np.float32))
    cum = jnp.asarray(np.stack([i[:, None] >= i[None, :], i[:, None] <= i[None, :]]).astype(np.float32), BF16)
    hm = jnp.asarray((head[:, None] == (np.arange(GROUP_W) // HEAD_DIM)[None, :]).astype(np.float32))
    return cum, tri, hm


def _pad_vec(v, lo, total):
    return jnp.zeros((total,), F32).at[lo:lo + v.shape[0]].set(v)


def kernel(x, c, ctx, c_ctx, w_mod, b_mod, norm_ffn1, ffn1_w_gu, ffn1_w_down, norm_mix, w_in, w_out, mla_q_norm, mla_kv_norm, mla_w_uq, mla_w_ukv, mla_qn, mla_kn, rwkv_shift, rwkv_w0, rwkv_w2, rwkv_a0, rwkv_a2, rwkv_g2, rwkv_k_k, rwkv_k_a, rwkv_r_k, rwkv_lnx_g, rwkv_lnx_b, na_qn, na_kn, na_rpb, gqa_qn, gqa_kn, norm_ffn2, ffn2_w_gu, ffn2_w_down):
    B, S, D = x.shape
    Lc = ctx.shape[1]
    depth = w_mod.shape[0]
    assert B + 1 <= SUBLANES and S % 512 == 0 and Lc % CHUNK == 0 and S % GRID_W == 0

    cvec = jnp.zeros((SUBLANES, D), F32).at[:B].set(c).at[B].set(c_ctx)
    mods = _modulation(cvec, w_mod, b_mod).reshape(depth, SUBLANES, 9, D)

    t = jnp.arange(S)
    pos = (t // GRID_W, t % GRID_W)
    mla_tab = _rope_tables(pos, LANES, MLA_NOPE, MLA_ROPE // 2, 1)
    gqa_tab = _rope_tables(pos, 4 * HEAD_DIM, 0, HEAD_DIM // 2, 4)
    mla_tab_c = _identity_rope_tables(Lc, LANES)
    gqa_tab_c = _identity_rope_tables(Lc, 4 * HEAD_DIM)

    seg = _seg_matrix(GROUP_W)
    cum, tri, hm = _scan_constants()
    mla_scale = MLA_QK ** -0.5 * LOG2E
    dh_scale = HEAD_DIM ** -0.5 * LOG2E

    w_gu1, w_dn1 = ffn1_w_gu.astype(BF16), ffn1_w_down.astype(BF16)
    w_gu2, w_dn2 = ffn2_w_gu.astype(BF16), ffn2_w_down.astype(BF16)
    w_out_b = w_out.astype(BF16)
    zc = lambda n: jnp.zeros((depth, D, n), BF16)
    wi = w_in.astype(BF16)
    o_b = MLA_COLS
    o_n = o_b + RWKV_COLS
    o_d = o_n + NA_COLS
    w_in_p = jnp.concatenate([
        wi[..., :MLA_Q_RANK + MLA_KV_RANK], zc(64), wi[..., MLA_Q_RANK + MLA_KV_RANK:MLA_COLS], zc(32),
        wi[..., o_b:o_n], zc(UB_W - RWKV_COLS),
        wi[..., o_n:o_d],
        wi[..., o_d:o_d + GQA_COLS]], axis=2)

    h, hc = x, ctx
    for l in range(depth):
        ctx_out = l < depth - 1
        m_lat = mods[l, :B]
        m_ctx = jnp.broadcast_to(mods[l, B], (B, 9, D))


        wuq = mla_w_uq[l].reshape(MLA_Q_RANK, MLA_HEADS, MLA_QK)
        wuq_p = jnp.pad(wuq, ((0, 0), (0, 0), (0, LANES - MLA_QK))).reshape(MLA_Q_RANK, MLA_HEADS * LANES).astype(BF16)
        wukv = mla_w_ukv[l].reshape(MLA_KV_RANK, MLA_HEADS, MLA_NOPE + MLA_V)
        wuk_p = jnp.pad(wukv[..., :MLA_NOPE], ((0, 0), (0, 0), (0, LANES - MLA_NOPE))).reshape(MLA_KV_RANK, MLA_HEADS * LANES).astype(BF16)
        wuv = wukv[..., MLA_NOPE:].reshape(MLA_KV_RANK, MLA_HEADS * MLA_V).astype(BF16)
        mla_small = (mla_q_norm[l].reshape(1, -1), mla_kv_norm[l].reshape(1, -1),
                     _pad_vec(mla_qn[l] * mla_scale, 0, LANES).reshape(1, LANES),
                     _pad_vec(mla_kn[l], 0, LANES).reshape(1, LANES), wuq_p, wuk_p, wuv)

        na_gq = jnp.tile(na_qn[l] * dh_scale, 4).reshape(1, -1)
        na_gk = jnp.tile(na_kn[l], 4).reshape(1, -1)
        gqa_gq = jnp.tile(gqa_qn[l] * dh_scale, 4).reshape(1, -1)
        gqa_gk = jnp.tile(gqa_kn[l], 2).reshape(1, -1)
        na_tb = _na_bias_tables(na_rpb[l])

        mu_p = jnp.zeros((SUBLANES, UB_W), F32).at[:2, :RWKV_COLS].set(rwkv_shift[l])
        lo = 3 * GROUP_W

        def lora_w(w, off):
            return jnp.zeros((GROUP_W, GROUP_W), F32).at[off:off + w.shape[0]].set(w).astype(BF16)

        w2_p = jnp.stack([lora_w(rwkv_w2[l, 0], 0), lora_w(rwkv_w2[l, 1], 32)])
        a2_p = jnp.stack([lora_w(rwkv_a2[l, 0], 64), lora_w(rwkv_a2[l, 1], 96)])
        g2_p = lora_w(rwkv_g2[l], 128)
        feat_vecs = jnp.zeros((SUBLANES, GROUP_W), F32).at[0].set(rwkv_k_k[l]).at[1].set(rwkv_k_a[l]) \
            .at[2].set(rwkv_r_k[l].reshape(-1)).at[3:5].set(rwkv_w0[l]).at[5:7].set(rwkv_a0[l])
        post_vecs = jnp.zeros((SUBLANES, GROUP_W), F32).at[0].set(rwkv_lnx_g[l]).at[1].set(rwkv_lnx_b[l])

        h = _ffn(h, m_lat, norm_ffn1[l], w_gu1, w_dn1, l, 0)
        hc = _ffn(hc, m_ctx, norm_ffn1[l], w_gu1, w_dn1, l, 0)

        ua, ub, un, ud = _inproj(h, m_lat, norm_mix[l], w_in_p, l)
        uca, ucb, ucn, ucd = _inproj(hc, m_ctx, norm_mix[l], w_in_p, l)

        qa, ka, vta = _mla_prep(ua, mla_tab, *mla_small)
        qca, kca, vtca = _mla_prep(uca, mla_tab_c, *mla_small)
        oa = _attention(qa, ka, vta, kca, vtca)

        qn_, kn_, vn_ = _qkv_prep(un, seg, na_gq, na_gk, 4, 4, 4)
        qcn, kcn, vcn, vtcn = _qkv_prep(ucn, seg, na_gq, na_gk, 4, 4, 4, v_cols=True)
        on = _na_attention(qn_, kn_, vn_, kcn, vcn, na_tb)

        qd, kd, vtd = _qkv_prep(ud, seg, gqa_gq, gqa_gk, 4, 2, 2, gqa_tab, v_rows=False, v_cols=True)
        qcd, kcd, vtcd = _qkv_prep(ucd, seg, gqa_gq, gqa_gk, 4, 2, 2, gqa_tab_c, v_rows=False, v_cols=True)
        od = _attention(qd, kd, vtd, kcd, vtcd)

        feats_c = _rwkv_features(ucb, mu_p, seg, w2_p, a2_p, g2_p, feat_vecs)
        feats = _rwkv_features(ub, mu_p, seg, w2_p, a2_p, g2_p, feat_vecs)
        trans_c = _rwkv_chunks(feats_c[:9], cum, tri, hm)
        trans = _rwkv_chunks(feats[:9], cum, tri, hm)
        s0 = jnp.zeros((B, 2, HEAD_DIM, GROUP_W), F32)
        yfc, ybc, s_ctx = _rwkv_scan(s0, hm, *trans_c)
        yf, yb, _ = _rwkv_scan(s_ctx, hm, *trans)

        h = _outproj(h, m_lat, oa, (yf, yb, feats[9], feats[10]), on, od, (True, False, True),
                     seg, post_vecs, w_out_b, l)
        if ctx_out:
            oca = _attention(qca, kca, vtca)
            ocn = _attention(qcn, kcn, vtcn)
            ocd = _attention(qcd, kcd, vtcd)
            hc = _outproj(hc, m_ctx, oca, (yfc, ybc, feats_c[9], feats_c[10]), ocn, ocd,
                          (True, True, True), seg, post_vecs, w_out_b, l)

        h = _ffn(h, m_lat, norm_ffn2[l], w_gu2, w_dn2, l, 6)
        if ctx_out:
            hc = _ffn(hc, m_ctx, norm_ffn2[l], w_gu2, w_dn2, l, 6)
    return h
```

```python
import functools

import numpy as np
import jax
import jax.numpy as jnp
from jax import lax
from jax.experimental import pallas as pl
from jax.experimental.pallas import tpu as pltpu

F32 = jnp.float32
BF16 = jnp.bfloat16

GRID_W = 64
HEAD_DIM = 64
GROUP_W = 256
ROPE_THETA = 10000.0
NORM_EPS = 1e-6
D_FF = 2816
MLA_HEADS = 4
MLA_Q_RANK = 256
MLA_KV_RANK = 128
MLA_NOPE = 64
MLA_ROPE = 32
MLA_V = 64
MLA_QK = MLA_NOPE + MLA_ROPE
RWKV_COLS = 960
RWKV_LNX_EPS = 64e-5
NA_WIN_R = 8
NA_WIN_C = 16
MLA_COLS = 416
NA_COLS = 768
GQA_COLS = 512

LANES = 128
SUBLANES = 8
VMEM_LIMIT_BYTES = 56 * 1024 * 1024

UA_W = 512
UB_W = 1024
UN_W = 768
UD_W = 512

VT_ROWS = 128
CHUNK = 64
NEG_INF = -1e30
LOG2E = 1.4426950408889634
NA_ROWS_PER_ITER = 8
MAX_JUMP = 24.0


def _dot(a, b):
    return jnp.dot(a, b, preferred_element_type=F32)


def _dot_nt(a, b):
    return lax.dot_general(a, b, (((1,), (1,)), ((), ())), preferred_element_type=F32)


def _dot_tn(a, b):
    return lax.dot_general(a, b, (((0,), (0,)), ((), ())), preferred_element_type=F32)


def _split2(x):
    hi = x.astype(BF16)
    lo = (x - hi.astype(F32)).astype(BF16)
    return hi, lo


def _dot_exact_rhs(x, m):
    h1 = x.astype(BF16)
    r1 = x - h1.astype(F32)
    h2 = r1.astype(BF16)
    h3 = (r1 - h2.astype(F32)).astype(BF16)
    return _dot(h1, m) + _dot(h2, m) + _dot(h3, m)


def _dot_exact_lhs(m, x):
    h1 = x.astype(BF16)
    r1 = x - h1.astype(F32)
    h2 = r1.astype(BF16)
    h3 = (r1 - h2.astype(F32)).astype(BF16)
    return _dot(m, h1) + _dot(m, h2) + _dot(m, h3)


def _rms(x, eps=NORM_EPS):
    return x * lax.rsqrt(jnp.mean(x * x, axis=-1, keepdims=True) + eps)


def _cparams(sem):
    return pltpu.CompilerParams(dimension_semantics=sem, vmem_limit_bytes=VMEM_LIMIT_BYTES)


def _mod_kernel(c_ref, w_ref, b_ref, o_ref):
    c = c_ref[...]
    s = c * jax.nn.sigmoid(c)
    o_ref[...] = jnp.dot(s, w_ref[...], preferred_element_type=F32,
                         precision=lax.Precision.HIGHEST) + b_ref[...]


def _modulation(cvec, w_mod, b_mod):
    L, D, N = w_mod.shape
    tn = 1024
    return pl.pallas_call(
        _mod_kernel,
        grid=(L, N // tn),
        in_specs=[
            pl.BlockSpec((SUBLANES, D), lambda l, n: (0, 0)),
            pl.BlockSpec((None, D, tn), lambda l, n: (l, 0, n)),
            pl.BlockSpec((None, 1, tn), lambda l, n: (l, 0, n)),
        ],
        out_specs=pl.BlockSpec((None, SUBLANES, tn), lambda l, n: (l, 0, n)),
        out_shape=jax.ShapeDtypeStruct((L, SUBLANES, N), F32),
        compiler_params=_cparams(("parallel", "parallel")),
        name="adaln_mod",
    )(cvec, w_mod, b_mod.reshape(L, 1, N))


def _ffn_kernel(h_ref, mod_ref, g_ref, wg_ref, wu_ref, wd_ref, o_ref, *, base):
    x = h_ref[...]
    y = _rms(x) * g_ref[...]
    xn = (y * (1.0 + mod_ref[base + 1:base + 2, :]) + mod_ref[base:base + 1, :]).astype(BF16)
    g = _dot(xn, wg_ref[...])
    u = _dot(xn, wu_ref[...])
    a = (g * jax.nn.sigmoid(g) * u).astype(BF16)
    o_ref[...] = x + 0.5 * mod_ref[base + 2:base + 3, :] * _dot(a, wd_ref[...])


def _ffn(h, mod, gain, w_gu, w_down, layer, base):
    B, T, D = h.shape
    F = w_down.shape[1]
    tm = min(T, 512)
    resident = pl.Buffered(1)
    return pl.pallas_call(
        functools.partial(_ffn_kernel, base=base),
        grid=(B, T // tm),
        in_specs=[
            pl.BlockSpec((None, tm, D), lambda b, i: (b, i, 0)),
            pl.BlockSpec((None, 9, D), lambda b, i: (b, 0, 0)),
            pl.BlockSpec((1, D), lambda b, i: (0, 0)),
            pl.BlockSpec((None, D, F), lambda b, i: (layer, 0, 0), pipeline_mode=resident),
            pl.BlockSpec((None, D, F), lambda b, i: (layer, 0, 1), pipeline_mode=resident),
            pl.BlockSpec((None, F, D), lambda b, i: (layer, 0, 0), pipeline_mode=resident),
        ],
        out_specs=pl.BlockSpec((None, tm, D), lambda b, i: (b, i, 0)),
        out_shape=jax.ShapeDtypeStruct((B, T, D), F32),
        compiler_params=_cparams(("parallel", "parallel")),
        name="swiglu_halfstep",
    )(h, mod, gain.reshape(1, D), w_gu, w_gu, w_down)


def _inproj_kernel(h_ref, mod_ref, g_ref, w_ref, oa_ref, ob_ref, on_ref, od_ref):
    y = _rms(h_ref[...]) * g_ref[...]
    xm = (y * (1.0 + mod_ref[4:5, :]) + mod_ref[3:4, :]).astype(BF16)
    o = 0
    for ref, w in ((oa_ref, UA_W), (ob_ref, UB_W), (on_ref, UN_W), (od_ref, UD_W)):
        ref[...] = _dot(xm, w_ref[:, o:o + w])
        o += w


def _inproj(h, mod, gain, w_in_p, layer):
    B, T, D = h.shape
    tm = min(T, 512)
    W = w_in_p.shape[2]
    widths = (UA_W, UB_W, UN_W, UD_W)
    return pl.pallas_call(
        _inproj_kernel,
        grid=(B, T // tm),
        in_specs=[
            pl.BlockSpec((None, tm, D), lambda b, i: (b, i, 0)),
            pl.BlockSpec((None, 9, D), lambda b, i: (b, 0, 0)),
            pl.BlockSpec((1, D), lambda b, i: (0, 0)),
            pl.BlockSpec((None, D, W), lambda b, i: (layer, 0, 0)),
        ],
        out_specs=[pl.BlockSpec((None, tm, w), lambda b, i: (b, i, 0)) for w in widths],
        out_shape=[jax.ShapeDtypeStruct((B, T, w), F32) for w in widths],
        compiler_params=_cparams(("parallel", "parallel")),
        name="in_projection",
    )(h, mod, gain.reshape(1, D), w_in_p)


def _rwkv_out(yf_ref, yb_ref, bon_ref, g_ref, seg_ref, vec_ref):
    y = yf_ref[...] + yb_ref[...]
    seg = seg_ref[...]

    def segmean(t):
        hi, lo = _split2(t)
        return (_dot(hi, seg) + _dot(lo, seg)) * (1.0 / HEAD_DIM)

    mu = segmean(y)
    yc = y - mu
    var = segmean(yc * yc)
    yn = yc * lax.rsqrt(var + RWKV_LNX_EPS) * vec_ref[0:1, :] + vec_ref[1:2, :]
    return ((yn + bon_ref[...]) * g_ref[...]).astype(BF16)


def _outproj_kernel(h_ref, mod_ref, oa_ref, yf_ref, yb_ref, bon_ref, g_ref, on_ref, od_ref,
                    seg_ref, vec_ref, w_ref, o_ref, *, transposed):
    ob = _rwkv_out(yf_ref, yb_ref, bon_ref, g_ref, seg_ref, vec_ref)
    groups = ((oa_ref, transposed[0]), (None, False), (on_ref, transposed[1]), (od_ref, transposed[2]))
    acc = None
    for g, (ref, t) in enumerate(groups):
        w = w_ref[g * GROUP_W:(g + 1) * GROUP_W, :]
        x = ob if ref is None else ref[...]
        part = _dot_tn(x, w) if t else _dot(x, w)
        acc = part if acc is None else acc + part
    o_ref[...] = h_ref[...] + mod_ref[5:6, :] * acc


def _outproj(h, mod, oa, rwkv, on, od, transposed, seg, post_vecs, w_out, layer):
    B, T, D = h.shape
    tm = min(T, 512)
    grp = pl.BlockSpec((None, tm, GROUP_W), lambda b, i: (b, i, 0))
    grp_t = pl.BlockSpec((None, GROUP_W, tm), lambda b, i: (b, 0, i))
    pick = lambda t: grp_t if t else grp
    full = lambda a: pl.BlockSpec(a.shape, lambda b, i: (0,) * a.ndim)
    return pl.pallas_call(
        functools.partial(_outproj_kernel, transposed=tuple(transposed)),
        grid=(B, T // tm),
        in_specs=[
            pl.BlockSpec((None, tm, D), lambda b, i: (b, i, 0)),
            pl.BlockSpec((None, 9, D), lambda b, i: (b, 0, 0)),
            pick(transposed[0]), grp, grp, grp, grp, pick(transposed[1]), pick(transposed[2]),
            full(seg), full(post_vecs),
            pl.BlockSpec((None, 4 * GROUP_W, D), lambda b, i: (layer, 0, 0)),
        ],
        out_specs=pl.BlockSpec((None, tm, D), lambda b, i: (b, i, 0)),
        out_shape=jax.ShapeDtypeStruct((B, T, D), F32),
        compiler_params=_cparams(("parallel", "parallel")),
        name="out_projection",
    )(h, mod, oa, *rwkv, on, od, seg, post_vecs, w_out)


def _rope_tables(n_rows, width, lead, group, n_rep):
    half = group // 2
    lane = np.arange(width)
    rel = (lane - lead) % (2 * group)
    inside = (lane >= lead) & (lane < lead + n_rep * 2 * group)
    is_row = (rel < group)[None, None, :]
    second = ((rel % group >= half) & inside)[None, None, :]
    first = ((rel % group < half) & inside)[None, None, :]
    inv_freq = ROPE_THETA ** (-jnp.arange(half, dtype=F32) / half)
    freq = jnp.where(inside, inv_freq[rel % half], 0.0)[None, :]
    ang_r = jnp.arange(n_rows).astype(F32)[:, None] * freq
    ang_c = jnp.arange(GRID_W).astype(F32)[:, None] * freq
    shape = (n_rows, GRID_W, width)
    c = jnp.where(is_row, jnp.cos(ang_r)[:, None, :], jnp.cos(ang_c)[None, :, :])
    s = jnp.where(is_row, jnp.sin(ang_r)[:, None, :], jnp.sin(ang_c)[None, :, :])
    flat = lambda t: jnp.broadcast_to(t, shape).reshape(n_rows * GRID_W, width)
    return [flat(c), flat(jnp.where(second, s, 0.0)), flat(jnp.where(first, -s, 0.0))]


def _identity_rope_tables(n_tokens, width):
    return [jnp.ones((n_tokens, width), F32), jnp.zeros((n_tokens, width), F32),
            jnp.zeros((n_tokens, width), F32)]


def _vt_ext(v):
    tm = v.shape[0]
    lane = lax.broadcasted_iota(jnp.int32, (tm, VT_ROWS - HEAD_DIM), 1)
    aux = jnp.where(lane == 0, 1.0, 0.0)
    return jnp.concatenate([v, aux], axis=-1).T.astype(BF16)


def _mla_prep_kernel(ua_ref, c_ref, s1_ref, s2_ref, qnorm_ref, kvnorm_ref, qn_ref, kn_ref,
                     wuq_ref, wuk_ref, wuv_ref, q_out, k_out, vt_out):
    ua = ua_ref[...]
    qc = (_rms(ua[:, :MLA_Q_RANK]) * qnorm_ref[...]).astype(BF16)
    kvc = (_rms(ua[:, MLA_Q_RANK:MLA_Q_RANK + MLA_KV_RANK]) * kvnorm_ref[...]).astype(BF16)
    k_rope = ua[:, 3 * LANES:4 * LANES]
    q_all = _dot(qc, wuq_ref[...])
    k_all = _dot(kvc, wuk_ref[...])
    v_all = _dot(kvc, wuv_ref[...])
    cos, s_dn, s_up = c_ref[...], s1_ref[...], s2_ref[...]
    half = MLA_ROPE // 4

    def rope(x):
        return x * cos + pltpu.roll(x, half, 1) * s_dn + pltpu.roll(x, LANES - half, 1) * s_up

    def headnorm(x, gain):
        ms = jnp.sum(x * x, axis=-1, keepdims=True) * (1.0 / MLA_QK)
        return x * lax.rsqrt(ms + NORM_EPS) * gain

    for h in range(MLA_HEADS):
        qh = headnorm(q_all[:, h * LANES:(h + 1) * LANES], qn_ref[...])
        q_out[h] = rope(qh).astype(BF16)
        kh = headnorm(k_all[:, h * LANES:(h + 1) * LANES] + k_rope, kn_ref[...])
        k_out[h] = rope(kh).astype(BF16)
        vt_out[h] = _vt_ext(v_all[:, h * MLA_V:(h + 1) * MLA_V])


def _mla_prep(ua, tables, q_norm, kv_norm, qn_p, kn_p, wuq_p, wuk_p, wuv):
    B, T, _ = ua.shape
    tm = min(T, 512)
    tab = pl.BlockSpec((tm, LANES), lambda b, i: (i, 0))
    full = lambda a: pl.BlockSpec(a.shape, lambda b, i: (0,) * a.ndim)
    small = [q_norm, kv_norm, qn_p, kn_p, wuq_p, wuk_p, wuv]
    return pl.pallas_call(
        _mla_prep_kernel,
        grid=(B, T // tm),
        in_specs=[pl.BlockSpec((None, tm, UA_W), lambda b, i: (b, i, 0)), tab, tab, tab]
        + [full(a) for a in small],
        out_specs=[
            pl.BlockSpec((None, MLA_HEADS, tm, LANES), lambda b, i: (b, 0, i, 0)),
            pl.BlockSpec((None, MLA_HEADS, tm, LANES), lambda b, i: (b, 0, i, 0)),
            pl.BlockSpec((None, MLA_HEADS, VT_ROWS, tm), lambda b, i: (b, 0, 0, i)),
        ],
        out_shape=[
            jax.ShapeDtypeStruct((B, MLA_HEADS, T, LANES), BF16),
            jax.ShapeDtypeStruct((B, MLA_HEADS, T, LANES), BF16),
            jax.ShapeDtypeStruct((B, MLA_HEADS, VT_ROWS, T), BF16),
        ],
        compiler_params=_cparams(("parallel", "parallel")),
        name="mla_prep",
    )(ua, *tables, *small)


def _qkv_prep_kernel(*refs, q_off, nq, nk, nv, use_rope, v_rows, v_cols):
    n_in = 7 if use_rope else 4
    x_ref, seg_ref, gq_ref, gk_ref = refs[:4]
    if use_rope:
        c_ref, s1_ref, s2_ref = refs[4:7]
    q_out, k_out = refs[n_in:n_in + 2]
    v_outs = list(refs[n_in + 2:])
    v_out = v_outs.pop(0) if v_rows else None
    vt_out = v_outs.pop(0) if v_cols else None
    x = x_ref[...]
    wq, wk, wv = nq * HEAD_DIM, nk * HEAD_DIM, nv * HEAD_DIM
    q = x[:, q_off:q_off + wq]
    k = x[:, q_off + wq:q_off + wq + wk]
    v = x[:, q_off + wq + wk:q_off + wq + wk + wv]

    def headnorm(t, gain, w):
        hi, lo = _split2(t * t)
        seg = seg_ref[0:w, 0:w]
        ms = (_dot(hi, seg) + _dot(lo, seg)) * (1.0 / HEAD_DIM)
        return t * lax.rsqrt(ms + NORM_EPS) * gain

    q = headnorm(q, gq_ref[...], wq)
    k = headnorm(k, gk_ref[...], wk)
    if use_rope:
        half = HEAD_DIM // 4

        def rope(t, w):
            return (t * c_ref[:, 0:w] + pltpu.roll(t, half, 1) * s1_ref[:, 0:w]
                    + pltpu.roll(t, w - half, 1) * s2_ref[:, 0:w])

        q = rope(q, wq)
        k = rope(k, wk)
    for h in range(nq):
        q_out[h] = q[:, h * HEAD_DIM:(h + 1) * HEAD_DIM].astype(BF16)
    for h in range(nk):
        k_out[h] = k[:, h * HEAD_DIM:(h + 1) * HEAD_DIM].astype(BF16)
    for h in range(nv):
        vh = v[:, h * HEAD_DIM:(h + 1) * HEAD_DIM]
        if v_rows:
            v_out[h] = vh.astype(BF16)
        if v_cols:
            vt_out[h] = _vt_ext(vh)


def _qkv_prep(x, seg, gq, gk, nq, nk, nv, tables=None, v_rows=True, v_cols=False):
    B, T, W = x.shape
    tm = min(T, 512)
    use_rope = tables is not None
    full = lambda a: pl.BlockSpec(a.shape, lambda b, i: (0,) * a.ndim)
    in_specs = [pl.BlockSpec((None, tm, W), lambda b, i: (b, i, 0)), full(seg), full(gq), full(gk)]
    args = [x, seg, gq, gk]
    if use_rope:
        in_specs += [pl.BlockSpec((tm, tables[0].shape[1]), lambda b, i: (i, 0))] * 3
        args += list(tables)
    hm = lambda n: pl.BlockSpec((None, n, tm, HEAD_DIM), lambda b, i: (b, 0, i, 0))
    out_specs = [hm(nq), hm(nk)]
    out_shape = [jax.ShapeDtypeStruct((B, n, T, HEAD_DIM), BF16) for n in (nq, nk)]
    if v_rows:
        out_specs.append(hm(nv))
        out_shape.append(jax.ShapeDtypeStruct((B, nv, T, HEAD_DIM), BF16))
    if v_cols:
        out_specs.append(pl.BlockSpec((None, nv, VT_ROWS, tm), lambda b, i: (b, 0, 0, i)))
        out_shape.append(jax.ShapeDtypeStruct((B, nv, VT_ROWS, T), BF16))
    return pl.pallas_call(
        functools.partial(_qkv_prep_kernel, q_off=0, nq=nq, nk=nk, nv=nv, use_rope=use_rope,
                          v_rows=v_rows, v_cols=v_cols),
        grid=(B, T // tm),
        in_specs=in_specs,
        out_specs=out_specs,
        out_shape=out_shape,
        compiler_params=_cparams(("parallel", "parallel")),
        name="qkv_prep_rope" if use_rope else "qkv_prep",
    )(*args)


def _attn_kernel(*refs, n_heads, group, has_extra, nk):
    if has_extra:
        q_ref, k_ref, vt_ref, kx_ref, vtx_ref, o_ref, m_ref, acc_ref = refs
    else:
        q_ref, k_ref, vt_ref, o_ref, m_ref, acc_ref = refs
    j = pl.program_id(2)

    heads = range(n_heads)

    def first(kr, vr):
        s = [_dot_nt(kr[h // group], q_ref[h]) for h in heads]
        m = [jnp.max(s[h], axis=0, keepdims=True) for h in heads]
        p = [jnp.exp2(s[h] - m[h]).astype(BF16) for h in heads]
        for h in heads:
            m_ref[h] = m[h]
            acc_ref[h] = _dot(vr[h // group], p[h])

    def update(kr, vr):
        m_prev = [m_ref[h] for h in heads]
        s = [_dot_nt(kr[h // group], q_ref[h]) for h in heads]
        pv = [_dot(vr[h // group], jnp.exp2(s[h] - m_prev[h]).astype(BF16)) for h in heads]
        m_blk = [jnp.max(s[h], axis=0, keepdims=True) for h in heads]
        jump = functools.reduce(jnp.maximum, [m_blk[h] - m_prev[h] for h in heads])
        safe = jnp.max(jump) <= MAX_JUMP

        @pl.when(safe)
        def _():
            for h in heads:
                m_new = jnp.maximum(m_prev[h], m_blk[h])
                m_ref[h] = m_new
                acc_ref[h] = (acc_ref[h] + pv[h]) * jnp.exp2(m_prev[h] - m_new)

        @pl.when(jnp.logical_not(safe))
        def _():
            s2 = [_dot_nt(kr[h // group], q_ref[h]) for h in heads]
            m_new = [jnp.maximum(m_ref[h], jnp.max(s2[h], axis=0, keepdims=True)) for h in heads]
            p2 = [jnp.exp2(s2[h] - m_new[h]).astype(BF16) for h in heads]
            for h in heads:
                acc_ref[h] = jnp.exp2(m_ref[h] - m_new[h]) * acc_ref[h] + _dot(vr[h // group], p2[h])
                m_ref[h] = m_new[h]

    if has_extra:
        @pl.when(j == 0)
        def _():
            first(kx_ref, vtx_ref)

        update(k_ref, vt_ref)
    else:
        @pl.when(j == 0)
        def _():
            first(k_ref, vt_ref)

        @pl.when(j > 0)
        def _():
            update(k_ref, vt_ref)

    @pl.when(j == nk - 1)
    def _():
        outs = []
        for h in range(n_heads):
            a = acc_ref[h]
            outs.append(a[:HEAD_DIM] / a[HEAD_DIM:HEAD_DIM + 1])
        o_ref[...] = jnp.concatenate(outs, axis=0).astype(o_ref.dtype)


def _attention(q, k, vt, kx=None, vtx=None):
    B, H, Tq, dq = q.shape
    Hk, Tk = k.shape[1], k.shape[2]
    group = H // Hk
    tq = min(Tq, 512)
    tk = min(Tk, 2048)
    nk = Tk // tk
    has_extra = kx is not None
    in_specs = [
        pl.BlockSpec((None, H, tq, dq), lambda b, i, j: (b, 0, i, 0)),
        pl.BlockSpec((None, Hk, tk, dq), lambda b, i, j: (b, 0, j, 0)),
        pl.BlockSpec((None, Hk, VT_ROWS, tk), lambda b, i, j: (b, 0, 0, j)),
    ]
    args = [q, k, vt]
    if has_extra:
        Tx = kx.shape[2]
        in_specs += [pl.BlockSpec((None, Hk, Tx, dq), lambda b, i, j: (b, 0, 0, 0)),
                     pl.BlockSpec((None, Hk, VT_ROWS, Tx), lambda b, i, j: (b, 0, 0, 0))]
        args += [kx, vtx]
    return pl.pallas_call(
        functools.partial(_attn_kernel, n_heads=H, group=group, has_extra=has_extra, nk=nk),
        grid=(B, Tq // tq, nk),
        in_specs=in_specs,
        out_specs=pl.BlockSpec((None, H * HEAD_DIM, tq), lambda b, i, j: (b, 0, i)),
        out_shape=jax.ShapeDtypeStruct((B, H * HEAD_DIM, Tq), BF16),
        scratch_shapes=[pltpu.VMEM((H, 1, tq), F32), pltpu.VMEM((H, VT_ROWS, tq), F32)],
        compiler_params=_cparams(("parallel", "parallel", "arbitrary")),
        name="flash_attention_ctx" if has_extra else "flash_attention",
    )(*args)


def _na_bias_kernel(r_ref, e_ref, m_ref, o_ref):
    o_ref[...] = _dot_exact_rhs(r_ref[...], e_ref[...]) + m_ref[...]


def _na_bias_tables(rpb):
    H, n_dr, n_dc = rpb.shape
    W = GRID_W
    qc = np.arange(W)[:, None]
    kc = np.arange(W)[None, :]
    c0 = np.clip(qc - NA_WIN_C // 2, 0, W - NA_WIN_C)
    valid = (kc >= c0) & (kc < c0 + NA_WIN_C)
    d = kc - qc + NA_WIN_C - 1
    onehot = np.zeros((LANES, W * W), np.float32)
    for a in range(W):
        for b in range(W):
            if valid[a, b]:
                onehot[d[a, b], a * W + b] = 1.0
    mask = np.where(valid, 0.0, NEG_INF).astype(np.float32).reshape(1, W * W)
    rows = H * n_dr
    rows_p = -(-rows // SUBLANES) * SUBLANES
    r2 = jnp.zeros((rows_p, LANES), F32).at[:rows, :n_dc].set(rpb.reshape(rows, n_dc) * LOG2E)
    tn = 1024
    tiles = pl.pallas_call(
        _na_bias_kernel,
        grid=(W * W // tn,),
        in_specs=[pl.BlockSpec((rows_p, LANES), lambda n: (0, 0)),
                  pl.BlockSpec((LANES, tn), lambda n: (0, n)),
                  pl.BlockSpec((1, tn), lambda n: (0, n))],
        out_specs=pl.BlockSpec((rows_p, tn), lambda n: (0, n)),
        out_shape=jax.ShapeDtypeStruct((rows_p, W * W), F32),
        compiler_params=_cparams(("parallel",)),
        name="na_bias_expand",
    )(r2, jnp.asarray(onehot, BF16), jnp.asarray(mask))
    tiles = tiles[:rows].reshape(H, n_dr, W, W)
    slabs = [tiles[:, base:base + NA_WIN_R].transpose(0, 2, 1, 3).reshape(H, W, NA_WIN_R * W)
             for base in range(NA_WIN_R)]
    return jnp.stack(slabs, 0)


def _na_kernel(q_ref, k_ref, v_ref, kc_ref, vc_ref, tb_ref, o_ref, *, rb, rows, n_heads):
    blk = pl.program_id(1)
    W = GRID_W
    win = NA_WIN_R * W

    nr = NA_ROWS_PER_ITER

    def rows_body(i, carry):
        q0 = pl.multiple_of(i * nr * W, nr * W)
        heads = range(n_heads)
        units = [(j, h) for j in range(nr) for h in heads]
        k0, base = [], []
        for j in range(nr):
            r = blk * rb + nr * i + j
            r0 = jnp.clip(r - NA_WIN_R // 2, 0, rows - NA_WIN_R)
            base.append(r0 - r + NA_WIN_R - 1)
            k0.append(pl.multiple_of(r0 * W, W))
        qs = [q_ref[h, pl.ds(q0, nr * W), :] for h in heads]
        s_cxs = [_dot_nt(qs[h], kc_ref[h]) for h in heads]
        s_nb = [_dot_nt(qs[h][j * W:(j + 1) * W], k_ref[h, pl.ds(k0[j], win), :]) + tb_ref[base[j], h]
                for j, h in units]
        s_cx = [s_cxs[h][j * W:(j + 1) * W] for j, h in units]
        n = len(units)
        m = [jnp.maximum(jnp.max(s_nb[u], axis=-1, keepdims=True),
                         jnp.max(s_cx[u], axis=-1, keepdims=True)) for u in range(n)]
        p_nb = [jnp.exp2(s_nb[u] - m[u]) for u in range(n)]
        p_cx = [jnp.exp2(s_cx[u] - m[u]) for u in range(n)]
        l = [jnp.sum(p_nb[u], axis=-1, keepdims=True) + jnp.sum(p_cx[u], axis=-1, keepdims=True)
             for u in range(n)]
        o_cxs = [_dot(jnp.concatenate([p_cx[j * n_heads + h] for j in range(nr)], axis=0).astype(BF16),
                      vc_ref[h]) for h in heads]
        o = [_dot(p_nb[u].astype(BF16), v_ref[h, pl.ds(k0[j], win), :]) + o_cxs[h][j * W:(j + 1) * W]
             for u, (j, h) in enumerate(units)]
        o_ref[pl.ds(q0, nr * W), :] = jnp.concatenate(
            [jnp.concatenate([o[j * n_heads + h] / l[j * n_heads + h] for h in heads], axis=-1)
             for j in range(nr)], axis=0).astype(o_ref.dtype)
        return carry

    lax.fori_loop(0, rb // nr, rows_body, 0)


def _na_attention(q, k, v, kc, vc, tb):
    B, H, S, dh = q.shape
    L = kc.shape[2]
    rows = S // GRID_W
    assert rows >= NA_WIN_R
    rb = 8
    return pl.pallas_call(
        functools.partial(_na_kernel, rb=rb, rows=rows, n_heads=H),
        grid=(B, rows // rb),
        in_specs=[
            pl.BlockSpec((None, H, rb * GRID_W, dh), lambda b, i: (b, 0, i, 0)),
            pl.BlockSpec((None, H, S, dh), lambda b, i: (b, 0, 0, 0)),
            pl.BlockSpec((None, H, S, dh), lambda b, i: (b, 0, 0, 0)),
            pl.BlockSpec((None, H, L, dh), lambda b, i: (b, 0, 0, 0)),
            pl.BlockSpec((None, H, L, dh), lambda b, i: (b, 0, 0, 0)),
            pl.BlockSpec(tb.shape, lambda b, i: (0, 0, 0, 0)),
        ],
        out_specs=pl.BlockSpec((None, rb * GRID_W, H * dh), lambda b, i: (b, i, 0)),
        out_shape=jax.ShapeDtypeStruct((B, S, H * dh), BF16),
        compiler_params=_cparams(("parallel", "arbitrary")),
        name="neighbourhood_attention",
    )(q, k, v, kc, vc, tb)


def _rwkv_feat_kernel(x_ref, xp_ref, xn_ref, mu_ref, seg_ref, w2_ref, a2_ref, g2_ref, vec_ref,
                      r_out, v_out, kk_out, lwf_out, lwb_out, kdf_out, kdb_out, af_out, ab_out,
                      bon_out, g_out, *, nblk):
    i = pl.program_id(1)
    x = x_ref[...]
    tm = x.shape[0]
    has_prev = jnp.where(i > 0, 1.0, 0.0)
    has_next = jnp.where(i < nblk - 1, 1.0, 0.0)
    row = lax.broadcasted_iota(jnp.int32, (tm, 1), 0)
    prev = jnp.where(row == 0, xp_ref[SUBLANES - 1:SUBLANES, :] * has_prev, pltpu.roll(x, 1, 0))
    nxt = jnp.where(row == tm - 1, xn_ref[0:1, :] * has_next, pltpu.roll(x, tm - 1, 0))
    xs = x + mu_ref[0:1, :] * (prev - x) + mu_ref[1:2, :] * (nxt - x)

    G = GROUP_W
    r, k, v = xs[:, 0:G], xs[:, G:2 * G], xs[:, 2 * G:3 * G]
    lora = xs[:, 3 * G:4 * G]
    seg = seg_ref[...]

    def segsum(t):
        hi, lo = _split2(t)
        return _dot(hi, seg) + _dot(lo, seg)

    k_k, k_a, r_k = vec_ref[0:1, :], vec_ref[1:2, :], vec_ref[2:3, :]
    kk = k * k_k
    kk = kk * lax.rsqrt(segsum(kk * kk) + 1e-12)
    g_out[...] = _dot(jax.nn.sigmoid(lora).astype(BF16), g2_ref[...])
    tanh_l = jnp.tanh(lora).astype(BF16)
    lora_b = lora.astype(BF16)
    kd_sum = None
    for d, (lw_out, kd_out, a_out) in enumerate(((lwf_out, kdf_out, af_out), (lwb_out, kdb_out, ab_out))):
        w0, a0 = vec_ref[3 + d:4 + d, :], vec_ref[5 + d:6 + d, :]
        z = w0 + _dot(tanh_l, w2_ref[d])
        w_log = -(jnp.maximum(-z, 0.0) + jnp.log(1.0 + jnp.exp(-jnp.abs(z)))) - 0.5
        lw_out[...] = -jnp.exp(w_log)
        a = jax.nn.sigmoid(a0 + _dot(lora_b, a2_ref[d]))
        kd = k * (1.0 + (a - 1.0) * k_a)
        kd_out[...] = kd
        a_out[...] = a
        kd_sum = kd if kd_sum is None else kd_sum + kd
    r_out[...] = r
    v_out[...] = v
    kk_out[...] = kk
    bon_out[...] = segsum(r * kd_sum * r_k) * v


def _rwkv_features(ub, mu_p, seg, w2_p, a2_p, g2_p, vecs):
    B, T, W = ub.shape
    tm = min(T, 256)
    nblk = T // tm
    hb = tm // SUBLANES
    nh = T // SUBLANES
    full = lambda a: pl.BlockSpec(a.shape, lambda b, i: (0,) * a.ndim)
    out = pl.BlockSpec((None, tm, GROUP_W), lambda b, i: (b, i, 0))
    return pl.pallas_call(
        functools.partial(_rwkv_feat_kernel, nblk=nblk),
        grid=(B, nblk),
        in_specs=[
            pl.BlockSpec((None, tm, W), lambda b, i: (b, i, 0)),
            pl.BlockSpec((None, SUBLANES, W), lambda b, i: (b, jnp.maximum(i * hb - 1, 0), 0)),
            pl.BlockSpec((None, SUBLANES, W), lambda b, i: (b, jnp.minimum((i + 1) * hb, nh - 1), 0)),
            full(mu_p), full(seg), full(w2_p), full(a2_p), full(g2_p), full(vecs),
        ],
        out_specs=[out] * 11,
        out_shape=[jax.ShapeDtypeStruct((B, T, GROUP_W), F32)] * 11,
        compiler_params=_cparams(("parallel", "parallel")),
        name="rwkv_features",
    )(ub, ub, ub, mu_p, seg, w2_p, a2_p, g2_p, vecs)


def _rwkv_chunk_kernel(r_ref, v_ref, kk_ref, lwf_ref, lwb_ref, kdf_ref, kdb_ref, af_ref, ab_ref,
                       cum_ref, tri_ref, hm_ref,
                       rh_out, y1_out, mp_out, np_out, pc_out, *, nsub):
    C = CHUNK
    hm = hm_ref[...]
    eye_w = tri_ref[2]
    lw_refs, kd_refs, a_refs = (lwf_ref, lwb_ref), (kdf_ref, kdb_ref), (af_ref, ab_ref)

    def diag(t):
        return (jnp.concatenate([t, t, t, t], axis=0) * hm).astype(BF16)

    def rows2(a, b):
        return jnp.concatenate([a, b], axis=0).astype(BF16)

    chains = [(ci, d) for ci in range(nsub) for d in range(2)]
    n = len(chains)
    rng = range(n)
    rows = [slice(ci * C, (ci + 1) * C) for ci, _ in chains]

    at, rt, vv, bh, kh, b_d, k_d, e_end = ([None] * n for _ in range(8))
    for i, (ci, d) in enumerate(chains):
        rs = rows[i]
        r, kk = r_ref[rs, :], kk_ref[rs, :]
        lw, kd, a = lw_refs[d][rs, :], kd_refs[d][rs, :], a_refs[d][rs, :]
        cl = _dot_exact_lhs(cum_ref[d], lw)
        last = C - 1 if d == 0 else 0
        e_end[i] = jnp.exp(cl[last:last + 1, :])
        e_neg = jnp.exp(-cl)
        bt = kk * a * e_neg
        kt = kd * e_neg
        vv[i] = v_ref[rs, :]
        at[i] = -kk * jnp.exp(cl - lw)
        rt[i] = r * jnp.exp(cl)
        bh[i], kh[i] = bt * e_end[i], kt * e_end[i]
        b_d[i], k_d[i] = diag(bt), diag(kt)
    a_d = [diag(at[i]) for i in rng]
    v_d = [diag(vv[i]) for i in rng]

    strict = [tri_ref[d] for _, d in chains]
    ar = [rows2(at[i], rt[i]) for i in rng]
    gb = [_dot_nt(ar[i], b_d[i]) for i in rng]
    gk = [_dot_nt(ar[i], k_d[i]) for i in rng]
    l_ab = [gb[i][:C] * strict[i] for i in rng]
    w_rb = [(gb[i][C:] * (strict[i] + eye_w)).astype(BF16) for i in rng]
    w_akrk = [rows2(gk[i][:C] * strict[i], gk[i][C:] * (strict[i] + eye_w)) for i in rng]

    pw = [_dot(l_ab[i].astype(BF16), diag(l_ab[i])) for i in rng]
    tm = [eye_w + l_ab[i] for i in rng]
    for _ in range(4):
        sq = [_dot(rows2(pw[i], tm[i]), diag(pw[i])) for i in rng]
        tm = [tm[i] + sq[i][C:] for i in rng]
        pw = [sq[i][:C] for i in rng]
    tm = [(tm[i] + _dot(tm[i].astype(BF16), diag(pw[i]))).astype(BF16) for i in rng]

    wt = [_dot(tm[i], a_d[i]) for i in rng]
    xy = [_dot(w_akrk[i], v_d[i]) for i in rng]
    u0 = [_dot(tm[i], diag(xy[i][:C])) for i in rng]
    rh = [rt[i] + _dot(w_rb[i], diag(wt[i])) for i in rng]
    y1 = [_dot(w_rb[i], diag(u0[i])) + xy[i][C:] for i in rng]
    for i, (ci, d) in enumerate(chains):
        rh_out[d, rows[i], :] = rh[i].astype(BF16)
        y1_out[d, rows[i], :] = y1[i]
        mp_out[d, ci] = (_dot_tn(wt[i].astype(BF16), bh[i].astype(BF16)) * hm).astype(BF16)
        np_bd = _dot_tn(rows2(u0[i], vv[i]), rows2(bh[i], kh[i])) * hm
        np_out[d, ci] = functools.reduce(
            jnp.add, [np_bd[h * HEAD_DIM:(h + 1) * HEAD_DIM] for h in range(4)])
        pc_out[d, ci] = jnp.broadcast_to(e_end[i], (SUBLANES, GROUP_W))


def _rwkv_chunks(feats, cum, tri, hm):
    r = feats[0]
    B, T, G = r.shape
    nch = T // CHUNK
    nsub = 4 if nch % 4 == 0 else 1
    tb = nsub * CHUNK
    tok = pl.BlockSpec((None, tb, G), lambda b, c: (b, c, 0))
    full = lambda a: pl.BlockSpec(a.shape, lambda b, c: (0,) * a.ndim)
    return pl.pallas_call(
        functools.partial(_rwkv_chunk_kernel, nsub=nsub),
        grid=(B, nch // nsub),
        in_specs=[tok] * 9 + [full(cum), full(tri), full(hm)],
        out_specs=[
            pl.BlockSpec((None, 2, tb, G), lambda b, c: (b, 0, c, 0)),
            pl.BlockSpec((None, 2, tb, G), lambda b, c: (b, 0, c, 0)),
            pl.BlockSpec((None, 2, nsub, G, G), lambda b, c: (b, 0, c, 0, 0)),
            pl.BlockSpec((None, 2, nsub, HEAD_DIM, G), lambda b, c: (b, 0, c, 0, 0)),
            pl.BlockSpec((None, 2, nsub, SUBLANES, G), lambda b, c: (b, 0, c, 0, 0)),
        ],
        out_shape=[
            jax.ShapeDtypeStruct((B, 2, T, G), BF16),
            jax.ShapeDtypeStruct((B, 2, T, G), F32),
            jax.ShapeDtypeStruct((B, 2, nch, G, G), BF16),
            jax.ShapeDtypeStruct((B, 2, nch, HEAD_DIM, G), F32),
            jax.ShapeDtypeStruct((B, 2, nch, SUBLANES, G), F32),
        ],
        compiler_params=_cparams(("parallel", "parallel")),
        name="rwkv_chunk_transitions",
    )(*feats, cum, tri, hm)


def _rwkv_scan_kernel(s0_ref, hm_ref, rhf_ref, y1f_ref, mpf_ref, npf_ref, pcf_ref,
                      rhb_ref, y1b_ref, mpb_ref, npb_ref, pcb_ref,
                      yf_out, yb_out, s_out, *, batch, nsub):
    c = pl.program_id(0)

    @pl.when(c == 0)
    def _():
        s_out[...] = s0_ref[...]

    hm = hm_ref[...]
    dirs = ((rhf_ref, y1f_ref, mpf_ref, npf_ref, pcf_ref, yf_out),
            (rhb_ref, y1b_ref, mpb_ref, npb_ref, pcb_ref, yb_out))
    chains = [(b, d) for b in range(batch) for d in range(2)]
    s = [s_out[b, d] for b, d in chains]
    for step in range(nsub):
        s_d = [(jnp.concatenate([t, t, t, t], axis=0) * hm).astype(BF16) for t in s]
        s_b = [t.astype(BF16) for t in s]
        for i, (b, d) in enumerate(chains):
            rh_ref, y1_ref, mp_ref, np_ref, pc_ref, y_out = dirs[d]
            ci = step if d == 0 else nsub - 1 - step
            rows = slice(ci * CHUNK, (ci + 1) * CHUNK)
            y_out[b, rows, :] = _dot_nt(rh_ref[b, rows, :], s_d[i]) + y1_ref[b, rows, :]
            s[i] = s[i] * pc_ref[b, ci, 0:1, :] + _dot(s_b[i], mp_ref[b, ci]) + np_ref[b, ci]
    for i, (b, d) in enumerate(chains):
        s_out[b, d] = s[i]


def _rwkv_scan(s0, hm, rh, y1, mp, npm, pc):
    B, _, T, G = rh.shape
    nch = T // CHUNK
    nsub = 4 if nch % 4 == 0 else 1
    nblk = nch // nsub
    tb = nsub * CHUNK
    fwd = lambda c: c
    bwd = lambda c: nblk - 1 - c

    def specs(order, d):
        return [
            pl.BlockSpec((B, None, tb, G), lambda c: (0, d, order(c), 0)),
            pl.BlockSpec((B, None, tb, G), lambda c: (0, d, order(c), 0)),
            pl.BlockSpec((B, None, nsub, G, G), lambda c: (0, d, order(c), 0, 0)),
            pl.BlockSpec((B, None, nsub, HEAD_DIM, G), lambda c: (0, d, order(c), 0, 0)),
            pl.BlockSpec((B, None, nsub, SUBLANES, G), lambda c: (0, d, order(c), 0, 0)),
        ]

    return pl.pallas_call(
        functools.partial(_rwkv_scan_kernel, batch=B, nsub=nsub),
        grid=(nblk,),
        in_specs=[pl.BlockSpec(s0.shape, lambda c: (0, 0, 0, 0)), pl.BlockSpec(hm.shape, lambda c: (0, 0))]
        + specs(fwd, 0) + specs(bwd, 1),
        out_specs=[
            pl.BlockSpec((B, tb, G), lambda c: (0, fwd(c), 0)),
            pl.BlockSpec((B, tb, G), lambda c: (0, bwd(c), 0)),
            pl.BlockSpec(s0.shape, lambda c: (0, 0, 0, 0)),
        ],
        out_shape=[
            jax.ShapeDtypeStruct((B, T, G), F32),
            jax.ShapeDtypeStruct((B, T, G), F32),
            jax.ShapeDtypeStruct(s0.shape, F32),
        ],
        compiler_params=_cparams(("arbitrary",)),
        name="rwkv_state_scan",
    )(s0, hm, rh, y1, mp, npm, pc, rh, y1, mp, npm, pc)


def _seg_matrix(width):
    i = np.arange(width) // HEAD_DIM
    return jnp.asarray((i[:, None] == i[None, :]).astype(np.float32), BF16)


def _scan_constants():
    C = CHUNK
    t = np.arange(4 * C)
    head, tok = t // C, t % C
    i = np.arange(C)
    lower = i[:, None] > tok[None, :]
    upper = i[:, None] < tok[None, :]
    eye = i[:, None] == tok[None, :]
    tri = jnp.asarray(np.stack([lower, upper, eye]).astype(np.float32))
    cum = jnp.asarray(np.stack([i[:, None] >= i[None, :], i[:, None] <= i[None, :]]).astype(np.float32), BF16)
    hm = jnp.asarray((head[:, None] == (np.arange(GROUP_W) // HEAD_DIM)[None, :]).astype(np.float32))
    return cum, tri, hm


def _pad_vec(v, lo, total):
    return jnp.zeros((total,), F32).at[lo:lo + v.shape[0]].set(v)


def kernel(x, c, ctx, c_ctx, w_mod, b_mod, norm_ffn1, ffn1_w_gu, ffn1_w_down, norm_mix, w_in, w_out, mla_q_norm, mla_kv_norm, mla_w_uq, mla_w_ukv, mla_qn, mla_kn, rwkv_shift, rwkv_w0, rwkv_w2, rwkv_a0, rwkv_a2, rwkv_g2, rwkv_k_k, rwkv_k_a, rwkv_r_k, rwkv_lnx_g, rwkv_lnx_b, na_qn, na_kn, na_rpb, gqa_qn, gqa_kn, norm_ffn2, ffn2_w_gu, ffn2_w_down):
    B, S, D = x.shape
    Lc = ctx.shape[1]
    depth = w_mod.shape[0]
    assert B + 1 <= SUBLANES and S % 512 == 0 and Lc % CHUNK == 0 and S % GRID_W == 0

    cvec = jnp.zeros((SUBLANES, D), F32).at[:B].set(c).at[B].set(c_ctx)
    mods = _modulation(cvec, w_mod, b_mod).reshape(depth, SUBLANES, 9, D)

    mla_tab = _rope_tables(S // GRID_W, LANES, MLA_NOPE, MLA_ROPE // 2, 1)
    gqa_tab = _rope_tables(S // GRID_W, 4 * HEAD_DIM, 0, HEAD_DIM // 2, 4)
    mla_tab_c = _identity_rope_tables(Lc, LANES)
    gqa_tab_c = _identity_rope_tables(Lc, 4 * HEAD_DIM)

    seg = _seg_matrix(GROUP_W)
    cum, tri, hm = _scan_constants()
    mla_scale = MLA_QK ** -0.5 * LOG2E
    dh_scale = HEAD_DIM ** -0.5 * LOG2E

    w_gu1, w_dn1 = ffn1_w_gu.astype(BF16), ffn1_w_down.astype(BF16)
    w_gu2, w_dn2 = ffn2_w_gu.astype(BF16), ffn2_w_down.astype(BF16)
    w_out_b = w_out.astype(BF16)
    zc = lambda n: jnp.zeros((depth, D, n), BF16)
    wi = w_in.astype(BF16)
    o_b = MLA_COLS
    o_n = o_b + RWKV_COLS
    o_d = o_n + NA_COLS
    w_in_p = jnp.concatenate([
        wi[..., :MLA_Q_RANK + MLA_KV_RANK], zc(64), wi[..., MLA_Q_RANK + MLA_KV_RANK:MLA_COLS], zc(32),
        wi[..., o_b:o_n], zc(UB_W - RWKV_COLS),
        wi[..., o_n:o_d],
        wi[..., o_d:o_d + GQA_COLS]], axis=2)

    h, hc = x, ctx
    for l in range(depth):
        ctx_out = l < depth - 1
        m_lat = mods[l, :B]
        m_ctx = jnp.broadcast_to(mods[l, B], (B, 9, D))


        wuq = mla_w_uq[l].reshape(MLA_Q_RANK, MLA_HEADS, MLA_QK)
        wuq_p = jnp.pad(wuq, ((0, 0), (0, 0), (0, LANES - MLA_QK))).reshape(MLA_Q_RANK, MLA_HEADS * LANES).astype(BF16)
        wukv = mla_w_ukv[l].reshape(MLA_KV_RANK, MLA_HEADS, MLA_NOPE + MLA_V)
        wuk_p = jnp.pad(wukv[..., :MLA_NOPE], ((0, 0), (0, 0), (0, LANES - MLA_NOPE))).reshape(MLA_KV_RANK, MLA_HEADS * LANES).astype(BF16)
        wuv = wukv[..., MLA_NOPE:].reshape(MLA_KV_RANK, MLA_HEADS * MLA_V).astype(BF16)
        mla_small = (mla_q_norm[l].reshape(1, -1), mla_kv_norm[l].reshape(1, -1),
                     _pad_vec(mla_qn[l] * mla_scale, 0, LANES).reshape(1, LANES),
                     _pad_vec(mla_kn[l], 0, LANES).reshape(1, LANES), wuq_p, wuk_p, wuv)

        na_gq = jnp.tile(na_qn[l] * dh_scale, 4).reshape(1, -1)
        na_gk = jnp.tile(na_kn[l], 4).reshape(1, -1)
        gqa_gq = jnp.tile(gqa_qn[l] * dh_scale, 4).reshape(1, -1)
        gqa_gk = jnp.tile(gqa_kn[l], 2).reshape(1, -1)
        na_tb = _na_bias_tables(na_rpb[l])

        mu_p = jnp.zeros((SUBLANES, UB_W), F32).at[:2, :RWKV_COLS].set(rwkv_shift[l])
        lo = 3 * GROUP_W

        def lora_w(w, off):
            return jnp.zeros((GROUP_W, GROUP_W), F32).at[off:off + w.shape[0]].set(w).astype(BF16)

        w2_p = jnp.stack([lora_w(rwkv_w2[l, 0], 0), lora_w(rwkv_w2[l, 1], 32)])
        a2_p = jnp.stack([lora_w(rwkv_a2[l, 0], 64), lora_w(rwkv_a2[l, 1], 96)])
        g2_p = lora_w(rwkv_g2[l], 128)
        feat_vecs = jnp.zeros((SUBLANES, GROUP_W), F32).at[0].set(rwkv_k_k[l]).at[1].set(rwkv_k_a[l]) \
            .at[2].set(rwkv_r_k[l].reshape(-1)).at[3:5].set(rwkv_w0[l]).at[5:7].set(rwkv_a0[l])
        post_vecs = jnp.zeros((SUBLANES, GROUP_W), F32).at[0].set(rwkv_lnx_g[l]).at[1].set(rwkv_lnx_b[l])

        h = _ffn(h, m_lat, norm_ffn1[l], w_gu1, w_dn1, l, 0)
        hc = _ffn(hc, m_ctx, norm_ffn1[l], w_gu1, w_dn1, l, 0)

        ua, ub, un, ud = _inproj(h, m_lat, norm_mix[l], w_in_p, l)
        uca, ucb, ucn, ucd = _inproj(hc, m_ctx, norm_mix[l], w_in_p, l)

        qa, ka, vta = _mla_prep(ua, mla_tab, *mla_small)
        qca, kca, vtca = _mla_prep(uca, mla_tab_c, *mla_small)
        oa = _attention(qa, ka, vta, kca, vtca)

        qn_, kn_, vn_ = _qkv_prep(un, seg, na_gq, na_gk, 4, 4, 4)
        qcn, kcn, vcn, vtcn = _qkv_prep(ucn, seg, na_gq, na_gk, 4, 4, 4, v_cols=True)
        on = _na_attention(qn_, kn_, vn_, kcn, vcn, na_tb)

        qd, kd, vtd = _qkv_prep(ud, seg, gqa_gq, gqa_gk, 4, 2, 2, gqa_tab, v_rows=False, v_cols=True)
        qcd, kcd, vtcd = _qkv_prep(ucd, seg, gqa_gq, gqa_gk, 4, 2, 2, gqa_tab_c, v_rows=False, v_cols=True)
        od = _attention(qd, kd, vtd, kcd, vtcd)

        feats_c = _rwkv_features(ucb, mu_p, seg, w2_p, a2_p, g2_p, feat_vecs)
        feats = _rwkv_features(ub, mu_p, seg, w2_p, a2_p, g2_p, feat_vecs)
        trans_c = _rwkv_chunks(feats_c[:9], cum, tri, hm)
        trans = _rwkv_chunks(feats[:9], cum, tri, hm)
        s0 = jnp.zeros((B, 2, HEAD_DIM, GROUP_W), F32)
        yfc, ybc, s_ctx = _rwkv_scan(s0, hm, *trans_c)
        yf, yb, _ = _rwkv_scan(s_ctx, hm, *trans)

        h = _outproj(h, m_lat, oa, (yf, yb, feats[9], feats[10]), on, od, (True, False, True),
                     seg, post_vecs, w_out_b, l)
        if ctx_out:
            oca = _attention(qca, kca, vtca)
            ocn = _attention(qcn, kcn, vtcn)
            ocd = _attention(qcd, kcd, vtcd)
            hc = _outproj(hc, m_ctx, oca, (yfc, ybc, feats_c[9], feats_c[10]), ocn, ocd,
                          (True, True, True), seg, post_vecs, w_out_b, l)

        h = _ffn(h, m_lat, norm_ffn2[l], w_gu2, w_dn2, l, 6)
        if ctx_out:
            hc = _ffn(hc, m_ctx, norm_ffn2[l], w_gu2, w_dn2, l, 6)
    return h
```

```python
import functools

import numpy as np
import jax
import jax.numpy as jnp
from jax import lax
from jax.experimental import pallas as pl
from jax.experimental.pallas import tpu as pltpu

F32 = jnp.float32
BF16 = jnp.bfloat16

GRID_W = 64
HEAD_DIM = 64
GROUP_W = 256
ROPE_THETA = 10000.0
NORM_EPS = 1e-6
D_FF = 2816
MLA_HEADS = 4
MLA_Q_RANK = 256
MLA_KV_RANK = 128
MLA_NOPE = 64
MLA_ROPE = 32
MLA_V = 64
MLA_QK = MLA_NOPE + MLA_ROPE
RWKV_COLS = 960
RWKV_LNX_EPS = 64e-5
NA_WIN_R = 8
NA_WIN_C = 16
MLA_COLS = 416
NA_COLS = 768
GQA_COLS = 512

LANES = 128
SUBLANES = 8
VMEM_LIMIT_BYTES = 56 * 1024 * 1024

UA_W = 512
UB_W = 1024
UN_W = 768
UD_W = 512

VT_ROWS = 128
CHUNK = 64
NEG_INF = -1e30
LOG2E = 1.4426950408889634
NA_ROWS_PER_ITER = 8
MAX_JUMP = 24.0


def _dot(a, b):
    return jnp.dot(a, b, preferred_element_type=F32)


def _dot_nt(a, b):
    return lax.dot_general(a, b, (((1,), (1,)), ((), ())), preferred_element_type=F32)


def _dot_tn(a, b):
    return lax.dot_general(a, b, (((0,), (0,)), ((), ())), preferred_element_type=F32)


def _split2(x):
    hi = x.astype(BF16)
    lo = (x - hi.astype(F32)).astype(BF16)
    return hi, lo


def _dot_exact_rhs(x, m):
    h1 = x.astype(BF16)
    r1 = x - h1.astype(F32)
    h2 = r1.astype(BF16)
    h3 = (r1 - h2.astype(F32)).astype(BF16)
    return _dot(h1, m) + _dot(h2, m) + _dot(h3, m)


def _dot_exact_lhs(m, x):
    h1 = x.astype(BF16)
    r1 = x - h1.astype(F32)
    h2 = r1.astype(BF16)
    h3 = (r1 - h2.astype(F32)).astype(BF16)
    return _dot(m, h1) + _dot(m, h2) + _dot(m, h3)


def _rms(x, eps=NORM_EPS):
    return x * lax.rsqrt(jnp.mean(x * x, axis=-1, keepdims=True) + eps)


def _cparams(sem):
    return pltpu.CompilerParams(dimension_semantics=sem, vmem_limit_bytes=VMEM_LIMIT_BYTES)


def _mod_kernel(c_ref, w_ref, b_ref, o_ref):
    c = c_ref[...]
    s = c * jax.nn.sigmoid(c)
    o_ref[...] = jnp.dot(s, w_ref[...], preferred_element_type=F32,
                         precision=lax.Precision.HIGHEST) + b_ref[...]


def _modulation(cvec, w_mod, b_mod):
    L, D, N = w_mod.shape
    tn = 1024
    return pl.pallas_call(
        _mod_kernel,
        grid=(L, N // tn),
        in_specs=[
            pl.BlockSpec((SUBLANES, D), lambda l, n: (0, 0)),
            pl.BlockSpec((None, D, tn), lambda l, n: (l, 0, n)),
            pl.BlockSpec((None, 1, tn), lambda l, n: (l, 0, n)),
        ],
        out_specs=pl.BlockSpec((None, SUBLANES, tn), lambda l, n: (l, 0, n)),
        out_shape=jax.ShapeDtypeStruct((L, SUBLANES, N), F32),
        compiler_params=_cparams(("parallel", "parallel")),
        name="adaln_mod",
    )(cvec, w_mod, b_mod.reshape(L, 1, N))


def _ffn_kernel(h_ref, mod_ref, g_ref, wg_ref, wu_ref, wd_ref, o_ref, *, base):
    x = h_ref[...]
    y = _rms(x) * g_ref[...]
    xn = (y * (1.0 + mod_ref[base + 1:base + 2, :]) + mod_ref[base:base + 1, :]).astype(BF16)
    g = _dot(xn, wg_ref[...])
    u = _dot(xn, wu_ref[...])
    a = (g * jax.nn.sigmoid(g) * u).astype(BF16)
    o_ref[...] = x + 0.5 * mod_ref[base + 2:base + 3, :] * _dot(a, wd_ref[...])


def _ffn(h, mod, gain, w_gu, w_down, layer, base):
    B, T, D = h.shape
    F = w_down.shape[1]
    tm = min(T, 512)
    resident = pl.Buffered(1)
    return pl.pallas_call(
        functools.partial(_ffn_kernel, base=base),
        grid=(B, T // tm),
        in_specs=[
            pl.BlockSpec((None, tm, D), lambda b, i: (b, i, 0)),
            pl.BlockSpec((None, 9, D), lambda b, i: (b, 0, 0)),
            pl.BlockSpec((1, D), lambda b, i: (0, 0)),
            pl.BlockSpec((None, D, F), lambda b, i: (layer, 0, 0), pipeline_mode=resident),
            pl.BlockSpec((None, D, F), lambda b, i: (layer, 0, 1), pipeline_mode=resident),
            pl.BlockSpec((None, F, D), lambda b, i: (layer, 0, 0), pipeline_mode=resident),
        ],
        out_specs=pl.BlockSpec((None, tm, D), lambda b, i: (b, i, 0)),
        out_shape=jax.ShapeDtypeStruct((B, T, D), F32),
        compiler_params=_cparams(("parallel", "parallel")),
        name="swiglu_halfstep",
    )(h, mod, gain.reshape(1, D), w_gu, w_gu, w_down)


def _rwkv_out(yf_ref, yb_ref, bon_ref, g_ref, seg_ref, vec_ref):
    y = yf_ref[...] + yb_ref[...]
    seg = seg_ref[...]

    def segmean(t):
        hi, lo = _split2(t)
        return (_dot(hi, seg) + _dot(lo, seg)) * (1.0 / HEAD_DIM)

    mu = segmean(y)
    yc = y - mu
    var = segmean(yc * yc)
    yn = yc * lax.rsqrt(var + RWKV_LNX_EPS) * vec_ref[0:1, :] + vec_ref[1:2, :]
    return ((yn + bon_ref[...]) * g_ref[...]).astype(BF16)


def _outproj_kernel(h_ref, mod_ref, oa_ref, yf_ref, yb_ref, bon_ref, g_ref, on_ref, od_ref,
                    seg_ref, vec_ref, w_ref, o_ref, *, transposed):
    ob = _rwkv_out(yf_ref, yb_ref, bon_ref, g_ref, seg_ref, vec_ref)
    groups = ((oa_ref, transposed[0]), (None, False), (on_ref, transposed[1]), (od_ref, transposed[2]))
    acc = None
    for g, (ref, t) in enumerate(groups):
        w = w_ref[g * GROUP_W:(g + 1) * GROUP_W, :]
        x = ob if ref is None else ref[...]
        part = _dot_tn(x, w) if t else _dot(x, w)
        acc = part if acc is None else acc + part
    o_ref[...] = h_ref[...] + mod_ref[5:6, :] * acc


def _outproj(h, mod, oa, rwkv, on, od, transposed, seg, post_vecs, w_out, layer):
    B, T, D = h.shape
    tm = min(T, 512)
    grp = pl.BlockSpec((None, tm, GROUP_W), lambda b, i: (b, i, 0))
    grp_t = pl.BlockSpec((None, GROUP_W, tm), lambda b, i: (b, 0, i))
    pick = lambda t: grp_t if t else grp
    full = lambda a: pl.BlockSpec(a.shape, lambda b, i: (0,) * a.ndim)
    return pl.pallas_call(
        functools.partial(_outproj_kernel, transposed=tuple(transposed)),
        grid=(B, T // tm),
        in_specs=[
            pl.BlockSpec((None, tm, D), lambda b, i: (b, i, 0)),
            pl.BlockSpec((None, 9, D), lambda b, i: (b, 0, 0)),
            pick(transposed[0]), grp, grp, grp, grp, pick(transposed[1]), pick(transposed[2]),
            full(seg), full(post_vecs),
            pl.BlockSpec((None, 4 * GROUP_W, D), lambda b, i: (layer, 0, 0)),
        ],
        out_specs=pl.BlockSpec((None, tm, D), lambda b, i: (b, i, 0)),
        out_shape=jax.ShapeDtypeStruct((B, T, D), F32),
        compiler_params=_cparams(("parallel", "parallel")),
        name="out_projection",
    )(h, mod, oa, *rwkv, on, od, seg, post_vecs, w_out)


def _rope_tables(n_rows, width, lead, group, n_rep):
    half = group // 2
    lane = np.arange(width)
    rel = (lane - lead) % (2 * group)
    inside = (lane >= lead) & (lane < lead + n_rep * 2 * group)
    is_row = (rel < group)[None, None, :]
    second = ((rel % group >= half) & inside)[None, None, :]
    first = ((rel % group < half) & inside)[None, None, :]
    inv_freq = ROPE_THETA ** (-jnp.arange(half, dtype=F32) / half)
    freq = jnp.where(inside, inv_freq[rel % half], 0.0)[None, :]
    ang_r = jnp.arange(n_rows).astype(F32)[:, None] * freq
    ang_c = jnp.arange(GRID_W).astype(F32)[:, None] * freq
    shape = (n_rows, GRID_W, width)
    c = jnp.where(is_row, jnp.cos(ang_r)[:, None, :], jnp.cos(ang_c)[None, :, :])
    s = jnp.where(is_row, jnp.sin(ang_r)[:, None, :], jnp.sin(ang_c)[None, :, :])
    flat = lambda t: jnp.broadcast_to(t, shape).reshape(n_rows * GRID_W, width)
    return [flat(c), flat(jnp.where(second, s, 0.0)), flat(jnp.where(first, -s, 0.0))]


def _identity_rope_tables(n_tokens, width):
    return [jnp.ones((n_tokens, width), F32), jnp.zeros((n_tokens, width), F32),
            jnp.zeros((n_tokens, width), F32)]


def _vt_ext(v):
    tm = v.shape[0]
    lane = lax.broadcasted_iota(jnp.int32, (tm, VT_ROWS - HEAD_DIM), 1)
    aux = jnp.where(lane == 0, 1.0, 0.0)
    return jnp.concatenate([v, aux], axis=-1).T.astype(BF16)


def _mla_prep_math(ua, c_ref, s1_ref, s2_ref, qnorm_ref, kvnorm_ref, qn_ref, kn_ref,
                   wuq_ref, wuk_ref, wuv_ref, q_out, k_out, vt_out):
    qc = (_rms(ua[:, :MLA_Q_RANK]) * qnorm_ref[...]).astype(BF16)
    kvc = (_rms(ua[:, MLA_Q_RANK:MLA_Q_RANK + MLA_KV_RANK]) * kvnorm_ref[...]).astype(BF16)
    k_rope = ua[:, 3 * LANES:4 * LANES]
    q_all = _dot(qc, wuq_ref[...])
    k_all = _dot(kvc, wuk_ref[...])
    v_all = _dot(kvc, wuv_ref[...])
    cos, s_dn, s_up = c_ref[...], s1_ref[...], s2_ref[...]
    half = MLA_ROPE // 4

    def rope(x):
        return x * cos + pltpu.roll(x, half, 1) * s_dn + pltpu.roll(x, LANES - half, 1) * s_up

    def headnorm(x, gain):
        ms = jnp.sum(x * x, axis=-1, keepdims=True) * (1.0 / MLA_QK)
        return x * lax.rsqrt(ms + NORM_EPS) * gain

    for h in range(MLA_HEADS):
        qh = headnorm(q_all[:, h * LANES:(h + 1) * LANES], qn_ref[...])
        q_out[h] = rope(qh).astype(BF16)
        kh = headnorm(k_all[:, h * LANES:(h + 1) * LANES] + k_rope, kn_ref[...])
        k_out[h] = rope(kh).astype(BF16)
        vt_out[h] = _vt_ext(v_all[:, h * MLA_V:(h + 1) * MLA_V])


def _qkv_prep_math(x, seg_ref, gq_ref, gk_ref, tabs, q_out, k_out, v_out, vt_out, *, nq, nk, nv):
    use_rope = tabs is not None
    wq, wk, wv = nq * HEAD_DIM, nk * HEAD_DIM, nv * HEAD_DIM
    q = x[:, 0:wq]
    k = x[:, wq:wq + wk]
    v = x[:, wq + wk:wq + wk + wv]

    def headnorm(t, gain, w):
        hi, lo = _split2(t * t)
        seg = seg_ref[0:w, 0:w]
        ms = (_dot(hi, seg) + _dot(lo, seg)) * (1.0 / HEAD_DIM)
        return t * lax.rsqrt(ms + NORM_EPS) * gain

    q = headnorm(q, gq_ref[...], wq)
    k = headnorm(k, gk_ref[...], wk)
    if use_rope:
        c_ref, s1_ref, s2_ref = tabs
        half = HEAD_DIM // 4

        def rope(t, w):
            return (t * c_ref[:, 0:w] + pltpu.roll(t, half, 1) * s1_ref[:, 0:w]
                    + pltpu.roll(t, w - half, 1) * s2_ref[:, 0:w])

        q = rope(q, wq)
        k = rope(k, wk)
    for h in range(nq):
        q_out[h] = q[:, h * HEAD_DIM:(h + 1) * HEAD_DIM].astype(BF16)
    for h in range(nk):
        k_out[h] = k[:, h * HEAD_DIM:(h + 1) * HEAD_DIM].astype(BF16)
    for h in range(nv):
        vh = v[:, h * HEAD_DIM:(h + 1) * HEAD_DIM]
        if v_out is not None:
            v_out[h] = vh.astype(BF16)
        if vt_out is not None:
            vt_out[h] = _vt_ext(vh)


def _inproj_kernel(h_ref, mod_ref, g_ref, w_ref, oa_ref, ob_ref, on_ref, od_ref):
    y = _rms(h_ref[...]) * g_ref[...]
    xm = (y * (1.0 + mod_ref[4:5, :]) + mod_ref[3:4, :]).astype(BF16)
    o = 0
    for ref, w in ((oa_ref, UA_W), (ob_ref, UB_W), (on_ref, UN_W), (od_ref, UD_W)):
        ref[...] = _dot(xm, w_ref[:, o:o + w])
        o += w


def _inproj(h, mod, gain, w_in_p, layer):
    B, T, D = h.shape
    tm = min(T, 512)
    W = w_in_p.shape[2]
    widths = (UA_W, UB_W, UN_W, UD_W)
    return pl.pallas_call(
        _inproj_kernel,
        grid=(B, T // tm),
        in_specs=[
            pl.BlockSpec((None, tm, D), lambda b, i: (b, i, 0)),
            pl.BlockSpec((None, 9, D), lambda b, i: (b, 0, 0)),
            pl.BlockSpec((1, D), lambda b, i: (0, 0)),
            pl.BlockSpec((None, D, W), lambda b, i: (layer, 0, 0)),
        ],
        out_specs=[pl.BlockSpec((None, tm, w), lambda b, i: (b, i, 0)) for w in widths],
        out_shape=[jax.ShapeDtypeStruct((B, T, w), F32) for w in widths],
        compiler_params=_cparams(("parallel", "parallel")),
        name="in_projection",
    )(h, mod, gain.reshape(1, D), w_in_p)


def _mla_prep_kernel(ua_ref, *refs):
    _mla_prep_math(ua_ref[...], *refs)


def _mla_prep(ua, tables, q_norm, kv_norm, qn_p, kn_p, wuq_p, wuk_p, wuv):
    B, T, _ = ua.shape
    tm = min(T, 512)
    tab = pl.BlockSpec((tm, LANES), lambda b, i: (i, 0))
    full = lambda a: pl.BlockSpec(a.shape, lambda b, i: (0,) * a.ndim)
    small = [q_norm, kv_norm, qn_p, kn_p, wuq_p, wuk_p, wuv]
    return pl.pallas_call(
        _mla_prep_kernel,
        grid=(B, T // tm),
        in_specs=[pl.BlockSpec((None, tm, UA_W), lambda b, i: (b, i, 0)), tab, tab, tab]
        + [full(a) for a in small],
        out_specs=[
            pl.BlockSpec((None, MLA_HEADS, tm, LANES), lambda b, i: (b, 0, i, 0)),
            pl.BlockSpec((None, MLA_HEADS, tm, LANES), lambda b, i: (b, 0, i, 0)),
            pl.BlockSpec((None, MLA_HEADS, VT_ROWS, tm), lambda b, i: (b, 0, 0, i)),
        ],
        out_shape=[
            jax.ShapeDtypeStruct((B, MLA_HEADS, T, LANES), BF16),
            jax.ShapeDtypeStruct((B, MLA_HEADS, T, LANES), BF16),
            jax.ShapeDtypeStruct((B, MLA_HEADS, VT_ROWS, T), BF16),
        ],
        compiler_params=_cparams(("parallel", "parallel")),
        name="mla_prep",
    )(ua, *tables, *small)


def _qkv_prep_kernel(*refs, nq, nk, nv, use_rope, v_rows, v_cols):
    n_in = 7 if use_rope else 4
    x_ref, seg_ref, gq_ref, gk_ref = refs[:4]
    tabs = refs[4:7] if use_rope else None
    q_out, k_out = refs[n_in:n_in + 2]
    v_outs = list(refs[n_in + 2:])
    v_out = v_outs.pop(0) if v_rows else None
    vt_out = v_outs.pop(0) if v_cols else None
    _qkv_prep_math(x_ref[...], seg_ref, gq_ref, gk_ref, tabs, q_out, k_out, v_out, vt_out,
                   nq=nq, nk=nk, nv=nv)


def _qkv_prep(x, seg, gq, gk, nq, nk, nv, tables=None, v_rows=True, v_cols=False):
    B, T, W = x.shape
    tm = min(T, 512)
    use_rope = tables is not None
    full = lambda a: pl.BlockSpec(a.shape, lambda b, i: (0,) * a.ndim)
    in_specs = [pl.BlockSpec((None, tm, W), lambda b, i: (b, i, 0)), full(seg), full(gq), full(gk)]
    args = [x, seg, gq, gk]
    if use_rope:
        in_specs += [pl.BlockSpec((tm, tables[0].shape[1]), lambda b, i: (i, 0))] * 3
        args += list(tables)
    hm = lambda n: pl.BlockSpec((None, n, tm, HEAD_DIM), lambda b, i: (b, 0, i, 0))
    out_specs = [hm(nq), hm(nk)]
    out_shape = [jax.ShapeDtypeStruct((B, n, T, HEAD_DIM), BF16) for n in (nq, nk)]
    if v_rows:
        out_specs.append(hm(nv))
        out_shape.append(jax.ShapeDtypeStruct((B, nv, T, HEAD_DIM), BF16))
    if v_cols:
        out_specs.append(pl.BlockSpec((None, nv, VT_ROWS, tm), lambda b, i: (b, 0, 0, i)))
        out_shape.append(jax.ShapeDtypeStruct((B, nv, VT_ROWS, T), BF16))
    return pl.pallas_call(
        functools.partial(_qkv_prep_kernel, nq=nq, nk=nk, nv=nv, use_rope=use_rope,
                          v_rows=v_rows, v_cols=v_cols),
        grid=(B, T // tm),
        in_specs=in_specs,
        out_specs=out_specs,
        out_shape=out_shape,
        compiler_params=_cparams(("parallel", "parallel")),
        name="qkv_prep_rope" if use_rope else "qkv_prep",
    )(*args)


def _attn_kernel(*refs, n_heads, group, has_extra, nk):
    if has_extra:
        q_ref, k_ref, vt_ref, kx_ref, vtx_ref, o_ref, m_ref, acc_ref = refs
    else:
        q_ref, k_ref, vt_ref, o_ref, m_ref, acc_ref = refs
    j = pl.program_id(2)

    heads = range(n_heads)

    def first(kr, vr):
        s = [_dot_nt(kr[h // group], q_ref[h]) for h in heads]
        m = [jnp.max(s[h], axis=0, keepdims=True) for h in heads]
        p = [jnp.exp2(s[h] - m[h]).astype(BF16) for h in heads]
        for h in heads:
            m_ref[h] = m[h]
            acc_ref[h] = _dot(vr[h // group], p[h])

    def update(kr, vr):
        m_prev = [m_ref[h] for h in heads]
        s = [_dot_nt(kr[h // group], q_ref[h]) for h in heads]
        pv = [_dot(vr[h // group], jnp.exp2(s[h] - m_prev[h]).astype(BF16)) for h in heads]
        m_blk = [jnp.max(s[h], axis=0, keepdims=True) for h in heads]
        jump = functools.reduce(jnp.maximum, [m_blk[h] - m_prev[h] for h in heads])
        safe = jnp.max(jump) <= MAX_JUMP

        @pl.when(safe)
        def _():
            for h in heads:
                m_new = jnp.maximum(m_prev[h], m_blk[h])
                m_ref[h] = m_new
                acc_ref[h] = (acc_ref[h] + pv[h]) * jnp.exp2(m_prev[h] - m_new)

        @pl.when(jnp.logical_not(safe))
        def _():
            s2 = [_dot_nt(kr[h // group], q_ref[h]) for h in heads]
            m_new = [jnp.maximum(m_ref[h], jnp.max(s2[h], axis=0, keepdims=True)) for h in heads]
            p2 = [jnp.exp2(s2[h] - m_new[h]).astype(BF16) for h in heads]
            for h in heads:
                acc_ref[h] = jnp.exp2(m_ref[h] - m_new[h]) * acc_ref[h] + _dot(vr[h // group], p2[h])
                m_ref[h] = m_new[h]

    if has_extra:
        @pl.when(j == 0)
        def _():
            first(kx_ref, vtx_ref)

        update(k_ref, vt_ref)
    else:
        @pl.when(j == 0)
        def _():
            first(k_ref, vt_ref)

        @pl.when(j > 0)
        def _():
            update(k_ref, vt_ref)

    @pl.when(j == nk - 1)
    def _():
        outs = []
        for h in range(n_heads):
            a = acc_ref[h]
            outs.append(a[:HEAD_DIM] / a[HEAD_DIM:HEAD_DIM + 1])
        o_ref[...] = jnp.concatenate(outs, axis=0).astype(o_ref.dtype)


def _attention(q, k, vt, kx=None, vtx=None):
    B, H, Tq, dq = q.shape
    Hk, Tk = k.shape[1], k.shape[2]
    group = H // Hk
    tq = min(Tq, 512)
    tk = min(Tk, 2048)
    nk = Tk // tk
    has_extra = kx is not None
    in_specs = [
        pl.BlockSpec((None, H, tq, dq), lambda b, i, j: (b, 0, i, 0)),
        pl.BlockSpec((None, Hk, tk, dq), lambda b, i, j: (b, 0, j, 0)),
        pl.BlockSpec((None, Hk, VT_ROWS, tk), lambda b, i, j: (b, 0, 0, j)),
    ]
    args = [q, k, vt]
    if has_extra:
        Tx = kx.shape[2]
        in_specs += [pl.BlockSpec((None, Hk, Tx, dq), lambda b, i, j: (b, 0, 0, 0)),
                     pl.BlockSpec((None, Hk, VT_ROWS, Tx), lambda b, i, j: (b, 0, 0, 0))]
        args += [kx, vtx]
    return pl.pallas_call(
        functools.partial(_attn_kernel, n_heads=H, group=group, has_extra=has_extra, nk=nk),
        grid=(B, Tq // tq, nk),
        in_specs=in_specs,
        out_specs=pl.BlockSpec((None, H * HEAD_DIM, tq), lambda b, i, j: (b, 0, i)),
        out_shape=jax.ShapeDtypeStruct((B, H * HEAD_DIM, Tq), BF16),
        scratch_shapes=[pltpu.VMEM((H, 1, tq), F32), pltpu.VMEM((H, VT_ROWS, tq), F32)],
        compiler_params=_cparams(("parallel", "parallel", "arbitrary")),
        name="flash_attention_ctx" if has_extra else "flash_attention",
    )(*args)


def _na_bias_kernel(r_ref, e_ref, m_ref, o_ref):
    o_ref[...] = _dot_exact_rhs(r_ref[...], e_ref[...]) + m_ref[...]


def _na_bias_tables(rpb):
    H, n_dr, n_dc = rpb.shape
    W = GRID_W
    qc = np.arange(W)[:, None]
    kc = np.arange(W)[None, :]
    c0 = np.clip(qc - NA_WIN_C // 2, 0, W - NA_WIN_C)
    valid = (kc >= c0) & (kc < c0 + NA_WIN_C)
    d = kc - qc + NA_WIN_C - 1
    onehot = np.zeros((LANES, W * W), np.float32)
    for a in range(W):
        for b in range(W):
            if valid[a, b]:
                onehot[d[a, b], a * W + b] = 1.0
    mask = np.where(valid, 0.0, NEG_INF).astype(np.float32).reshape(1, W * W)
    rows = H * n_dr
    rows_p = -(-rows // SUBLANES) * SUBLANES
    r2 = jnp.zeros((rows_p, LANES), F32).at[:rows, :n_dc].set(rpb.reshape(rows, n_dc) * LOG2E)
    tn = 1024
    tiles = pl.pallas_call(
        _na_bias_kernel,
        grid=(W * W // tn,),
        in_specs=[pl.BlockSpec((rows_p, LANES), lambda n: (0, 0)),
                  pl.BlockSpec((LANES, tn), lambda n: (0, n)),
                  pl.BlockSpec((1, tn), lambda n: (0, n))],
        out_specs=pl.BlockSpec((rows_p, tn), lambda n: (0, n)),
        out_shape=jax.ShapeDtypeStruct((rows_p, W * W), F32),
        compiler_params=_cparams(("parallel",)),
        name="na_bias_expand",
    )(r2, jnp.asarray(onehot, BF16), jnp.asarray(mask))
    tiles = tiles[:rows].reshape(H, n_dr, W, W)
    slabs = [tiles[:, base:base + NA_WIN_R].transpose(0, 2, 1, 3).reshape(H, W, NA_WIN_R * W)
             for base in range(NA_WIN_R)]
    return jnp.stack(slabs, 0)


def _na_kernel(q_ref, k_ref, v_ref, kc_ref, vc_ref, tb_ref, o_ref, *, rb, rows, n_heads):
    blk = pl.program_id(1)
    W = GRID_W
    win = NA_WIN_R * W

    nr = NA_ROWS_PER_ITER

    def rows_body(i, carry):
        q0 = pl.multiple_of(i * nr * W, nr * W)
        heads = range(n_heads)
        units = [(j, h) for j in range(nr) for h in heads]
        k0, base = [], []
        for j in range(nr):
            r = blk * rb + nr * i + j
            r0 = jnp.clip(r - NA_WIN_R // 2, 0, rows - NA_WIN_R)
            base.append(r0 - r + NA_WIN_R - 1)
            k0.append(pl.multiple_of(r0 * W, W))
        qs = [q_ref[h, pl.ds(q0, nr * W), :] for h in heads]
        s_cxs = [_dot_nt(qs[h], kc_ref[h]) for h in heads]
        s_nb = [_dot_nt(qs[h][j * W:(j + 1) * W], k_ref[h, pl.ds(k0[j], win), :]) + tb_ref[base[j], h]
                for j, h in units]
        s_cx = [s_cxs[h][j * W:(j + 1) * W] for j, h in units]
        n = len(units)
        m = [jnp.maximum(jnp.max(s_nb[u], axis=-1, keepdims=True),
                         jnp.max(s_cx[u], axis=-1, keepdims=True)) for u in range(n)]
        p_nb = [jnp.exp2(s_nb[u] - m[u]) for u in range(n)]
        p_cx = [jnp.exp2(s_cx[u] - m[u]) for u in range(n)]
        l = [jnp.sum(p_nb[u], axis=-1, keepdims=True) + jnp.sum(p_cx[u], axis=-1, keepdims=True)
             for u in range(n)]
        o_cxs = [_dot(jnp.concatenate([p_cx[j * n_heads + h] for j in range(nr)], axis=0).astype(BF16),
                      vc_ref[h]) for h in heads]
        o = [_dot(p_nb[u].astype(BF16), v_ref[h, pl.ds(k0[j], win), :]) + o_cxs[h][j * W:(j + 1) * W]
             for u, (j, h) in enumerate(units)]
        o_ref[pl.ds(q0, nr * W), :] = jnp.concatenate(
            [jnp.concatenate([o[j * n_heads + h] / l[j * n_heads + h] for h in heads], axis=-1)
             for j in range(nr)], axis=0).astype(o_ref.dtype)
        return carry

    lax.fori_loop(0, rb // nr, rows_body, 0)


def _na_attention(q, k, v, kc, vc, tb):
    B, H, S, dh = q.shape
    L = kc.shape[2]
    rows = S // GRID_W
    assert rows >= NA_WIN_R
    rb = 8
    return pl.pallas_call(
        functools.partial(_na_kernel, rb=rb, rows=rows, n_heads=H),
        grid=(B, rows // rb),
        in_specs=[
            pl.BlockSpec((None, H, rb * GRID_W, dh), lambda b, i: (b, 0, i, 0)),
            pl.BlockSpec((None, H, S, dh), lambda b, i: (b, 0, 0, 0)),
            pl.BlockSpec((None, H, S, dh), lambda b, i: (b, 0, 0, 0)),
            pl.BlockSpec((None, H, L, dh), lambda b, i: (b, 0, 0, 0)),
            pl.BlockSpec((None, H, L, dh), lambda b, i: (b, 0, 0, 0)),
            pl.BlockSpec(tb.shape, lambda b, i: (0, 0, 0, 0)),
        ],
        out_specs=pl.BlockSpec((None, rb * GRID_W, H * dh), lambda b, i: (b, i, 0)),
        out_shape=jax.ShapeDtypeStruct((B, S, H * dh), BF16),
        compiler_params=_cparams(("parallel", "arbitrary")),
        name="neighbourhood_attention",
    )(q, k, v, kc, vc, tb)


def _rwkv_feat_kernel(x_ref, xp_ref, xn_ref, mu_ref, seg_ref, w2_ref, a2_ref, g2_ref, vec_ref,
                      r_out, v_out, kk_out, lwf_out, lwb_out, kdf_out, kdb_out, af_out, ab_out,
                      bon_out, g_out, *, nblk):
    i = pl.program_id(1)
    x = x_ref[...]
    tm = x.shape[0]
    has_prev = jnp.where(i > 0, 1.0, 0.0)
    has_next = jnp.where(i < nblk - 1, 1.0, 0.0)
    row = lax.broadcasted_iota(jnp.int32, (tm, 1), 0)
    prev = jnp.where(row == 0, xp_ref[SUBLANES - 1:SUBLANES, :] * has_prev, pltpu.roll(x, 1, 0))
    nxt = jnp.where(row == tm - 1, xn_ref[0:1, :] * has_next, pltpu.roll(x, tm - 1, 0))
    xs = x + mu_ref[0:1, :] * (prev - x) + mu_ref[1:2, :] * (nxt - x)

    G = GROUP_W
    r, k, v = xs[:, 0:G], xs[:, G:2 * G], xs[:, 2 * G:3 * G]
    lora = xs[:, 3 * G:4 * G]
    seg = seg_ref[...]

    def segsum(t):
        hi, lo = _split2(t)
        return _dot(hi, seg) + _dot(lo, seg)

    k_k, k_a, r_k = vec_ref[0:1, :], vec_ref[1:2, :], vec_ref[2:3, :]
    kk = k * k_k
    kk = kk * lax.rsqrt(segsum(kk * kk) + 1e-12)
    g_out[...] = _dot(jax.nn.sigmoid(lora).astype(BF16), g2_ref[...])
    tanh_l = jnp.tanh(lora).astype(BF16)
    lora_b = lora.astype(BF16)
    kd_sum = None
    for d, (lw_out, kd_out, a_out) in enumerate(((lwf_out, kdf_out, af_out), (lwb_out, kdb_out, ab_out))):
        w0, a0 = vec_ref[3 + d:4 + d, :], vec_ref[5 + d:6 + d, :]
        z = w0 + _dot(tanh_l, w2_ref[d])
        w_log = -(jnp.maximum(-z, 0.0) + jnp.log(1.0 + jnp.exp(-jnp.abs(z)))) - 0.5
        lw_out[...] = -jnp.exp(w_log)
        a = jax.nn.sigmoid(a0 + _dot(lora_b, a2_ref[d]))
        kd = k * (1.0 + (a - 1.0) * k_a)
        kd_out[...] = kd
        a_out[...] = a
        kd_sum = kd if kd_sum is None else kd_sum + kd
    r_out[...] = r
    v_out[...] = v
    kk_out[...] = kk
    bon_out[...] = segsum(r * kd_sum * r_k) * v


def _rwkv_features(ub, mu_p, seg, w2_p, a2_p, g2_p, vecs):
    B, T, W = ub.shape
    tm = min(T, 256)
    nblk = T // tm
    hb = tm // SUBLANES
    nh = T // SUBLANES
    full = lambda a: pl.BlockSpec(a.shape, lambda b, i: (0,) * a.ndim)
    out = pl.BlockSpec((None, tm, GROUP_W), lambda b, i: (b, i, 0))
    return pl.pallas_call(
        functools.partial(_rwkv_feat_kernel, nblk=nblk),
        grid=(B, nblk),
        in_specs=[
            pl.BlockSpec((None, tm, W), lambda b, i: (b, i, 0)),
            pl.BlockSpec((None, SUBLANES, W), lambda b, i: (b, jnp.maximum(i * hb - 1, 0), 0)),
            pl.BlockSpec((None, SUBLANES, W), lambda b, i: (b, jnp.minimum((i + 1) * hb, nh - 1), 0)),
            full(mu_p), full(seg), full(w2_p), full(a2_p), full(g2_p), full(vecs),
        ],
        out_specs=[out] * 11,
        out_shape=[jax.ShapeDtypeStruct((B, T, GROUP_W), F32)] * 11,
        compiler_params=_cparams(("parallel", "parallel")),
        name="rwkv_features",
    )(ub, ub, ub, mu_p, seg, w2_p, a2_p, g2_p, vecs)


def _rwkv_chunk_kernel(r_ref, v_ref, kk_ref, lwf_ref, lwb_ref, kdf_ref, kdb_ref, af_ref, ab_ref,
                       cum_ref, tri_ref, hm_ref,
                       rh_out, y1_out, mp_out, np_out, pc_out, *, nsub):
    C = CHUNK
    hm = hm_ref[...]
    hm_b = hm.astype(BF16)
    eye_w = tri_ref[2]
    lw_refs, kd_refs, a_refs = (lwf_ref, lwb_ref), (kdf_ref, kdb_ref), (af_ref, ab_ref)

    def diag(t):
        tb = t.astype(BF16)
        return jnp.concatenate([tb, tb, tb, tb], axis=0) * hm_b

    def rows2(a, b):
        return jnp.concatenate([a, b], axis=0).astype(BF16)

    chains = [(ci, d) for ci in range(nsub) for d in range(2)]
    n = len(chains)
    rng = range(n)
    rows = [slice(ci * C, (ci + 1) * C) for ci, _ in chains]

    at, rt, vv, bh, kh, b_d, k_d, e_end = ([None] * n for _ in range(8))
    for i, (ci, d) in enumerate(chains):
        rs = rows[i]
        r, kk = r_ref[rs, :], kk_ref[rs, :]
        lw, kd, a = lw_refs[d][rs, :], kd_refs[d][rs, :], a_refs[d][rs, :]
        cl = _dot_exact_lhs(cum_ref[d], lw)
        last = C - 1 if d == 0 else 0
        e_end[i] = jnp.exp(cl[last:last + 1, :])
        e_neg = jnp.exp(-cl)
        bt = kk * a * e_neg
        kt = kd * e_neg
        vv[i] = v_ref[rs, :]
        at[i] = -kk * jnp.exp(cl - lw)
        rt[i] = r * jnp.exp(cl)
        bh[i], kh[i] = bt * e_end[i], kt * e_end[i]
        b_d[i], k_d[i] = diag(bt), diag(kt)
    a_d = [diag(at[i]) for i in rng]
    v_d = [diag(vv[i]) for i in rng]

    strict = [tri_ref[d] for _, d in chains]
    ar = [rows2(at[i], rt[i]) for i in rng]
    gb = [_dot_nt(ar[i], b_d[i]) for i in rng]
    gk = [_dot_nt(ar[i], k_d[i]) for i in rng]
    l_ab = [gb[i][:C] * strict[i] for i in rng]
    w_rb = [(gb[i][C:] * (strict[i] + eye_w)).astype(BF16) for i in rng]
    w_akrk = [rows2(gk[i][:C] * strict[i], gk[i][C:] * (strict[i] + eye_w)) for i in rng]

    pw = [_dot(l_ab[i].astype(BF16), diag(l_ab[i])) for i in rng]
    tm = [eye_w + l_ab[i] for i in rng]
    for _ in range(4):
        sq = [_dot(rows2(pw[i], tm[i]), diag(pw[i])) for i in rng]
        tm = [tm[i] + sq[i][C:] for i in rng]
        pw = [sq[i][:C] for i in rng]
    tm = [(tm[i] + _dot(tm[i].astype(BF16), diag(pw[i]))).astype(BF16) for i in rng]

    wt = [_dot(tm[i], a_d[i]) for i in rng]
    xy = [_dot(w_akrk[i], v_d[i]) for i in rng]
    u0 = [_dot(tm[i], diag(xy[i][:C])) for i in rng]
    rh = [rt[i] + _dot(w_rb[i], diag(wt[i])) for i in rng]
    y1 = [_dot(w_rb[i], diag(u0[i])) + xy[i][C:] for i in rng]
    for i, (ci, d) in enumerate(chains):
        rh_out[d, rows[i], :] = rh[i].astype(BF16)
        y1_out[d, rows[i], :] = y1[i]
        mp_out[d, ci] = (_dot_tn(wt[i].astype(BF16), bh[i].astype(BF16)) * hm).astype(BF16)
        np_bd = _dot_tn(rows2(u0[i], vv[i]), rows2(bh[i], kh[i])) * hm
        np_out[d, ci] = functools.reduce(
            jnp.add, [np_bd[h * HEAD_DIM:(h + 1) * HEAD_DIM] for h in range(4)])
        pc_out[d, ci] = jnp.broadcast_to(e_end[i], (SUBLANES, GROUP_W))


def _rwkv_chunks(feats, cum, tri, hm):
    r = feats[0]
    B, T, G = r.shape
    nch = T // CHUNK
    nsub = 4 if nch % 4 == 0 else 1
    tb = nsub * CHUNK
    tok = pl.BlockSpec((None, tb, G), lambda b, c: (b, c, 0))
    full = lambda a: pl.BlockSpec(a.shape, lambda b, c: (0,) * a.ndim)
    return pl.pallas_call(
        functools.partial(_rwkv_chunk_kernel, nsub=nsub),
        grid=(B, nch // nsub),
        in_specs=[tok] * 9 + [full(cum), full(tri), full(hm)],
        out_specs=[
            pl.BlockSpec((None, 2, tb, G), lambda b, c: (b, 0, c, 0)),
            pl.BlockSpec((None, 2, tb, G), lambda b, c: (b, 0, c, 0)),
            pl.BlockSpec((None, 2, nsub, G, G), lambda b, c: (b, 0, c, 0, 0)),
            pl.BlockSpec((None, 2, nsub, HEAD_DIM, G), lambda b, c: (b, 0, c, 0, 0)),
            pl.BlockSpec((None, 2, nsub, SUBLANES, G), lambda b, c: (b, 0, c, 0, 0)),
        ],
        out_shape=[
            jax.ShapeDtypeStruct((B, 2, T, G), BF16),
            jax.ShapeDtypeStruct((B, 2, T, G), F32),
            jax.ShapeDtypeStruct((B, 2, nch, G, G), BF16),
            jax.ShapeDtypeStruct((B, 2, nch, HEAD_DIM, G), F32),
            jax.ShapeDtypeStruct((B, 2, nch, SUBLANES, G), F32),
        ],
        compiler_params=_cparams(("parallel", "parallel")),
        name="rwkv_chunk_transitions",
    )(*feats, cum, tri, hm)


def _rwkv_scan_kernel(s0_ref, hm_ref, rhf_ref, y1f_ref, mpf_ref, npf_ref, pcf_ref,
                      rhb_ref, y1b_ref, mpb_ref, npb_ref, pcb_ref,
                      yf_out, yb_out, s_out, *, batch, nsub):
    c = pl.program_id(0)

    @pl.when(c == 0)
    def _():
        s_out[...] = s0_ref[...]

    hm_b = hm_ref[...].astype(BF16)
    dirs = ((rhf_ref, y1f_ref, mpf_ref, npf_ref, pcf_ref, yf_out),
            (rhb_ref, y1b_ref, mpb_ref, npb_ref, pcb_ref, yb_out))
    chains = [(b, d) for b in range(batch) for d in range(2)]
    s = [s_out[b, d] for b, d in chains]
    for step in range(nsub):
        s_b = [t.astype(BF16) for t in s]
        s_d = [jnp.concatenate([t, t, t, t], axis=0) * hm_b for t in s_b]
        for i, (b, d) in enumerate(chains):
            rh_ref, y1_ref, mp_ref, np_ref, pc_ref, y_out = dirs[d]
            ci = step if d == 0 else nsub - 1 - step
            rows = slice(ci * CHUNK, (ci + 1) * CHUNK)
            y_out[b, rows, :] = _dot_nt(rh_ref[b, rows, :], s_d[i]) + y1_ref[b, rows, :]
            s[i] = s[i] * pc_ref[b, ci, 0:1, :] + _dot(s_b[i], mp_ref[b, ci]) + np_ref[b, ci]
    for i, (b, d) in enumerate(chains):
        s_out[b, d] = s[i]


def _rwkv_scan(s0, hm, rh, y1, mp, npm, pc):
    B, _, T, G = rh.shape
    nch = T // CHUNK
    nsub = 4 if nch % 4 == 0 else 1
    nblk = nch // nsub
    tb = nsub * CHUNK
    fwd = lambda c: c
    bwd = lambda c: nblk - 1 - c

    def specs(order, d):
        return [
            pl.BlockSpec((B, None, tb, G), lambda c: (0, d, order(c), 0)),
            pl.BlockSpec((B, None, tb, G), lambda c: (0, d, order(c), 0)),
            pl.BlockSpec((B, None, nsub, G, G), lambda c: (0, d, order(c), 0, 0)),
            pl.BlockSpec((B, None, nsub, HEAD_DIM, G), lambda c: (0, d, order(c), 0, 0)),
            pl.BlockSpec((B, None, nsub, SUBLANES, G), lambda c: (0, d, order(c), 0, 0)),
        ]

    return pl.pallas_call(
        functools.partial(_rwkv_scan_kernel, batch=B, nsub=nsub),
        grid=(nblk,),
        in_specs=[pl.BlockSpec(s0.shape, lambda c: (0, 0, 0, 0)), pl.BlockSpec(hm.shape, lambda c: (0, 0))]
        + specs(fwd, 0) + specs(bwd, 1),
        out_specs=[
            pl.BlockSpec((B, tb, G), lambda c: (0, fwd(c), 0)),
            pl.BlockSpec((B, tb, G), lambda c: (0, bwd(c), 0)),
            pl.BlockSpec(s0.shape, lambda c: (0, 0, 0, 0)),
        ],
        out_shape=[
            jax.ShapeDtypeStruct((B, T, G), F32),
            jax.ShapeDtypeStruct((B, T, G), F32),
            jax.ShapeDtypeStruct(s0.shape, F32),
        ],
        compiler_params=_cparams(("arbitrary",)),
        name="rwkv_state_scan",
    )(s0, hm, rh, y1, mp, npm, pc, rh, y1, mp, npm, pc)


def _seg_matrix(width):
    i = np.arange(width) // HEAD_DIM
    return jnp.asarray((i[:, None] == i[None, :]).astype(np.float32), BF16)


def _scan_constants():
    C = CHUNK
    t = np.arange(4 * C)
    head, tok = t // C, t % C
    i = np.arange(C)
    lower = i[:, None] > tok[None, :]
    upper = i[:, None] < tok[None, :]
    eye = i[:, None] == tok[None, :]
    tri = jnp.asarray(np.stack([lower, upper, eye]).astype(np.float32))
    cum = jnp.asarray(np.stack([i[:, None] >= i[None, :], i[:, None] <= i[None, :]]).astype(np.float32), BF16)
    hm = jnp.asarray((head[:, None] == (np.arange(GROUP_W) // HEAD_DIM)[None, :]).astype(np.float32))
    return cum, tri, hm


def _pad_vec(v, lo, total):
    return jnp.zeros((total,), F32).at[lo:lo + v.shape[0]].set(v)


def kernel(x, c, ctx, c_ctx, w_mod, b_mod, norm_ffn1, ffn1_w_gu, ffn1_w_down, norm_mix, w_in, w_out, mla_q_norm, mla_kv_norm, mla_w_uq, mla_w_ukv, mla_qn, mla_kn, rwkv_shift, rwkv_w0, rwkv_w2, rwkv_a0, rwkv_a2, rwkv_g2, rwkv_k_k, rwkv_k_a, rwkv_r_k, rwkv_lnx_g, rwkv_lnx_b, na_qn, na_kn, na_rpb, gqa_qn, gqa_kn, norm_ffn2, ffn2_w_gu, ffn2_w_down):
    B, S, D = x.shape
    Lc = ctx.shape[1]
    depth = w_mod.shape[0]
    assert B + 1 <= SUBLANES and S % 512 == 0 and Lc % CHUNK == 0 and S % GRID_W == 0

    cvec = jnp.zeros((SUBLANES, D), F32).at[:B].set(c).at[B].set(c_ctx)
    mods = _modulation(cvec, w_mod, b_mod).reshape(depth, SUBLANES, 9, D)

    mla_tab = _rope_tables(S // GRID_W, LANES, MLA_NOPE, MLA_ROPE // 2, 1)
    gqa_tab = _rope_tables(S // GRID_W, 4 * HEAD_DIM, 0, HEAD_DIM // 2, 4)
    mla_tab_c = _identity_rope_tables(Lc, LANES)
    gqa_tab_c = _identity_rope_tables(Lc, 4 * HEAD_DIM)

    seg = _seg_matrix(GROUP_W)
    cum, tri, hm = _scan_constants()
    mla_scale = MLA_QK ** -0.5 * LOG2E
    dh_scale = HEAD_DIM ** -0.5 * LOG2E

    w_gu1, w_dn1 = ffn1_w_gu.astype(BF16), ffn1_w_down.astype(BF16)
    w_gu2, w_dn2 = ffn2_w_gu.astype(BF16), ffn2_w_down.astype(BF16)
    w_out_b = w_out.astype(BF16)
    zc = lambda n: jnp.zeros((depth, D, n), BF16)
    wi = w_in.astype(BF16)
    o_b = MLA_COLS
    o_n = o_b + RWKV_COLS
    o_d = o_n + NA_COLS
    w_in_p = jnp.concatenate([
        wi[..., :MLA_Q_RANK + MLA_KV_RANK], zc(64), wi[..., MLA_Q_RANK + MLA_KV_RANK:MLA_COLS], zc(32),
        wi[..., o_b:o_n], zc(UB_W - RWKV_COLS),
        wi[..., o_n:o_d],
        wi[..., o_d:o_d + GQA_COLS]], axis=2)

    h, hc = x, ctx
    for l in range(depth):
        ctx_out = l < depth - 1
        m_lat = mods[l, :B]
        m_ctx = jnp.broadcast_to(mods[l, B], (B, 9, D))


        wuq = mla_w_uq[l].reshape(MLA_Q_RANK, MLA_HEADS, MLA_QK)
        wuq_p = jnp.pad(wuq, ((0, 0), (0, 0), (0, LANES - MLA_QK))).reshape(MLA_Q_RANK, MLA_HEADS * LANES).astype(BF16)
        wukv = mla_w_ukv[l].reshape(MLA_KV_RANK, MLA_HEADS, MLA_NOPE + MLA_V)
        wuk_p = jnp.pad(wukv[..., :MLA_NOPE], ((0, 0), (0, 0), (0, LANES - MLA_NOPE))).reshape(MLA_KV_RANK, MLA_HEADS * LANES).astype(BF16)
        wuv = wukv[..., MLA_NOPE:].reshape(MLA_KV_RANK, MLA_HEADS * MLA_V).astype(BF16)
        mla_small = (mla_q_norm[l].reshape(1, -1), mla_kv_norm[l].reshape(1, -1),
                     _pad_vec(mla_qn[l] * mla_scale, 0, LANES).reshape(1, LANES),
                     _pad_vec(mla_kn[l], 0, LANES).reshape(1, LANES), wuq_p, wuk_p, wuv)

        na_gq = jnp.tile(na_qn[l] * dh_scale, 4).reshape(1, -1)
        na_gk = jnp.tile(na_kn[l], 4).reshape(1, -1)
        gqa_gq = jnp.tile(gqa_qn[l] * dh_scale, 4).reshape(1, -1)
        gqa_gk = jnp.tile(gqa_kn[l], 2).reshape(1, -1)
        na_tb = _na_bias_tables(na_rpb[l])

        mu_p = jnp.zeros((SUBLANES, UB_W), F32).at[:2, :RWKV_COLS].set(rwkv_shift[l])
        lo = 3 * GROUP_W

        def lora_w(w, off):
            return jnp.zeros((GROUP_W, GROUP_W), F32).at[off:off + w.shape[0]].set(w).astype(BF16)

        w2_p = jnp.stack([lora_w(rwkv_w2[l, 0], 0), lora_w(rwkv_w2[l, 1], 32)])
        a2_p = jnp.stack([lora_w(rwkv_a2[l, 0], 64), lora_w(rwkv_a2[l, 1], 96)])
        g2_p = lora_w(rwkv_g2[l], 128)
        feat_vecs = jnp.zeros((SUBLANES, GROUP_W), F32).at[0].set(rwkv_k_k[l]).at[1].set(rwkv_k_a[l]) \
            .at[2].set(rwkv_r_k[l].reshape(-1)).at[3:5].set(rwkv_w0[l]).at[5:7].set(rwkv_a0[l])
        post_vecs = jnp.zeros((SUBLANES, GROUP_W), F32).at[0].set(rwkv_lnx_g[l]).at[1].set(rwkv_lnx_b[l])

        h = _ffn(h, m_lat, norm_ffn1[l], w_gu1, w_dn1, l, 0)
        hc = _ffn(hc, m_ctx, norm_ffn1[l], w_gu1, w_dn1, l, 0)

        ua, ub, un, ud = _inproj(h, m_lat, norm_mix[l], w_in_p, l)
        uca, ucb, ucn, ucd = _inproj(hc, m_ctx, norm_mix[l], w_in_p, l)

        qa, ka, vta = _mla_prep(ua, mla_tab, *mla_small)
        qca, kca, vtca = _mla_prep(uca, mla_tab_c, *mla_small)
        oa = _attention(qa, ka, vta, kca, vtca)

        qn_, kn_, vn_ = _qkv_prep(un, seg, na_gq, na_gk, 4, 4, 4)
        qcn, kcn, vcn, vtcn = _qkv_prep(ucn, seg, na_gq, na_gk, 4, 4, 4, v_cols=True)
        on = _na_attention(qn_, kn_, vn_, kcn, vcn, na_tb)

        qd, kd, vtd = _qkv_prep(ud, seg, gqa_gq, gqa_gk, 4, 2, 2, gqa_tab, v_rows=False, v_cols=True)
        qcd, kcd, vtcd = _qkv_prep(ucd, seg, gqa_gq, gqa_gk, 4, 2, 2, gqa_tab_c, v_rows=False, v_cols=True)
        od = _attention(qd, kd, vtd, kcd, vtcd)

        feats_c = _rwkv_features(ucb, mu_p, seg, w2_p, a2_p, g2_p, feat_vecs)
        feats = _rwkv_features(ub, mu_p, seg, w2_p, a2_p, g2_p, feat_vecs)
        trans_c = _rwkv_chunks(feats_c[:9], cum, tri, hm)
        trans = _rwkv_chunks(feats[:9], cum, tri, hm)
        s0 = jnp.zeros((B, 2, HEAD_DIM, GROUP_W), F32)
        yfc, ybc, s_ctx = _rwkv_scan(s0, hm, *trans_c)
        yf, yb, _ = _rwkv_scan(s_ctx, hm, *trans)

        h = _outproj(h, m_lat, oa, (yf, yb, feats[9], feats[10]), on, od, (True, False, True),
                     seg, post_vecs, w_out_b, l)
        if ctx_out:
            oca = _attention(qca, kca, vtca)
            ocn = _attention(qcn, kcn, vtcn)
            ocd = _attention(qcd, kcd, vtcd)
            hc = _outproj(hc, m_ctx, oca, (yfc, ybc, feats_c[9], feats_c[10]), ocn, ocd,
                          (True, True, True), seg, post_vecs, w_out_b, l)

        h = _ffn(h, m_lat, norm_ffn2[l], w_gu2, w_dn2, l, 6)
        if ctx_out:
            hc = _ffn(hc, m_ctx, norm_ffn2[l], w_gu2, w_dn2, l, 6)
    return h
```

```python
import functools

import numpy as np
import jax
import jax.numpy as jnp
from jax import lax
from jax.experimental import pallas as pl
from jax.experimental.pallas import tpu as pltpu

F32 = jnp.float32
BF16 = jnp.bfloat16

GRID_W = 64
HEAD_DIM = 64
GROUP_W = 256
ROPE_THETA = 10000.0
NORM_EPS = 1e-6
MLA_HEADS = 4
MLA_Q_RANK = 256
MLA_KV_RANK = 128
MLA_NOPE = 64
MLA_ROPE = 32
MLA_V = 64
MLA_QK = MLA_NOPE + MLA_ROPE
RWKV_COLS = 960
RWKV_LNX_EPS = 64e-5
NA_WIN_R = 8
NA_WIN_C = 16
MLA_COLS = 416
NA_COLS = 768
GQA_COLS = 512

LANES = 128
SUBLANES = 8
VMEM_LIMIT_BYTES = 56 * 1024 * 1024

UA_W = 512
UB_W = 1024
UN_W = 768
UD_W = 512

VT_ROWS = 128
CHUNK = 64
NEG_INF = -1e30
LOG2E = 1.4426950408889634
DECAY_SCALE = 0.6065306597126334
NA_ROWS_PER_ITER = 8
MAX_JUMP = 24.0


def _dot(a, b):
    return jnp.dot(a, b, preferred_element_type=F32)


def _dot_nt(a, b):
    return lax.dot_general(a, b, (((1,), (1,)), ((), ())), preferred_element_type=F32)


def _dot_tn(a, b):
    return lax.dot_general(a, b, (((0,), (0,)), ((), ())), preferred_element_type=F32)


def _split2(x):
    hi = x.astype(BF16)
    lo = (x - hi.astype(F32)).astype(BF16)
    return hi, lo


def _dot_exact_rhs(x, m):
    h1 = x.astype(BF16)
    r1 = x - h1.astype(F32)
    h2 = r1.astype(BF16)
    h3 = (r1 - h2.astype(F32)).astype(BF16)
    return _dot(h1, m) + _dot(h2, m) + _dot(h3, m)


def _dot_exact_lhs(m, x):
    h1 = x.astype(BF16)
    r1 = x - h1.astype(F32)
    h2 = r1.astype(BF16)
    h3 = (r1 - h2.astype(F32)).astype(BF16)
    return _dot(m, h1) + _dot(m, h2) + _dot(m, h3)


def _rms(x, eps=NORM_EPS):
    return x * lax.rsqrt(jnp.mean(x * x, axis=-1, keepdims=True) + eps)


def _cparams(sem):
    return pltpu.CompilerParams(dimension_semantics=sem, vmem_limit_bytes=VMEM_LIMIT_BYTES)


def _mod_kernel(ct_ref, w_ref, b_ref, o_ref, *, n_rows):
    ct = ct_ref[...]
    st = ct * jax.nn.sigmoid(ct)
    w = w_ref[...]
    rows = [jnp.sum(w * st[:, r:r + 1], axis=0, keepdims=True) for r in range(n_rows)]
    rows.append(jnp.zeros((SUBLANES - n_rows, w.shape[1]), F32))
    o_ref[...] = jnp.concatenate(rows, axis=0) + b_ref[...]


def _modulation(cvec, n_rows, w_mod, b_mod):
    L, D, N = w_mod.shape
    tn = 1024
    return pl.pallas_call(
        functools.partial(_mod_kernel, n_rows=n_rows),
        grid=(L, N // tn),
        in_specs=[
            pl.BlockSpec((D, SUBLANES), lambda l, n: (0, 0)),
            pl.BlockSpec((None, D, tn), lambda l, n: (l, 0, n)),
            pl.BlockSpec((None, 1, tn), lambda l, n: (l, 0, n)),
        ],
        out_specs=pl.BlockSpec((None, SUBLANES, tn), lambda l, n: (l, 0, n)),
        out_shape=jax.ShapeDtypeStruct((L, SUBLANES, N), F32),
        compiler_params=_cparams(("parallel", "parallel")),
        name="adaln_mod",
    )(cvec.T, w_mod, b_mod.reshape(L, 1, N))


def _ffn_kernel(h_ref, mod_ref, g_ref, wg_ref, wu_ref, wd_ref, o_ref, *, base):
    x = h_ref[...]
    y = _rms(x) * g_ref[...]
    xn = (y * (1.0 + mod_ref[base + 1:base + 2, :]) + mod_ref[base:base + 1, :]).astype(BF16)
    g = _dot(xn, wg_ref[...])
    u = _dot(xn, wu_ref[...])
    a = (g * jax.nn.sigmoid(g) * u).astype(BF16)
    o_ref[...] = x + 0.5 * mod_ref[base + 2:base + 3, :] * _dot(a, wd_ref[...])


def _ffn(h, mod, gain, w_gu, w_down, layer, base):
    B, T, D = h.shape
    F = w_down.shape[1]
    tm = min(T, 512)
    resident = pl.Buffered(1)
    return pl.pallas_call(
        functools.partial(_ffn_kernel, base=base),
        grid=(B, T // tm),
        in_specs=[
            pl.BlockSpec((None, tm, D), lambda b, i: (b, i, 0)),
            pl.BlockSpec((None, 9, D), lambda b, i: (b, 0, 0)),
            pl.BlockSpec((1, D), lambda b, i: (0, 0)),
            pl.BlockSpec((None, D, F), lambda b, i: (layer, 0, 0), pipeline_mode=resident),
            pl.BlockSpec((None, D, F), lambda b, i: (layer, 0, 1), pipeline_mode=resident),
            pl.BlockSpec((None, F, D), lambda b, i: (layer, 0, 0), pipeline_mode=resident),
        ],
        out_specs=pl.BlockSpec((None, tm, D), lambda b, i: (b, i, 0)),
        out_shape=jax.ShapeDtypeStruct((B, T, D), F32),
        compiler_params=_cparams(("parallel", "parallel")),
        name="swiglu_halfstep",
    )(h, mod, gain.reshape(1, D), w_gu, w_gu, w_down)


def _rwkv_out(yf_ref, yb_ref, bon_ref, g_ref, seg_ref, vec_ref):
    y = yf_ref[...] + yb_ref[...]
    seg = seg_ref[...]

    def segmean(t):
        hi, lo = _split2(t)
        return (_dot(hi, seg) + _dot(lo, seg)) * (1.0 / HEAD_DIM)

    mu = segmean(y)
    yc = y - mu
    var = segmean(yc * yc)
    yn = yc * lax.rsqrt(var + RWKV_LNX_EPS) * vec_ref[0:1, :] + vec_ref[1:2, :]
    return ((yn + bon_ref[...]) * g_ref[...]).astype(BF16)


def _outproj_kernel(h_ref, mod_ref, oa_ref, yf_ref, yb_ref, bon_ref, g_ref, on_ref, od_ref,
                    seg_ref, vec_ref, w_ref, o_ref, *, transposed):
    ob = _rwkv_out(yf_ref, yb_ref, bon_ref, g_ref, seg_ref, vec_ref)
    groups = ((oa_ref, transposed[0]), (None, False), (on_ref, transposed[1]), (od_ref, transposed[2]))
    acc = None
    for g, (ref, t) in enumerate(groups):
        w = w_ref[g * GROUP_W:(g + 1) * GROUP_W, :]
        x = ob if ref is None else ref[...]
        part = _dot_tn(x, w) if t else _dot(x, w)
        acc = part if acc is None else acc + part
    o_ref[...] = h_ref[...] + mod_ref[5:6, :] * acc


def _outproj(h, mod, oa, rwkv, on, od, transposed, seg, post_vecs, w_out, layer):
    B, T, D = h.shape
    tm = min(T, 512)
    grp = pl.BlockSpec((None, tm, GROUP_W), lambda b, i: (b, i, 0))
    grp_t = pl.BlockSpec((None, GROUP_W, tm), lambda b, i: (b, 0, i))
    pick = lambda t: grp_t if t else grp
    full = lambda a: pl.BlockSpec(a.shape, lambda b, i: (0,) * a.ndim)
    return pl.pallas_call(
        functools.partial(_outproj_kernel, transposed=tuple(transposed)),
        grid=(B, T // tm),
        in_specs=[
            pl.BlockSpec((None, tm, D), lambda b, i: (b, i, 0)),
            pl.BlockSpec((None, 9, D), lambda b, i: (b, 0, 0)),
            pick(transposed[0]), grp, grp, grp, grp, pick(transposed[1]), pick(transposed[2]),
            full(seg), full(post_vecs),
            pl.BlockSpec((None, 4 * GROUP_W, D), lambda b, i: (layer, 0, 0)),
        ],
        out_specs=pl.BlockSpec((None, tm, D), lambda b, i: (b, i, 0)),
        out_shape=jax.ShapeDtypeStruct((B, T, D), F32),
        compiler_params=_cparams(("parallel", "parallel")),
        name="out_projection",
    )(h, mod, oa, *rwkv, on, od, seg, post_vecs, w_out)


def _rope_tables(n_rows, width, lead, group, n_rep):
    half = group // 2
    lane = np.arange(width)
    rel = (lane - lead) % (2 * group)
    inside = (lane >= lead) & (lane < lead + n_rep * 2 * group)
    is_row = (rel < group)[None, None, :]
    second = ((rel % group >= half) & inside)[None, None, :]
    first = ((rel % group < half) & inside)[None, None, :]
    inv_freq = ROPE_THETA ** (-jnp.arange(half, dtype=F32) / half)
    freq = jnp.where(inside, inv_freq[rel % half], 0.0)[None, :]
    ang_r = jnp.arange(n_rows).astype(F32)[:, None] * freq
    ang_c = jnp.arange(GRID_W).astype(F32)[:, None] * freq
    shape = (n_rows, GRID_W, width)
    c = jnp.where(is_row, jnp.cos(ang_r)[:, None, :], jnp.cos(ang_c)[None, :, :])
    s = jnp.where(is_row, jnp.sin(ang_r)[:, None, :], jnp.sin(ang_c)[None, :, :])
    flat = lambda t: jnp.broadcast_to(t, shape).reshape(n_rows * GRID_W, width)
    return [flat(c), flat(jnp.where(second, s, 0.0)), flat(jnp.where(first, -s, 0.0))]


def _identity_rope_tables(n_tokens, width):
    return [jnp.ones((n_tokens, width), F32), jnp.zeros((n_tokens, width), F32),
            jnp.zeros((n_tokens, width), F32)]


def _vt_ext(v):
    tm = v.shape[0]
    lane = lax.broadcasted_iota(jnp.int32, (tm, VT_ROWS - HEAD_DIM), 1)
    aux = jnp.where(lane == 0, 1.0, 0.0)
    return jnp.concatenate([v, aux], axis=-1).T.astype(BF16)


def _mla_prep_math(ua, c_ref, s1_ref, s2_ref, qnorm_ref, kvnorm_ref, qn_ref, kn_ref,
                   wuq_ref, wuk_ref, wuv_ref, q_out, k_out, vt_out):
    qc = (_rms(ua[:, :MLA_Q_RANK]) * qnorm_ref[...]).astype(BF16)
    kvc = (_rms(ua[:, MLA_Q_RANK:MLA_Q_RANK + MLA_KV_RANK]) * kvnorm_ref[...]).astype(BF16)
    k_rope = ua[:, 3 * LANES:4 * LANES]
    q_all = _dot(qc, wuq_ref[...])
    k_all = _dot(kvc, wuk_ref[...])
    v_all = _dot(kvc, wuv_ref[...])
    cos, s_dn, s_up = c_ref[...], s1_ref[...], s2_ref[...]
    half = MLA_ROPE // 4

    def rope(x):
        return x * cos + pltpu.roll(x, half, 1) * s_dn + pltpu.roll(x, LANES - half, 1) * s_up

    def headnorm(x, gain):
        ms = jnp.sum(x * x, axis=-1, keepdims=True) * (1.0 / MLA_QK)
        return x * lax.rsqrt(ms + NORM_EPS) * gain

    for h in range(MLA_HEADS):
        qh = headnorm(q_all[:, h * LANES:(h + 1) * LANES], qn_ref[...])
        q_out[h] = rope(qh).astype(BF16)
        kh = headnorm(k_all[:, h * LANES:(h + 1) * LANES] + k_rope, kn_ref[...])
        k_out[h] = rope(kh).astype(BF16)
        vt_out[h] = _vt_ext(v_all[:, h * MLA_V:(h + 1) * MLA_V])


def _qkv_prep_math(x, seg_ref, gq_ref, gk_ref, tabs, q_out, k_out, v_out, vt_out, *, nq, nk, nv):
    use_rope = tabs is not None
    wq, wk, wv = nq * HEAD_DIM, nk * HEAD_DIM, nv * HEAD_DIM
    q = x[:, 0:wq]
    k = x[:, wq:wq + wk]
    v = x[:, wq + wk:wq + wk + wv]

    def headnorm(t, gain, w):
        hi, lo = _split2(t * t)
        seg = seg_ref[0:w, 0:w]
        ms = (_dot(hi, seg) + _dot(lo, seg)) * (1.0 / HEAD_DIM)
        return t * lax.rsqrt(ms + NORM_EPS) * gain

    q = headnorm(q, gq_ref[...], wq)
    k = headnorm(k, gk_ref[...], wk)
    if use_rope:
        c_ref, s1_ref, s2_ref = tabs
        half = HEAD_DIM // 4

        def rope(t, w):
            return (t * c_ref[:, 0:w] + pltpu.roll(t, half, 1) * s1_ref[:, 0:w]
                    + pltpu.roll(t, w - half, 1) * s2_ref[:, 0:w])

        q = rope(q, wq)
        k = rope(k, wk)
    for h in range(nq):
        q_out[h] = q[:, h * HEAD_DIM:(h + 1) * HEAD_DIM].astype(BF16)
    for h in range(nk):
        k_out[h] = k[:, h * HEAD_DIM:(h + 1) * HEAD_DIM].astype(BF16)
    for h in range(nv):
        vh = v[:, h * HEAD_DIM:(h + 1) * HEAD_DIM]
        if v_out is not None:
            v_out[h] = vh.astype(BF16)
        if vt_out is not None:
            vt_out[h] = _vt_ext(vh)


def _inproj_kernel(h_ref, mod_ref, g_ref, w_ref, oa_ref, ob_ref, on_ref, od_ref):
    y = _rms(h_ref[...]) * g_ref[...]
    xm = (y * (1.0 + mod_ref[4:5, :]) + mod_ref[3:4, :]).astype(BF16)
    o = 0
    for ref, w in ((oa_ref, UA_W), (ob_ref, UB_W), (on_ref, UN_W), (od_ref, UD_W)):
        ref[...] = _dot(xm, w_ref[:, o:o + w])
        o += w


def _inproj(h, mod, gain, w_in_p, layer):
    B, T, D = h.shape
    tm = min(T, 512)
    W = w_in_p.shape[2]
    widths = (UA_W, UB_W, UN_W, UD_W)
    return pl.pallas_call(
        _inproj_kernel,
        grid=(B, T // tm),
        in_specs=[
            pl.BlockSpec((None, tm, D), lambda b, i: (b, i, 0)),
            pl.BlockSpec((None, 9, D), lambda b, i: (b, 0, 0)),
            pl.BlockSpec((1, D), lambda b, i: (0, 0)),
            pl.BlockSpec((None, D, W), lambda b, i: (layer, 0, 0)),
        ],
        out_specs=[pl.BlockSpec((None, tm, w), lambda b, i: (b, i, 0)) for w in widths],
        out_shape=[jax.ShapeDtypeStruct((B, T, w), F32) for w in widths],
        compiler_params=_cparams(("parallel", "parallel")),
        name="in_projection",
    )(h, mod, gain.reshape(1, D), w_in_p)


def _mla_prep_kernel(ua_ref, *refs):
    _mla_prep_math(ua_ref[...], *refs)


def _mla_prep(ua, tables, q_norm, kv_norm, qn_p, kn_p, wuq_p, wuk_p, wuv):
    B, T, _ = ua.shape
    tm = min(T, 512)
    tab = pl.BlockSpec((tm, LANES), lambda b, i: (i, 0))
    full = lambda a: pl.BlockSpec(a.shape, lambda b, i: (0,) * a.ndim)
    small = [q_norm, kv_norm, qn_p, kn_p, wuq_p, wuk_p, wuv]
    return pl.pallas_call(
        _mla_prep_kernel,
        grid=(B, T // tm),
        in_specs=[pl.BlockSpec((None, tm, UA_W), lambda b, i: (b, i, 0)), tab, tab, tab]
        + [full(a) for a in small],
        out_specs=[
            pl.BlockSpec((None, MLA_HEADS, tm, LANES), lambda b, i: (b, 0, i, 0)),
            pl.BlockSpec((None, MLA_HEADS, tm, LANES), lambda b, i: (b, 0, i, 0)),
            pl.BlockSpec((None, MLA_HEADS, VT_ROWS, tm), lambda b, i: (b, 0, 0, i)),
        ],
        out_shape=[
            jax.ShapeDtypeStruct((B, MLA_HEADS, T, LANES), BF16),
            jax.ShapeDtypeStruct((B, MLA_HEADS, T, LANES), BF16),
            jax.ShapeDtypeStruct((B, MLA_HEADS, VT_ROWS, T), BF16),
        ],
        compiler_params=_cparams(("parallel", "parallel")),
        name="mla_prep",
    )(ua, *tables, *small)


def _qkv_prep_kernel(*refs, nq, nk, nv, use_rope, v_rows, v_cols):
    n_in = 7 if use_rope else 4
    x_ref, seg_ref, gq_ref, gk_ref = refs[:4]
    tabs = refs[4:7] if use_rope else None
    q_out, k_out = refs[n_in:n_in + 2]
    v_outs = list(refs[n_in + 2:])
    v_out = v_outs.pop(0) if v_rows else None
    vt_out = v_outs.pop(0) if v_cols else None
    _qkv_prep_math(x_ref[...], seg_ref, gq_ref, gk_ref, tabs, q_out, k_out, v_out, vt_out,
                   nq=nq, nk=nk, nv=nv)


def _qkv_prep(x, seg, gq, gk, nq, nk, nv, tables=None, v_rows=True, v_cols=False):
    B, T, W = x.shape
    tm = min(T, 512)
    use_rope = tables is not None
    full = lambda a: pl.BlockSpec(a.shape, lambda b, i: (0,) * a.ndim)
    in_specs = [pl.BlockSpec((None, tm, W), lambda b, i: (b, i, 0)), full(seg), full(gq), full(gk)]
    args = [x, seg, gq, gk]
    if use_rope:
        in_specs += [pl.BlockSpec((tm, tables[0].shape[1]), lambda b, i: (i, 0))] * 3
        args += list(tables)
    hm = lambda n: pl.BlockSpec((None, n, tm, HEAD_DIM), lambda b, i: (b, 0, i, 0))
    out_specs = [hm(nq), hm(nk)]
    out_shape = [jax.ShapeDtypeStruct((B, n, T, HEAD_DIM), BF16) for n in (nq, nk)]
    if v_rows:
        out_specs.append(hm(nv))
        out_shape.append(jax.ShapeDtypeStruct((B, nv, T, HEAD_DIM), BF16))
    if v_cols:
        out_specs.append(pl.BlockSpec((None, nv, VT_ROWS, tm), lambda b, i: (b, 0, 0, i)))
        out_shape.append(jax.ShapeDtypeStruct((B, nv, VT_ROWS, T), BF16))
    return pl.pallas_call(
        functools.partial(_qkv_prep_kernel, nq=nq, nk=nk, nv=nv, use_rope=use_rope,
                          v_rows=v_rows, v_cols=v_cols),
        grid=(B, T // tm),
        in_specs=in_specs,
        out_specs=out_specs,
        out_shape=out_shape,
        compiler_params=_cparams(("parallel", "parallel")),
        name="qkv_prep_rope" if use_rope else "qkv_prep",
    )(*args)


def _attn_kernel(*refs, n_heads, group, has_extra, nk):
    if has_extra:
        q_ref, k_ref, vt_ref, kx_ref, vtx_ref, o_ref, m_ref, acc_ref = refs
    else:
        q_ref, k_ref, vt_ref, o_ref, m_ref, acc_ref = refs
    j = pl.program_id(2)

    heads = range(n_heads)

    def first(kr, vr):
        s = [_dot_nt(kr[h // group], q_ref[h]) for h in heads]
        m = [jnp.max(s[h], axis=0, keepdims=True) for h in heads]
        p = [jnp.exp2(s[h] - m[h]).astype(BF16) for h in heads]
        for h in heads:
            m_ref[h] = m[h]
            acc_ref[h] = _dot(vr[h // group], p[h])

    def update(kr, vr):
        m_prev = [m_ref[h] for h in heads]
        s = [_dot_nt(kr[h // group], q_ref[h]) for h in heads]
        pv = [_dot(vr[h // group], jnp.exp2(s[h] - m_prev[h]).astype(BF16)) for h in heads]
        m_blk = [jnp.max(s[h], axis=0, keepdims=True) for h in heads]
        jump = functools.reduce(jnp.maximum, [m_blk[h] - m_prev[h] for h in heads])
        safe = jnp.max(jump) <= MAX_JUMP

        @pl.when(safe)
        def _():
            for h in heads:
                m_new = jnp.maximum(m_prev[h], m_blk[h])
                m_ref[h] = m_new
                acc_ref[h] = (acc_ref[h] + pv[h]) * jnp.exp2(m_prev[h] - m_new)

        @pl.when(jnp.logical_not(safe))
        def _():
            s2 = [_dot_nt(kr[h // group], q_ref[h]) for h in heads]
            m_new = [jnp.maximum(m_ref[h], jnp.max(s2[h], axis=0, keepdims=True)) for h in heads]
            p2 = [jnp.exp2(s2[h] - m_new[h]).astype(BF16) for h in heads]
            for h in heads:
                acc_ref[h] = jnp.exp2(m_ref[h] - m_new[h]) * acc_ref[h] + _dot(vr[h // group], p2[h])
                m_ref[h] = m_new[h]

    if has_extra:
        @pl.when(j == 0)
        def _():
            first(kx_ref, vtx_ref)

        update(k_ref, vt_ref)
    else:
        @pl.when(j == 0)
        def _():
            first(k_ref, vt_ref)

        @pl.when(j > 0)
        def _():
            update(k_ref, vt_ref)

    @pl.when(j == nk - 1)
    def _():
        outs = []
        for h in range(n_heads):
            a = acc_ref[h]
            outs.append(a[:HEAD_DIM] / a[HEAD_DIM:HEAD_DIM + 1])
        o_ref[...] = jnp.concatenate(outs, axis=0).astype(o_ref.dtype)


def _attention(q, k, vt, kx=None, vtx=None):
    B, H, Tq, dq = q.shape
    Hk, Tk = k.shape[1], k.shape[2]
    group = H // Hk
    tq = min(Tq, 512)
    tk = min(Tk, 2048)
    nk = Tk // tk
    has_extra = kx is not None
    in_specs = [
        pl.BlockSpec((None, H, tq, dq), lambda b, i, j: (b, 0, i, 0)),
        pl.BlockSpec((None, Hk, tk, dq), lambda b, i, j: (b, 0, j, 0)),
        pl.BlockSpec((None, Hk, VT_ROWS, tk), lambda b, i, j: (b, 0, 0, j)),
    ]
    args = [q, k, vt]
    if has_extra:
        Tx = kx.shape[2]
        in_specs += [pl.BlockSpec((None, Hk, Tx, dq), lambda b, i, j: (b, 0, 0, 0)),
                     pl.BlockSpec((None, Hk, VT_ROWS, Tx), lambda b, i, j: (b, 0, 0, 0))]
        args += [kx, vtx]
    return pl.pallas_call(
        functools.partial(_attn_kernel, n_heads=H, group=group, has_extra=has_extra, nk=nk),
        grid=(B, Tq // tq, nk),
        in_specs=in_specs,
        out_specs=pl.BlockSpec((None, H * HEAD_DIM, tq), lambda b, i, j: (b, 0, i)),
        out_shape=jax.ShapeDtypeStruct((B, H * HEAD_DIM, Tq), BF16),
        scratch_shapes=[pltpu.VMEM((H, 1, tq), F32), pltpu.VMEM((H, VT_ROWS, tq), F32)],
        compiler_params=_cparams(("parallel", "parallel", "arbitrary")),
        name="flash_attention_ctx" if has_extra else "flash_attention",
    )(*args)


def _na_bias_kernel(r_ref, e_ref, m_ref, o_ref):
    o_ref[...] = _dot_exact_rhs(r_ref[...], e_ref[...]) + m_ref[...]


def _na_bias_tables(rpb):
    H, n_dr, n_dc = rpb.shape
    W = GRID_W
    qc = np.arange(W)[:, None]
    kc = np.arange(W)[None, :]
    c0 = np.clip(qc - NA_WIN_C // 2, 0, W - NA_WIN_C)
    valid = (kc >= c0) & (kc < c0 + NA_WIN_C)
    d = kc - qc + NA_WIN_C - 1
    onehot = np.zeros((LANES, W * W), np.float32)
    for a in range(W):
        for b in range(W):
            if valid[a, b]:
                onehot[d[a, b], a * W + b] = 1.0
    mask = np.where(valid, 0.0, NEG_INF).astype(np.float32).reshape(1, W * W)
    rows = H * n_dr
    rows_p = -(-rows // SUBLANES) * SUBLANES
    r2 = jnp.zeros((rows_p, LANES), F32).at[:rows, :n_dc].set(rpb.reshape(rows, n_dc) * LOG2E)
    tn = 1024
    tiles = pl.pallas_call(
        _na_bias_kernel,
        grid=(W * W // tn,),
        in_specs=[pl.BlockSpec((rows_p, LANES), lambda n: (0, 0)),
                  pl.BlockSpec((LANES, tn), lambda n: (0, n)),
                  pl.BlockSpec((1, tn), lambda n: (0, n))],
        out_specs=pl.BlockSpec((rows_p, tn), lambda n: (0, n)),
        out_shape=jax.ShapeDtypeStruct((rows_p, W * W), F32),
        compiler_params=_cparams(("parallel",)),
        name="na_bias_expand",
    )(r2, jnp.asarray(onehot, BF16), jnp.asarray(mask))
    tiles = tiles[:rows].reshape(H, n_dr, W, W)
    slabs = [tiles[:, base:base + NA_WIN_R].transpose(0, 2, 1, 3).reshape(H, W, NA_WIN_R * W)
             for base in range(NA_WIN_R)]
    return jnp.stack(slabs, 0)


def _na_kernel(q_ref, k_ref, v_ref, kc_ref, vc_ref, tb_ref, o_ref, *, rb, rows, n_heads):
    blk = pl.program_id(1)
    W = GRID_W
    win = NA_WIN_R * W

    nr = NA_ROWS_PER_ITER

    def rows_body(i, carry):
        q0 = pl.multiple_of(i * nr * W, nr * W)
        heads = range(n_heads)
        units = [(j, h) for j in range(nr) for h in heads]
        k0, base = [], []
        for j in range(nr):
            r = blk * rb + nr * i + j
            r0 = jnp.clip(r - NA_WIN_R // 2, 0, rows - NA_WIN_R)
            base.append(r0 - r + NA_WIN_R - 1)
            k0.append(pl.multiple_of(r0 * W, W))
        qs = [q_ref[h, pl.ds(q0, nr * W), :] for h in heads]
        s_cxs = [_dot_nt(qs[h], kc_ref[h]) for h in heads]
        s_nb = [_dot_nt(qs[h][j * W:(j + 1) * W], k_ref[h, pl.ds(k0[j], win), :]) + tb_ref[base[j], h]
                for j, h in units]
        s_cx = [s_cxs[h][j * W:(j + 1) * W] for j, h in units]
        n = len(units)
        m = [jnp.maximum(jnp.max(s_nb[u], axis=-1, keepdims=True),
                         jnp.max(s_cx[u], axis=-1, keepdims=True)) for u in range(n)]
        p_nb = [jnp.exp2(s_nb[u] - m[u]) for u in range(n)]
        p_cx = [jnp.exp2(s_cx[u] - m[u]) for u in range(n)]
        l = [jnp.sum(p_nb[u], axis=-1, keepdims=True) + jnp.sum(p_cx[u], axis=-1, keepdims=True)
             for u in range(n)]
        o_cxs = [_dot(jnp.concatenate([p_cx[j * n_heads + h] for j in range(nr)], axis=0).astype(BF16),
                      vc_ref[h]) for h in heads]
        o = [_dot(p_nb[u].astype(BF16), v_ref[h, pl.ds(k0[j], win), :]) + o_cxs[h][j * W:(j + 1) * W]
             for u, (j, h) in enumerate(units)]
        o_ref[pl.ds(q0, nr * W), :] = jnp.concatenate(
            [jnp.concatenate([o[j * n_heads + h] / l[j * n_heads + h] for h in heads], axis=-1)
             for j in range(nr)], axis=0).astype(o_ref.dtype)
        return carry

    lax.fori_loop(0, rb // nr, rows_body, 0)


def _na_attention(q, k, v, kc, vc, tb):
    B, H, S, dh = q.shape
    L = kc.shape[2]
    rows = S // GRID_W
    assert rows >= NA_WIN_R
    rb = 8
    return pl.pallas_call(
        functools.partial(_na_kernel, rb=rb, rows=rows, n_heads=H),
        grid=(B, rows // rb),
        in_specs=[
            pl.BlockSpec((None, H, rb * GRID_W, dh), lambda b, i: (b, 0, i, 0)),
            pl.BlockSpec((None, H, S, dh), lambda b, i: (b, 0, 0, 0)),
            pl.BlockSpec((None, H, S, dh), lambda b, i: (b, 0, 0, 0)),
            pl.BlockSpec((None, H, L, dh), lambda b, i: (b, 0, 0, 0)),
            pl.BlockSpec((None, H, L, dh), lambda b, i: (b, 0, 0, 0)),
            pl.BlockSpec(tb.shape, lambda b, i: (0, 0, 0, 0)),
        ],
        out_specs=pl.BlockSpec((None, rb * GRID_W, H * dh), lambda b, i: (b, i, 0)),
        out_shape=jax.ShapeDtypeStruct((B, S, H * dh), BF16),
        compiler_params=_cparams(("parallel", "arbitrary")),
        name="neighbourhood_attention",
    )(q, k, v, kc, vc, tb)


def _rwkv_feat_kernel(x_ref, xp_ref, xn_ref, mu_ref, seg_ref, w2_ref, a2_ref, g2_ref, vec_ref,
                      r_out, v_out, kk_out, lwf_out, lwb_out, kdf_out, kdb_out, af_out, ab_out,
                      bon_out, g_out, *, nblk):
    i = pl.program_id(1)
    x = x_ref[...]
    tm = x.shape[0]
    has_prev = jnp.where(i > 0, 1.0, 0.0)
    has_next = jnp.where(i < nblk - 1, 1.0, 0.0)
    row = lax.broadcasted_iota(jnp.int32, (tm, 1), 0)
    prev = jnp.where(row == 0, xp_ref[SUBLANES - 1:SUBLANES, :] * has_prev, pltpu.roll(x, 1, 0))
    nxt = jnp.where(row == tm - 1, xn_ref[0:1, :] * has_next, pltpu.roll(x, tm - 1, 0))
    xs = x + mu_ref[0:1, :] * (prev - x) + mu_ref[1:2, :] * (nxt - x)

    G = GROUP_W
    r, k, v = xs[:, 0:G], xs[:, G:2 * G], xs[:, 2 * G:3 * G]
    lora = xs[:, 3 * G:4 * G]
    seg = seg_ref[...]

    def segsum(t):
        hi, lo = _split2(t)
        return _dot(hi, seg) + _dot(lo, seg)

    k_k, k_a, r_k = vec_ref[0:1, :], vec_ref[1:2, :], vec_ref[2:3, :]
    kk = k * k_k
    kk = kk * lax.rsqrt(segsum(kk * kk) + 1e-12)
    g_out[...] = _dot(jax.nn.sigmoid(lora).astype(BF16), g2_ref[...])
    tanh_l = jnp.tanh(lora).astype(BF16)
    lora_b = lora.astype(BF16)
    kd_sum = None
    for d, (lw_out, kd_out, a_out) in enumerate(((lwf_out, kdf_out, af_out), (lwb_out, kdb_out, ab_out))):
        w0, a0 = vec_ref[3 + d:4 + d, :], vec_ref[5 + d:6 + d, :]
        z = w0 + _dot(tanh_l, w2_ref[d])
        lw_out[...] = -DECAY_SCALE * jax.nn.sigmoid(z)
        a = jax.nn.sigmoid(a0 + _dot(lora_b, a2_ref[d]))
        kd = k * (1.0 + (a - 1.0) * k_a)
        kd_out[...] = kd
        a_out[...] = a
        kd_sum = kd if kd_sum is None else kd_sum + kd
    r_out[...] = r
    v_out[...] = v
    kk_out[...] = kk
    bon_out[...] = segsum(r * kd_sum * r_k) * v


def _rwkv_features(ub, mu_p, seg, w2_p, a2_p, g2_p, vecs):
    B, T, W = ub.shape
    tm = min(T, 256)
    nblk = T // tm
    hb = tm // SUBLANES
    nh = T // SUBLANES
    full = lambda a: pl.BlockSpec(a.shape, lambda b, i: (0,) * a.ndim)
    out = pl.BlockSpec((None, tm, GROUP_W), lambda b, i: (b, i, 0))
    return pl.pallas_call(
        functools.partial(_rwkv_feat_kernel, nblk=nblk),
        grid=(B, nblk),
        in_specs=[
            pl.BlockSpec((None, tm, W), lambda b, i: (b, i, 0)),
            pl.BlockSpec((None, SUBLANES, W), lambda b, i: (b, jnp.maximum(i * hb - 1, 0), 0)),
            pl.BlockSpec((None, SUBLANES, W), lambda b, i: (b, jnp.minimum((i + 1) * hb, nh - 1), 0)),
            full(mu_p), full(seg), full(w2_p), full(a2_p), full(g2_p), full(vecs),
        ],
        out_specs=[out] * 11,
        out_shape=[jax.ShapeDtypeStruct((B, T, GROUP_W), F32)] * 11,
        compiler_params=_cparams(("parallel", "parallel")),
        name="rwkv_features",
    )(ub, ub, ub, mu_p, seg, w2_p, a2_p, g2_p, vecs)


def _rwkv_chunk_kernel(r_ref, v_ref, kk_ref, lwf_ref, lwb_ref, kdf_ref, kdb_ref, af_ref, ab_ref,
                       cum_ref, tri_ref, hm_ref,
                       rh_out, y1_out, mp_out, np_out, pc_out, *, nsub):
    C = CHUNK
    hm = hm_ref[...]
    hm_b = hm.astype(BF16)
    eye_w = tri_ref[2]
    lw_refs, kd_refs, a_refs = (lwf_ref, lwb_ref), (kdf_ref, kdb_ref), (af_ref, ab_ref)

    def diag(t):
        tb = t.astype(BF16)
        return jnp.concatenate([tb, tb, tb, tb], axis=0) * hm_b

    def rows2(a, b):
        return jnp.concatenate([a, b], axis=0).astype(BF16)

    chains = [(ci, d) for ci in range(nsub) for d in range(2)]
    n = len(chains)
    rng = range(n)
    rows = [slice(ci * C, (ci + 1) * C) for ci, _ in chains]

    at, rt, vv, bh, kh, b_d, k_d, e_end = ([None] * n for _ in range(8))
    for i, (ci, d) in enumerate(chains):
        rs = rows[i]
        r, kk = r_ref[rs, :], kk_ref[rs, :]
        lw, kd, a = lw_refs[d][rs, :], kd_refs[d][rs, :], a_refs[d][rs, :]
        cl = _dot_exact_lhs(cum_ref[d], lw)
        last = C - 1 if d == 0 else 0
        e_end[i] = jnp.exp(cl[last:last + 1, :])
        e_neg = jnp.exp(-cl)
        bt = kk * a * e_neg
        kt = kd * e_neg
        vv[i] = v_ref[rs, :]
        at[i] = -kk * jnp.exp(cl - lw)
        rt[i] = r * jnp.exp(cl)
        bh[i], kh[i] = bt * e_end[i], kt * e_end[i]
        b_d[i], k_d[i] = diag(bt), diag(kt)
    a_d = [diag(at[i]) for i in rng]
    v_d = [diag(vv[i]) for i in rng]

    strict = [tri_ref[d] for _, d in chains]
    ar = [rows2(at[i], rt[i]) for i in rng]
    gb = [_dot_nt(ar[i], b_d[i]) for i in rng]
    gk = [_dot_nt(ar[i], k_d[i]) for i in rng]
    l_ab = [gb[i][:C] * strict[i] for i in rng]
    w_rb = [(gb[i][C:] * (strict[i] + eye_w)).astype(BF16) for i in rng]
    w_akrk = [rows2(gk[i][:C] * strict[i], gk[i][C:] * (strict[i] + eye_w)) for i in rng]

    pw = [_dot(l_ab[i].astype(BF16), diag(l_ab[i])) for i in rng]
    tm = [eye_w + l_ab[i] for i in rng]
    for _ in range(4):
        sq = [_dot(rows2(pw[i], tm[i]), diag(pw[i])) for i in rng]
        tm = [tm[i] + sq[i][C:] for i in rng]
        pw = [sq[i][:C] for i in rng]
    tm = [(tm[i] + _dot(tm[i].astype(BF16), diag(pw[i]))).astype(BF16) for i in rng]

    wt = [_dot(tm[i], a_d[i]) for i in rng]
    xy = [_dot(w_akrk[i], v_d[i]) for i in rng]
    u0 = [_dot(tm[i], diag(xy[i][:C])) for i in rng]
    rh = [rt[i] + _dot(w_rb[i], diag(wt[i])) for i in rng]
    y1 = [_dot(w_rb[i], diag(u0[i])) + xy[i][C:] for i in rng]
    for i, (ci, d) in enumerate(chains):
        rh_out[d, rows[i], :] = rh[i].astype(BF16)
        y1_out[d, rows[i], :] = y1[i]
        mp_out[d, ci] = (_dot_tn(wt[i].astype(BF16), bh[i].astype(BF16)) * hm).astype(BF16)
        np_bd = _dot_tn(rows2(u0[i], vv[i]), rows2(bh[i], kh[i])) * hm
        np_out[d, ci] = functools.reduce(
            jnp.add, [np_bd[h * HEAD_DIM:(h + 1) * HEAD_DIM] for h in range(4)])
        pc_out[d, ci] = jnp.broadcast_to(e_end[i], (SUBLANES, GROUP_W))


def _rwkv_chunks(feats, cum, tri, hm):
    r = feats[0]
    B, T, G = r.shape
    nch = T // CHUNK
    nsub = 4 if nch % 4 == 0 else 1
    tb = nsub * CHUNK
    tok = pl.BlockSpec((None, tb, G), lambda b, c: (b, c, 0))
    full = lambda a: pl.BlockSpec(a.shape, lambda b, c: (0,) * a.ndim)
    return pl.pallas_call(
        functools.partial(_rwkv_chunk_kernel, nsub=nsub),
        grid=(B, nch // nsub),
        in_specs=[tok] * 9 + [full(cum), full(tri), full(hm)],
        out_specs=[
            pl.BlockSpec((None, 2, tb, G), lambda b, c: (b, 0, c, 0)),
            pl.BlockSpec((None, 2, tb, G), lambda b, c: (b, 0, c, 0)),
            pl.BlockSpec((None, 2, nsub, G, G), lambda b, c: (b, 0, c, 0, 0)),
            pl.BlockSpec((None, 2, nsub, HEAD_DIM, G), lambda b, c: (b, 0, c, 0, 0)),
            pl.BlockSpec((None, 2, nsub, SUBLANES, G), lambda b, c: (b, 0, c, 0, 0)),
        ],
        out_shape=[
            jax.ShapeDtypeStruct((B, 2, T, G), BF16),
            jax.ShapeDtypeStruct((B, 2, T, G), F32),
            jax.ShapeDtypeStruct((B, 2, nch, G, G), BF16),
            jax.ShapeDtypeStruct((B, 2, nch, HEAD_DIM, G), F32),
            jax.ShapeDtypeStruct((B, 2, nch, SUBLANES, G), F32),
        ],
        compiler_params=_cparams(("parallel", "parallel")),
        name="rwkv_chunk_transitions",
    )(*feats, cum, tri, hm)


def _rwkv_scan_kernel(s0_ref, hm_ref, rhf_ref, y1f_ref, mpf_ref, npf_ref, pcf_ref,
                      rhb_ref, y1b_ref, mpb_ref, npb_ref, pcb_ref,
                      yf_out, yb_out, s_out, *, batch, nsub):
    c = pl.program_id(0)

    @pl.when(c == 0)
    def _():
        s_out[...] = s0_ref[...]

    hm_b = hm_ref[...].astype(BF16)
    dirs = ((rhf_ref, y1f_ref, mpf_ref, npf_ref, pcf_ref, yf_out),
            (rhb_ref, y1b_ref, mpb_ref, npb_ref, pcb_ref, yb_out))
    chains = [(b, d) for b in range(batch) for d in range(2)]
    s = [s_out[b, d] for b, d in chains]
    for step in range(nsub):
        s_b = [t.astype(BF16) for t in s]
        s_d = [jnp.concatenate([t, t, t, t], axis=0) * hm_b for t in s_b]
        for i, (b, d) in enumerate(chains):
            rh_ref, y1_ref, mp_ref, np_ref, pc_ref, y_out = dirs[d]
            ci = step if d == 0 else nsub - 1 - step
            rows = slice(ci * CHUNK, (ci + 1) * CHUNK)
            y_out[b, rows, :] = _dot_nt(rh_ref[b, rows, :], s_d[i]) + y1_ref[b, rows, :]
            s[i] = s[i] * pc_ref[b, ci, 0:1, :] + _dot(s_b[i], mp_ref[b, ci]) + np_ref[b, ci]
    for i, (b, d) in enumerate(chains):
        s_out[b, d] = s[i]


def _rwkv_scan(s0, hm, rh, y1, mp, npm, pc):
    B, _, T, G = rh.shape
    nch = T // CHUNK
    nsub = 4 if nch % 4 == 0 else 1
    nblk = nch // nsub
    tb = nsub * CHUNK
    fwd = lambda c: c
    bwd = lambda c: nblk - 1 - c

    def specs(order, d):
        return [
            pl.BlockSpec((B, None, tb, G), lambda c: (0, d, order(c), 0)),
            pl.BlockSpec((B, None, tb, G), lambda c: (0, d, order(c), 0)),
            pl.BlockSpec((B, None, nsub, G, G), lambda c: (0, d, order(c), 0, 0)),
            pl.BlockSpec((B, None, nsub, HEAD_DIM, G), lambda c: (0, d, order(c), 0, 0)),
            pl.BlockSpec((B, None, nsub, SUBLANES, G), lambda c: (0, d, order(c), 0, 0)),
        ]

    return pl.pallas_call(
        functools.partial(_rwkv_scan_kernel, batch=B, nsub=nsub),
        grid=(nblk,),
        in_specs=[pl.BlockSpec(s0.shape, lambda c: (0, 0, 0, 0)), pl.BlockSpec(hm.shape, lambda c: (0, 0))]
        + specs(fwd, 0) + specs(bwd, 1),
        out_specs=[
            pl.BlockSpec((B, tb, G), lambda c: (0, fwd(c), 0)),
            pl.BlockSpec((B, tb, G), lambda c: (0, bwd(c), 0)),
            pl.BlockSpec(s0.shape, lambda c: (0, 0, 0, 0)),
        ],
        out_shape=[
            jax.ShapeDtypeStruct((B, T, G), F32),
            jax.ShapeDtypeStruct((B, T, G), F32),
            jax.ShapeDtypeStruct(s0.shape, F32),
        ],
        compiler_params=_cparams(("arbitrary",)),
        name="rwkv_state_scan",
    )(s0, hm, rh, y1, mp, npm, pc, rh, y1, mp, npm, pc)


def _seg_matrix(width):
    i = np.arange(width) // HEAD_DIM
    return jnp.asarray((i[:, None] == i[None, :]).astype(np.float32), BF16)


def _scan_constants():
    C = CHUNK
    t = np.arange(4 * C)
    head, tok = t // C, t % C
    i = np.arange(C)
    lower = i[:, None] > tok[None, :]
    upper = i[:, None] < tok[None, :]
    eye = i[:, None] == tok[None, :]
    tri = jnp.asarray(np.stack([lower, upper, eye]).astype(np.float32))
    cum = jnp.asarray(np.stack([i[:, None] >= i[None, :], i[:, None] <= i[None, :]]).astype(np.float32), BF16)
    hm = jnp.asarray((head[:, None] == (np.arange(GROUP_W) // HEAD_DIM)[None, :]).astype(np.float32))
    return cum, tri, hm


def _pad_vec(v, lo, total):
    return jnp.zeros((total,), F32).at[lo:lo + v.shape[0]].set(v)


def kernel(x, c, ctx, c_ctx, w_mod, b_mod, norm_ffn1, ffn1_w_gu, ffn1_w_down, norm_mix, w_in, w_out, mla_q_norm, mla_kv_norm, mla_w_uq, mla_w_ukv, mla_qn, mla_kn, rwkv_shift, rwkv_w0, rwkv_w2, rwkv_a0, rwkv_a2, rwkv_g2, rwkv_k_k, rwkv_k_a, rwkv_r_k, rwkv_lnx_g, rwkv_lnx_b, na_qn, na_kn, na_rpb, gqa_qn, gqa_kn, norm_ffn2, ffn2_w_gu, ffn2_w_down):
    B, S, D = x.shape
    Lc = ctx.shape[1]
    depth = w_mod.shape[0]
    assert B + 1 <= SUBLANES and S % 512 == 0 and Lc % CHUNK == 0 and S % GRID_W == 0

    cvec = jnp.zeros((SUBLANES, D), F32).at[:B].set(c).at[B].set(c_ctx)
    mods = _modulation(cvec, B + 1, w_mod, b_mod).reshape(depth, SUBLANES, 9, D)

    mla_tab = _rope_tables(S // GRID_W, LANES, MLA_NOPE, MLA_ROPE // 2, 1)
    gqa_tab = _rope_tables(S // GRID_W, 4 * HEAD_DIM, 0, HEAD_DIM // 2, 4)
    mla_tab_c = _identity_rope_tables(Lc, LANES)
    gqa_tab_c = _identity_rope_tables(Lc, 4 * HEAD_DIM)

    seg = _seg_matrix(GROUP_W)
    cum, tri, hm = _scan_constants()
    mla_scale = MLA_QK ** -0.5 * LOG2E
    dh_scale = HEAD_DIM ** -0.5 * LOG2E

    w_gu1, w_dn1 = ffn1_w_gu.astype(BF16), ffn1_w_down.astype(BF16)
    w_gu2, w_dn2 = ffn2_w_gu.astype(BF16), ffn2_w_down.astype(BF16)
    w_out_b = w_out.astype(BF16)
    zc = lambda n: jnp.zeros((depth, D, n), BF16)
    wi = w_in.astype(BF16)
    o_b = MLA_COLS
    o_n = o_b + RWKV_COLS
    o_d = o_n + NA_COLS
    w_in_p = jnp.concatenate([
        wi[..., :MLA_Q_RANK + MLA_KV_RANK], zc(64), wi[..., MLA_Q_RANK + MLA_KV_RANK:MLA_COLS], zc(32),
        wi[..., o_b:o_n], zc(UB_W - RWKV_COLS),
        wi[..., o_n:o_d],
        wi[..., o_d:o_d + GQA_COLS]], axis=2)

    h, hc = x, ctx
    for l in range(depth):
        ctx_out = l < depth - 1
        m_lat = mods[l, :B]
        m_ctx = jnp.broadcast_to(mods[l, B], (B, 9, D))


        wuq = mla_w_uq[l].reshape(MLA_Q_RANK, MLA_HEADS, MLA_QK)
        wuq_p = jnp.pad(wuq, ((0, 0), (0, 0), (0, LANES - MLA_QK))).reshape(MLA_Q_RANK, MLA_HEADS * LANES).astype(BF16)
        wukv = mla_w_ukv[l].reshape(MLA_KV_RANK, MLA_HEADS, MLA_NOPE + MLA_V)
        wuk_p = jnp.pad(wukv[..., :MLA_NOPE], ((0, 0), (0, 0), (0, LANES - MLA_NOPE))).reshape(MLA_KV_RANK, MLA_HEADS * LANES).astype(BF16)
        wuv = wukv[..., MLA_NOPE:].reshape(MLA_KV_RANK, MLA_HEADS * MLA_V).astype(BF16)
        mla_small = (mla_q_norm[l].reshape(1, -1), mla_kv_norm[l].reshape(1, -1),
                     _pad_vec(mla_qn[l] * mla_scale, 0, LANES).reshape(1, LANES),
                     _pad_vec(mla_kn[l], 0, LANES).reshape(1, LANES), wuq_p, wuk_p, wuv)

        na_gq = jnp.tile(na_qn[l] * dh_scale, 4).reshape(1, -1)
        na_gk = jnp.tile(na_kn[l], 4).reshape(1, -1)
        gqa_gq = jnp.tile(gqa_qn[l] * dh_scale, 4).reshape(1, -1)
        gqa_gk = jnp.tile(gqa_kn[l], 2).reshape(1, -1)
        na_tb = _na_bias_tables(na_rpb[l])

        mu_p = jnp.zeros((SUBLANES, UB_W), F32).at[:2, :RWKV_COLS].set(rwkv_shift[l])
        def lora_w(w, off):
            return jnp.zeros((GROUP_W, GROUP_W), F32).at[off:off + w.shape[0]].set(w).astype(BF16)

        w2_p = jnp.stack([lora_w(rwkv_w2[l, 0], 0), lora_w(rwkv_w2[l, 1], 32)])
        a2_p = jnp.stack([lora_w(rwkv_a2[l, 0], 64), lora_w(rwkv_a2[l, 1], 96)])
        g2_p = lora_w(rwkv_g2[l], 128)
        feat_vecs = jnp.zeros((SUBLANES, GROUP_W), F32).at[0].set(rwkv_k_k[l]).at[1].set(rwkv_k_a[l]) \
            .at[2].set(rwkv_r_k[l].reshape(-1)).at[3:5].set(rwkv_w0[l]).at[5:7].set(rwkv_a0[l])
        post_vecs = jnp.zeros((SUBLANES, GROUP_W), F32).at[0].set(rwkv_lnx_g[l]).at[1].set(rwkv_lnx_b[l])

        h = _ffn(h, m_lat, norm_ffn1[l], w_gu1, w_dn1, l, 0)
        hc = _ffn(hc, m_ctx, norm_ffn1[l], w_gu1, w_dn1, l, 0)

        ua, ub, un, ud = _inproj(h, m_lat, norm_mix[l], w_in_p, l)
        uca, ucb, ucn, ucd = _inproj(hc, m_ctx, norm_mix[l], w_in_p, l)

        qa, ka, vta = _mla_prep(ua, mla_tab, *mla_small)
        qca, kca, vtca = _mla_prep(uca, mla_tab_c, *mla_small)
        oa = _attention(qa, ka, vta, kca, vtca)

        qn_, kn_, vn_ = _qkv_prep(un, seg, na_gq, na_gk, 4, 4, 4)
        qcn, kcn, vcn, vtcn = _qkv_prep(ucn, seg, na_gq, na_gk, 4, 4, 4, v_cols=True)
        on = _na_attention(qn_, kn_, vn_, kcn, vcn, na_tb)

        qd, kd, vtd = _qkv_prep(ud, seg, gqa_gq, gqa_gk, 4, 2, 2, gqa_tab, v_rows=False, v_cols=True)
        qcd, kcd, vtcd = _qkv_prep(ucd, seg, gqa_gq, gqa_gk, 4, 2, 2, gqa_tab_c, v_rows=False, v_cols=True)
        od = _attention(qd, kd, vtd, kcd, vtcd)

        feats_c = _rwkv_features(ucb, mu_p, seg, w2_p, a2_p, g2_p, feat_vecs)
        feats = _rwkv_features(ub, mu_p, seg, w2_p, a2_p, g2_p, feat_vecs)
        trans_c = _rwkv_chunks(feats_c[:9], cum, tri, hm)
        trans = _rwkv_chunks(feats[:9], cum, tri, hm)
        s0 = jnp.zeros((B, 2, HEAD_DIM, GROUP_W), F32)
        yfc, ybc, s_ctx = _rwkv_scan(s0, hm, *trans_c)
        yf, yb, _ = _rwkv_scan(s_ctx, hm, *trans)

        h = _outproj(h, m_lat, oa, (yf, yb, feats[9], feats[10]), on, od, (True, False, True),
                     seg, post_vecs, w_out_b, l)
        if ctx_out:
            oca = _attention(qca, kca, vtca)
            ocn = _attention(qcn, kcn, vtcn)
            ocd = _attention(qcd, kcd, vtcd)
            hc = _outproj(hc, m_ctx, oca, (yfc, ybc, feats_c[9], feats_c[10]), ocn, ocd,
                          (True, True, True), seg, post_vecs, w_out_b, l)

        h = _ffn(h, m_lat, norm_ffn2[l], w_gu2, w_dn2, l, 6)
        if ctx_out:
            hc = _ffn(hc, m_ctx, norm_ffn2[l], w_gu2, w_dn2, l, 6)
    return h
```

```python
import functools

import numpy as np
import jax
import jax.numpy as jnp
from jax import lax
from jax.experimental import pallas as pl
from jax.experimental.pallas import tpu as pltpu

F32 = jnp.float32
BF16 = jnp.bfloat16

GRID_W = 64
HEAD_DIM = 64
GROUP_W = 256
ROPE_THETA = 10000.0
NORM_EPS = 1e-6
MLA_HEADS = 4
MLA_Q_RANK = 256
MLA_KV_RANK = 128
MLA_NOPE = 64
MLA_ROPE = 32
MLA_V = 64
MLA_QK = MLA_NOPE + MLA_ROPE
RWKV_COLS = 960
RWKV_LNX_EPS = 64e-5
NA_WIN_R = 8
NA_WIN_C = 16
MLA_COLS = 416
NA_COLS = 768
GQA_COLS = 512

LANES = 128
SUBLANES = 8
VMEM_LIMIT_BYTES = 56 * 1024 * 1024

UA_W = 512
UB_W = 1024
UN_W = 768
UD_W = 512

VT_ROWS = 128
CHUNK = 64
NEG_INF = -1e30
LOG2E = 1.4426950408889634
DECAY_SCALE = 0.6065306597126334
NA_ROWS_PER_ITER = 8
MAX_JUMP = 24.0


def _dot(a, b):
    return jnp.dot(a, b, preferred_element_type=F32)


def _dot_nt(a, b):
    return lax.dot_general(a, b, (((1,), (1,)), ((), ())), preferred_element_type=F32)


def _dot_tn(a, b):
    return lax.dot_general(a, b, (((0,), (0,)), ((), ())), preferred_element_type=F32)


def _split2(x):
    hi = x.astype(BF16)
    lo = (x - hi.astype(F32)).astype(BF16)
    return hi, lo


def _dot_exact_rhs(x, m):
    h1 = x.astype(BF16)
    r1 = x - h1.astype(F32)
    h2 = r1.astype(BF16)
    h3 = (r1 - h2.astype(F32)).astype(BF16)
    return _dot(h1, m) + _dot(h2, m) + _dot(h3, m)


def _dot_exact_lhs(m, x):
    h1 = x.astype(BF16)
    r1 = x - h1.astype(F32)
    h2 = r1.astype(BF16)
    h3 = (r1 - h2.astype(F32)).astype(BF16)
    return _dot(m, h1) + _dot(m, h2) + _dot(m, h3)


def _rms(x, eps=NORM_EPS):
    return x * lax.rsqrt(jnp.mean(x * x, axis=-1, keepdims=True) + eps)


def _cparams(sem):
    return pltpu.CompilerParams(dimension_semantics=sem, vmem_limit_bytes=VMEM_LIMIT_BYTES)


def _mod_kernel(ct_ref, w_ref, b_ref, o_ref, *, n_rows):
    ct = ct_ref[...]
    st = ct * jax.nn.sigmoid(ct)
    w = w_ref[...]
    rows = [jnp.sum(w * st[:, r:r + 1], axis=0, keepdims=True) for r in range(n_rows)]
    rows.append(jnp.zeros((SUBLANES - n_rows, w.shape[1]), F32))
    o_ref[...] = jnp.concatenate(rows, axis=0) + b_ref[...]


def _modulation(cvec, n_rows, w_mod, b_mod):
    L, D, N = w_mod.shape
    tn = 1024
    return pl.pallas_call(
        functools.partial(_mod_kernel, n_rows=n_rows),
        grid=(L, N // tn),
        in_specs=[
            pl.BlockSpec((D, SUBLANES), lambda l, n: (0, 0)),
            pl.BlockSpec((None, D, tn), lambda l, n: (l, 0, n)),
            pl.BlockSpec((None, 1, tn), lambda l, n: (l, 0, n)),
        ],
        out_specs=pl.BlockSpec((None, SUBLANES, tn), lambda l, n: (l, 0, n)),
        out_shape=jax.ShapeDtypeStruct((L, SUBLANES, N), F32),
        compiler_params=_cparams(("parallel", "parallel")),
        name="adaln_mod",
    )(cvec.T, w_mod, b_mod.reshape(L, 1, N))


def _ffn_kernel(h_ref, mod_ref, g_ref, wg_ref, wu_ref, wd_ref, o_ref, *, base):
    x = h_ref[...]
    y = _rms(x) * g_ref[...]
    xn = (y * (1.0 + mod_ref[base + 1:base + 2, :]) + mod_ref[base:base + 1, :]).astype(BF16)
    g = _dot(xn, wg_ref[...])
    u = _dot(xn, wu_ref[...])
    a = (g * jax.nn.sigmoid(g) * u).astype(BF16)
    o_ref[...] = x + 0.5 * mod_ref[base + 2:base + 3, :] * _dot(a, wd_ref[...])


def _ffn(h, mod, gain, w_gu, w_down, layer, base):
    B, T, D = h.shape
    F = w_down.shape[1]
    tm = min(T, 512)
    resident = pl.Buffered(1)
    return pl.pallas_call(
        functools.partial(_ffn_kernel, base=base),
        grid=(B, T // tm),
        in_specs=[
            pl.BlockSpec((None, tm, D), lambda b, i: (b, i, 0)),
            pl.BlockSpec((None, 9, D), lambda b, i: (b, 0, 0)),
            pl.BlockSpec((1, D), lambda b, i: (0, 0)),
            pl.BlockSpec((None, D, F), lambda b, i: (layer, 0, 0), pipeline_mode=resident),
            pl.BlockSpec((None, D, F), lambda b, i: (layer, 0, 1), pipeline_mode=resident),
            pl.BlockSpec((None, F, D), lambda b, i: (layer, 0, 0), pipeline_mode=resident),
        ],
        out_specs=pl.BlockSpec((None, tm, D), lambda b, i: (b, i, 0)),
        out_shape=jax.ShapeDtypeStruct((B, T, D), F32),
        compiler_params=_cparams(("parallel", "parallel")),
        name="swiglu_halfstep",
    )(h, mod, gain.reshape(1, D), w_gu, w_gu, w_down)


def _rwkv_out(yf_ref, yb_ref, bon_ref, g_ref, seg_ref, vec_ref):
    y = yf_ref[...] + yb_ref[...]
    seg = seg_ref[...]

    def segmean(t):
        hi, lo = _split2(t)
        return (_dot(hi, seg) + _dot(lo, seg)) * (1.0 / HEAD_DIM)

    mu = segmean(y)
    yc = y - mu
    var = segmean(yc * yc)
    yn = yc * lax.rsqrt(var + RWKV_LNX_EPS) * vec_ref[0:1, :] + vec_ref[1:2, :]
    return ((yn + bon_ref[...]) * g_ref[...]).astype(BF16)


def _outproj_kernel(h_ref, mod_ref, oa_ref, yf_ref, yb_ref, bon_ref, g_ref, on_ref, od_ref,
                    seg_ref, vec_ref, w_ref, o_ref, *, transposed):
    ob = _rwkv_out(yf_ref, yb_ref, bon_ref, g_ref, seg_ref, vec_ref)
    groups = ((oa_ref, transposed[0]), (None, False), (on_ref, transposed[1]), (od_ref, transposed[2]))
    acc = None
    for g, (ref, t) in enumerate(groups):
        w = w_ref[g * GROUP_W:(g + 1) * GROUP_W, :]
        x = ob if ref is None else ref[...]
        part = _dot_tn(x, w) if t else _dot(x, w)
        acc = part if acc is None else acc + part
    o_ref[...] = h_ref[...] + mod_ref[5:6, :] * acc


def _outproj(h, mod, oa, rwkv, on, od, transposed, seg, post_vecs, w_out, layer):
    B, T, D = h.shape
    tm = min(T, 512)
    grp = pl.BlockSpec((None, tm, GROUP_W), lambda b, i: (b, i, 0))
    grp_t = pl.BlockSpec((None, GROUP_W, tm), lambda b, i: (b, 0, i))
    pick = lambda t: grp_t if t else grp
    full = lambda a: pl.BlockSpec(a.shape, lambda b, i: (0,) * a.ndim)
    return pl.pallas_call(
        functools.partial(_outproj_kernel, transposed=tuple(transposed)),
        grid=(B, T // tm),
        in_specs=[
            pl.BlockSpec((None, tm, D), lambda b, i: (b, i, 0)),
            pl.BlockSpec((None, 9, D), lambda b, i: (b, 0, 0)),
            pick(transposed[0]), grp, grp, grp, grp, pick(transposed[1]), pick(transposed[2]),
            full(seg), full(post_vecs),
            pl.BlockSpec((None, 4 * GROUP_W, D), lambda b, i: (layer, 0, 0)),
        ],
        out_specs=pl.BlockSpec((None, tm, D), lambda b, i: (b, i, 0)),
        out_shape=jax.ShapeDtypeStruct((B, T, D), F32),
        compiler_params=_cparams(("parallel", "parallel")),
        name="out_projection",
    )(h, mod, oa, *rwkv, on, od, seg, post_vecs, w_out)


def _rope_tables(n_rows, width, lead, group, n_rep):
    half = group // 2
    lane = np.arange(width)
    rel = (lane - lead) % (2 * group)
    inside = (lane >= lead) & (lane < lead + n_rep * 2 * group)
    is_row = (rel < group)[None, None, :]
    second = ((rel % group >= half) & inside)[None, None, :]
    first = ((rel % group < half) & inside)[None, None, :]
    inv_freq = ROPE_THETA ** (-jnp.arange(half, dtype=F32) / half)
    freq = jnp.where(inside, inv_freq[rel % half], 0.0)[None, :]
    ang_r = jnp.arange(n_rows).astype(F32)[:, None] * freq
    ang_c = jnp.arange(GRID_W).astype(F32)[:, None] * freq
    shape = (n_rows, GRID_W, width)
    c = jnp.where(is_row, jnp.cos(ang_r)[:, None, :], jnp.cos(ang_c)[None, :, :])
    s = jnp.where(is_row, jnp.sin(ang_r)[:, None, :], jnp.sin(ang_c)[None, :, :])
    flat = lambda t: jnp.broadcast_to(t, shape).reshape(n_rows * GRID_W, width)
    return [flat(c), flat(jnp.where(second, s, 0.0)), flat(jnp.where(first, -s, 0.0))]


def _identity_rope_tables(n_tokens, width):
    return [jnp.ones((n_tokens, width), F32), jnp.zeros((n_tokens, width), F32),
            jnp.zeros((n_tokens, width), F32)]


def _vt_ext(v):
    tm = v.shape[0]
    lane = lax.broadcasted_iota(jnp.int32, (tm, VT_ROWS - HEAD_DIM), 1)
    aux = jnp.where(lane == 0, 1.0, 0.0)
    return jnp.concatenate([v, aux], axis=-1).T.astype(BF16)


def _mla_prep_math(ua, c_ref, s1_ref, s2_ref, qnorm_ref, kvnorm_ref, qn_ref, kn_ref,
                   wuq_ref, wuk_ref, wuv_ref, q_out, k_out, vt_out):
    qc = (_rms(ua[:, :MLA_Q_RANK]) * qnorm_ref[...]).astype(BF16)
    kvc = (_rms(ua[:, MLA_Q_RANK:MLA_Q_RANK + MLA_KV_RANK]) * kvnorm_ref[...]).astype(BF16)
    k_rope = ua[:, 3 * LANES:4 * LANES]
    q_all = _dot(qc, wuq_ref[...])
    k_all = _dot(kvc, wuk_ref[...])
    v_all = _dot(kvc, wuv_ref[...])
    cos, s_dn, s_up = c_ref[...], s1_ref[...], s2_ref[...]
    half = MLA_ROPE // 4

    def rope(x):
        return x * cos + pltpu.roll(x, half, 1) * s_dn + pltpu.roll(x, LANES - half, 1) * s_up

    def headnorm(x, gain):
        ms = jnp.sum(x * x, axis=-1, keepdims=True) * (1.0 / MLA_QK)
        return x * lax.rsqrt(ms + NORM_EPS) * gain

    for h in range(MLA_HEADS):
        qh = headnorm(q_all[:, h * LANES:(h + 1) * LANES], qn_ref[...])
        q_out[h] = rope(qh).astype(BF16)
        kh = headnorm(k_all[:, h * LANES:(h + 1) * LANES] + k_rope, kn_ref[...])
        k_out[h] = rope(kh).astype(BF16)
        vt_out[h] = _vt_ext(v_all[:, h * MLA_V:(h + 1) * MLA_V])


def _qkv_prep_math(x, seg_ref, gq_ref, gk_ref, tabs, q_out, k_out, v_out, vt_out, *, nq, nk, nv):
    use_rope = tabs is not None
    wq, wk, wv = nq * HEAD_DIM, nk * HEAD_DIM, nv * HEAD_DIM
    q = x[:, 0:wq]
    k = x[:, wq:wq + wk]
    v = x[:, wq + wk:wq + wk + wv]

    def headnorm(t, gain, w):
        hi, lo = _split2(t * t)
        seg = seg_ref[0:w, 0:w]
        ms = (_dot(hi, seg) + _dot(lo, seg)) * (1.0 / HEAD_DIM)
        return t * lax.rsqrt(ms + NORM_EPS) * gain

    q = headnorm(q, gq_ref[...], wq)
    k = headnorm(k, gk_ref[...], wk)
    if use_rope:
        c_ref, s1_ref, s2_ref = tabs
        half = HEAD_DIM // 4

        def rope(t, w):
            return (t * c_ref[:, 0:w] + pltpu.roll(t, half, 1) * s1_ref[:, 0:w]
                    + pltpu.roll(t, w - half, 1) * s2_ref[:, 0:w])

        q = rope(q, wq)
        k = rope(k, wk)
    for h in range(nq):
        q_out[h] = q[:, h * HEAD_DIM:(h + 1) * HEAD_DIM].astype(BF16)
    for h in range(nk):
        k_out[h] = k[:, h * HEAD_DIM:(h + 1) * HEAD_DIM].astype(BF16)
    for h in range(nv):
        vh = v[:, h * HEAD_DIM:(h + 1) * HEAD_DIM]
        if v_out is not None:
            v_out[h] = vh.astype(BF16)
        if vt_out is not None:
            vt_out[h] = _vt_ext(vh)


def _inproj_kernel(h_ref, mod_ref, g_ref, w_ref, oa_ref, ob_ref, on_ref, od_ref):
    y = _rms(h_ref[...]) * g_ref[...]
    xm = (y * (1.0 + mod_ref[4:5, :]) + mod_ref[3:4, :]).astype(BF16)
    o = 0
    for ref, w in ((oa_ref, UA_W), (ob_ref, UB_W), (on_ref, UN_W), (od_ref, UD_W)):
        ref[...] = _dot(xm, w_ref[:, o:o + w])
        o += w


def _inproj(h, mod, gain, w_in_p, layer):
    B, T, D = h.shape
    tm = min(T, 512)
    W = w_in_p.shape[2]
    widths = (UA_W, UB_W, UN_W, UD_W)
    return pl.pallas_call(
        _inproj_kernel,
        grid=(B, T // tm),
        in_specs=[
            pl.BlockSpec((None, tm, D), lambda b, i: (b, i, 0)),
            pl.BlockSpec((None, 9, D), lambda b, i: (b, 0, 0)),
            pl.BlockSpec((1, D), lambda b, i: (0, 0)),
            pl.BlockSpec((None, D, W), lambda b, i: (layer, 0, 0)),
        ],
        out_specs=[pl.BlockSpec((None, tm, w), lambda b, i: (b, i, 0)) for w in widths],
        out_shape=[jax.ShapeDtypeStruct((B, T, w), F32) for w in widths],
        compiler_params=_cparams(("parallel", "parallel")),
        name="in_projection",
    )(h, mod, gain.reshape(1, D), w_in_p)


def _mla_prep_kernel(ua_ref, *refs):
    _mla_prep_math(ua_ref[...], *refs)


def _mla_prep(ua, tables, q_norm, kv_norm, qn_p, kn_p, wuq_p, wuk_p, wuv):
    B, T, _ = ua.shape
    tm = min(T, 1024)
    tab = pl.BlockSpec((tm, LANES), lambda b, i: (i, 0))
    full = lambda a: pl.BlockSpec(a.shape, lambda b, i: (0,) * a.ndim)
    small = [q_norm, kv_norm, qn_p, kn_p, wuq_p, wuk_p, wuv]
    return pl.pallas_call(
        _mla_prep_kernel,
        grid=(B, T // tm),
        in_specs=[pl.BlockSpec((None, tm, UA_W), lambda b, i: (b, i, 0)), tab, tab, tab]
        + [full(a) for a in small],
        out_specs=[
            pl.BlockSpec((None, MLA_HEADS, tm, LANES), lambda b, i: (b, 0, i, 0)),
            pl.BlockSpec((None, MLA_HEADS, tm, LANES), lambda b, i: (b, 0, i, 0)),
            pl.BlockSpec((None, MLA_HEADS, VT_ROWS, tm), lambda b, i: (b, 0, 0, i)),
        ],
        out_shape=[
            jax.ShapeDtypeStruct((B, MLA_HEADS, T, LANES), BF16),
            jax.ShapeDtypeStruct((B, MLA_HEADS, T, LANES), BF16),
            jax.ShapeDtypeStruct((B, MLA_HEADS, VT_ROWS, T), BF16),
        ],
        compiler_params=_cparams(("parallel", "parallel")),
        name="mla_prep",
    )(ua, *tables, *small)


def _qkv_prep_kernel(*refs, nq, nk, nv, use_rope, v_rows, v_cols):
    n_in = 7 if use_rope else 4
    x_ref, seg_ref, gq_ref, gk_ref = refs[:4]
    tabs = refs[4:7] if use_rope else None
    q_out, k_out = refs[n_in:n_in + 2]
    v_outs = list(refs[n_in + 2:])
    v_out = v_outs.pop(0) if v_rows else None
    vt_out = v_outs.pop(0) if v_cols else None
    _qkv_prep_math(x_ref[...], seg_ref, gq_ref, gk_ref, tabs, q_out, k_out, v_out, vt_out,
                   nq=nq, nk=nk, nv=nv)


def _qkv_prep(x, seg, gq, gk, nq, nk, nv, tables=None, v_rows=True, v_cols=False):
    B, T, W = x.shape
    tm = min(T, 1024)
    use_rope = tables is not None
    full = lambda a: pl.BlockSpec(a.shape, lambda b, i: (0,) * a.ndim)
    in_specs = [pl.BlockSpec((None, tm, W), lambda b, i: (b, i, 0)), full(seg), full(gq), full(gk)]
    args = [x, seg, gq, gk]
    if use_rope:
        in_specs += [pl.BlockSpec((tm, tables[0].shape[1]), lambda b, i: (i, 0))] * 3
        args += list(tables)
    hm = lambda n: pl.BlockSpec((None, n, tm, HEAD_DIM), lambda b, i: (b, 0, i, 0))
    out_specs = [hm(nq), hm(nk)]
    out_shape = [jax.ShapeDtypeStruct((B, n, T, HEAD_DIM), BF16) for n in (nq, nk)]
    if v_rows:
        out_specs.append(hm(nv))
        out_shape.append(jax.ShapeDtypeStruct((B, nv, T, HEAD_DIM), BF16))
    if v_cols:
        out_specs.append(pl.BlockSpec((None, nv, VT_ROWS, tm), lambda b, i: (b, 0, 0, i)))
        out_shape.append(jax.ShapeDtypeStruct((B, nv, VT_ROWS, T), BF16))
    return pl.pallas_call(
        functools.partial(_qkv_prep_kernel, nq=nq, nk=nk, nv=nv, use_rope=use_rope,
                          v_rows=v_rows, v_cols=v_cols),
        grid=(B, T // tm),
        in_specs=in_specs,
        out_specs=out_specs,
        out_shape=out_shape,
        compiler_params=_cparams(("parallel", "parallel")),
        name="qkv_prep_rope" if use_rope else "qkv_prep",
    )(*args)


def _attn_kernel(*refs, n_heads, group, has_extra, nk):
    if has_extra:
        q_ref, k_ref, vt_ref, kx_ref, vtx_ref, o_ref, m_ref, acc_ref = refs
    else:
        q_ref, k_ref, vt_ref, o_ref, m_ref, acc_ref = refs
    j = pl.program_id(2)

    heads = range(n_heads)

    def first(kr, vr):
        s = [_dot_nt(kr[h // group], q_ref[h]) for h in heads]
        m = [jnp.max(s[h], axis=0, keepdims=True) for h in heads]
        p = [jnp.exp2(s[h] - m[h]).astype(BF16) for h in heads]
        for h in heads:
            m_ref[h] = m[h]
            acc_ref[h] = _dot(vr[h // group], p[h])

    def update(kr, vr):
        m_prev = [m_ref[h] for h in heads]
        s = [_dot_nt(kr[h // group], q_ref[h]) for h in heads]
        pv = [_dot(vr[h // group], jnp.exp2(s[h] - m_prev[h]).astype(BF16)) for h in heads]
        m_blk = [jnp.max(s[h], axis=0, keepdims=True) for h in heads]
        jump = functools.reduce(jnp.maximum, [m_blk[h] - m_prev[h] for h in heads])
        safe = jnp.max(jump) <= MAX_JUMP

        @pl.when(safe)
        def _():
            for h in heads:
                m_new = jnp.maximum(m_prev[h], m_blk[h])
                m_ref[h] = m_new
                acc_ref[h] = (acc_ref[h] + pv[h]) * jnp.exp2(m_prev[h] - m_new)

        @pl.when(jnp.logical_not(safe))
        def _():
            s2 = [_dot_nt(kr[h // group], q_ref[h]) for h in heads]
            m_new = [jnp.maximum(m_ref[h], jnp.max(s2[h], axis=0, keepdims=True)) for h in heads]
            p2 = [jnp.exp2(s2[h] - m_new[h]).astype(BF16) for h in heads]
            for h in heads:
                acc_ref[h] = jnp.exp2(m_ref[h] - m_new[h]) * acc_ref[h] + _dot(vr[h // group], p2[h])
                m_ref[h] = m_new[h]

    if has_extra:
        @pl.when(j == 0)
        def _():
            first(kx_ref, vtx_ref)

        update(k_ref, vt_ref)
    else:
        @pl.when(j == 0)
        def _():
            first(k_ref, vt_ref)

        @pl.when(j > 0)
        def _():
            update(k_ref, vt_ref)

    @pl.when(j == nk - 1)
    def _():
        outs = []
        for h in range(n_heads):
            a = acc_ref[h]
            outs.append(a[:HEAD_DIM] / a[HEAD_DIM:HEAD_DIM + 1])
        o_ref[...] = jnp.concatenate(outs, axis=0).astype(o_ref.dtype)


def _attention(q, k, vt, kx=None, vtx=None):
    B, H, Tq, dq = q.shape
    Hk, Tk = k.shape[1], k.shape[2]
    group = H // Hk
    tq = min(Tq, 512)
    tk = min(Tk, 2048)
    nk = Tk // tk
    has_extra = kx is not None
    in_specs = [
        pl.BlockSpec((None, H, tq, dq), lambda b, i, j: (b, 0, i, 0)),
        pl.BlockSpec((None, Hk, tk, dq), lambda b, i, j: (b, 0, j, 0)),
        pl.BlockSpec((None, Hk, VT_ROWS, tk), lambda b, i, j: (b, 0, 0, j)),
    ]
    args = [q, k, vt]
    if has_extra:
        Tx = kx.shape[2]
        in_specs += [pl.BlockSpec((None, Hk, Tx, dq), lambda b, i, j: (b, 0, 0, 0)),
                     pl.BlockSpec((None, Hk, VT_ROWS, Tx), lambda b, i, j: (b, 0, 0, 0))]
        args += [kx, vtx]
    return pl.pallas_call(
        functools.partial(_attn_kernel, n_heads=H, group=group, has_extra=has_extra, nk=nk),
        grid=(B, Tq // tq, nk),
        in_specs=in_specs,
        out_specs=pl.BlockSpec((None, H * HEAD_DIM, tq), lambda b, i, j: (b, 0, i)),
        out_shape=jax.ShapeDtypeStruct((B, H * HEAD_DIM, Tq), BF16),
        scratch_shapes=[pltpu.VMEM((H, 1, tq), F32), pltpu.VMEM((H, VT_ROWS, tq), F32)],
        compiler_params=_cparams(("parallel", "parallel", "arbitrary")),
        name="flash_attention_ctx" if has_extra else "flash_attention",
    )(*args)


def _na_bias_kernel(r_ref, e_ref, m_ref, o_ref):
    o_ref[...] = _dot_exact_rhs(r_ref[...], e_ref[...]) + m_ref[...]


def _na_bias_tables(rpb):
    H, n_dr, n_dc = rpb.shape
    W = GRID_W
    qc = np.arange(W)[:, None]
    kc = np.arange(W)[None, :]
    c0 = np.clip(qc - NA_WIN_C // 2, 0, W - NA_WIN_C)
    valid = (kc >= c0) & (kc < c0 + NA_WIN_C)
    d = kc - qc + NA_WIN_C - 1
    onehot = np.zeros((LANES, W * W), np.float32)
    for a in range(W):
        for b in range(W):
            if valid[a, b]:
                onehot[d[a, b], a * W + b] = 1.0
    mask = np.where(valid, 0.0, NEG_INF).astype(np.float32).reshape(1, W * W)
    rows = H * n_dr
    rows_p = -(-rows // SUBLANES) * SUBLANES
    r2 = jnp.zeros((rows_p, LANES), F32).at[:rows, :n_dc].set(rpb.reshape(rows, n_dc) * LOG2E)
    tn = 1024
    tiles = pl.pallas_call(
        _na_bias_kernel,
        grid=(W * W // tn,),
        in_specs=[pl.BlockSpec((rows_p, LANES), lambda n: (0, 0)),
                  pl.BlockSpec((LANES, tn), lambda n: (0, n)),
                  pl.BlockSpec((1, tn), lambda n: (0, n))],
        out_specs=pl.BlockSpec((rows_p, tn), lambda n: (0, n)),
        out_shape=jax.ShapeDtypeStruct((rows_p, W * W), F32),
        compiler_params=_cparams(("parallel",)),
        name="na_bias_expand",
    )(r2, jnp.asarray(onehot, BF16), jnp.asarray(mask))
    tiles = tiles[:rows].reshape(H, n_dr, W, W)
    slabs = [tiles[:, base:base + NA_WIN_R].transpose(0, 2, 1, 3).reshape(H, W, NA_WIN_R * W)
             for base in range(NA_WIN_R)]
    return jnp.stack(slabs, 0)


def _na_kernel(q_ref, k_ref, v_ref, kc_ref, vc_ref, tb_ref, o_ref, *, rb, rows, n_heads):
    blk = pl.program_id(1)
    W = GRID_W
    win = NA_WIN_R * W

    nr = NA_ROWS_PER_ITER

    def rows_body(i, carry):
        q0 = pl.multiple_of(i * nr * W, nr * W)
        heads = range(n_heads)
        units = [(j, h) for j in range(nr) for h in heads]
        k0, base = [], []
        for j in range(nr):
            r = blk * rb + nr * i + j
            r0 = jnp.clip(r - NA_WIN_R // 2, 0, rows - NA_WIN_R)
            base.append(r0 - r + NA_WIN_R - 1)
            k0.append(pl.multiple_of(r0 * W, W))
        qs = [q_ref[h, pl.ds(q0, nr * W), :] for h in heads]
        s_cxs = [_dot_nt(qs[h], kc_ref[h]) for h in heads]
        s_nb = [_dot_nt(qs[h][j * W:(j + 1) * W], k_ref[h, pl.ds(k0[j], win), :]) + tb_ref[base[j], h]
                for j, h in units]
        s_cx = [s_cxs[h][j * W:(j + 1) * W] for j, h in units]
        n = len(units)
        m = [jnp.maximum(jnp.max(s_nb[u], axis=-1, keepdims=True),
                         jnp.max(s_cx[u], axis=-1, keepdims=True)) for u in range(n)]
        p_nb = [jnp.exp2(s_nb[u] - m[u]) for u in range(n)]
        p_cx = [jnp.exp2(s_cx[u] - m[u]) for u in range(n)]
        l = [jnp.sum(p_nb[u], axis=-1, keepdims=True) + jnp.sum(p_cx[u], axis=-1, keepdims=True)
             for u in range(n)]
        o_cxs = [_dot(jnp.concatenate([p_cx[j * n_heads + h] for j in range(nr)], axis=0).astype(BF16),
                      vc_ref[h]) for h in heads]
        o = [_dot(p_nb[u].astype(BF16), v_ref[h, pl.ds(k0[j], win), :]) + o_cxs[h][j * W:(j + 1) * W]
             for u, (j, h) in enumerate(units)]
        o_ref[pl.ds(q0, nr * W), :] = jnp.concatenate(
            [jnp.concatenate([o[j * n_heads + h] / l[j * n_heads + h] for h in heads], axis=-1)
             for j in range(nr)], axis=0).astype(o_ref.dtype)
        return carry

    lax.fori_loop(0, rb // nr, rows_body, 0)


def _na_attention(q, k, v, kc, vc, tb):
    B, H, S, dh = q.shape
    L = kc.shape[2]
    rows = S // GRID_W
    assert rows >= NA_WIN_R
    rb = 8
    return pl.pallas_call(
        functools.partial(_na_kernel, rb=rb, rows=rows, n_heads=H),
        grid=(B, rows // rb),
        in_specs=[
            pl.BlockSpec((None, H, rb * GRID_W, dh), lambda b, i: (b, 0, i, 0)),
            pl.BlockSpec((None, H, S, dh), lambda b, i: (b, 0, 0, 0)),
            pl.BlockSpec((None, H, S, dh), lambda b, i: (b, 0, 0, 0)),
            pl.BlockSpec((None, H, L, dh), lambda b, i: (b, 0, 0, 0)),
            pl.BlockSpec((None, H, L, dh), lambda b, i: (b, 0, 0, 0)),
            pl.BlockSpec(tb.shape, lambda b, i: (0, 0, 0, 0)),
        ],
        out_specs=pl.BlockSpec((None, rb * GRID_W, H * dh), lambda b, i: (b, i, 0)),
        out_shape=jax.ShapeDtypeStruct((B, S, H * dh), BF16),
        compiler_params=_cparams(("parallel", "arbitrary")),
        name="neighbourhood_attention",
    )(q, k, v, kc, vc, tb)


def _rwkv_feat_kernel(x_ref, xp_ref, xn_ref, mu_ref, seg_ref, w2_ref, a2_ref, g2_ref, vec_ref,
                      r_out, v_out, kk_out, lwf_out, lwb_out, kdf_out, kdb_out, af_out, ab_out,
                      bon_out, g_out, *, nblk):
    i = pl.program_id(1)
    x = x_ref[...]
    tm = x.shape[0]
    has_prev = jnp.where(i > 0, 1.0, 0.0)
    has_next = jnp.where(i < nblk - 1, 1.0, 0.0)
    row = lax.broadcasted_iota(jnp.int32, (tm, 1), 0)
    prev = jnp.where(row == 0, xp_ref[SUBLANES - 1:SUBLANES, :] * has_prev, pltpu.roll(x, 1, 0))
    nxt = jnp.where(row == tm - 1, xn_ref[0:1, :] * has_next, pltpu.roll(x, tm - 1, 0))
    xs = x + mu_ref[0:1, :] * (prev - x) + mu_ref[1:2, :] * (nxt - x)

    G = GROUP_W
    r, k, v = xs[:, 0:G], xs[:, G:2 * G], xs[:, 2 * G:3 * G]
    lora = xs[:, 3 * G:4 * G]
    seg = seg_ref[...]

    def segsum(t):
        hi, lo = _split2(t)
        return _dot(hi, seg) + _dot(lo, seg)

    k_k, k_a, r_k = vec_ref[0:1, :], vec_ref[1:2, :], vec_ref[2:3, :]
    kk = k * k_k
    kk = kk * lax.rsqrt(segsum(kk * kk) + 1e-12)
    g_out[...] = _dot(jax.nn.sigmoid(lora).astype(BF16), g2_ref[...])
    tanh_l = jnp.tanh(lora).astype(BF16)
    lora_b = lora.astype(BF16)
    kd_sum = None
    for d, (lw_out, kd_out, a_out) in enumerate(((lwf_out, kdf_out, af_out), (lwb_out, kdb_out, ab_out))):
        w0, a0 = vec_ref[3 + d:4 + d, :], vec_ref[5 + d:6 + d, :]
        z = w0 + _dot(tanh_l, w2_ref[d])
        lw_out[...] = -DECAY_SCALE * jax.nn.sigmoid(z)
        a = jax.nn.sigmoid(a0 + _dot(lora_b, a2_ref[d]))
        kd = k * (1.0 + (a - 1.0) * k_a)
        kd_out[...] = kd
        a_out[...] = a
        kd_sum = kd if kd_sum is None else kd_sum + kd
    r_out[...] = r
    v_out[...] = v
    kk_out[...] = kk
    bon_out[...] = segsum(r * kd_sum * r_k) * v


def _rwkv_features(ub, mu_p, seg, w2_p, a2_p, g2_p, vecs):
    B, T, W = ub.shape
    tm = min(T, 512)
    nblk = T // tm
    hb = tm // SUBLANES
    nh = T // SUBLANES
    full = lambda a: pl.BlockSpec(a.shape, lambda b, i: (0,) * a.ndim)
    out = pl.BlockSpec((None, tm, GROUP_W), lambda b, i: (b, i, 0))
    return pl.pallas_call(
        functools.partial(_rwkv_feat_kernel, nblk=nblk),
        grid=(B, nblk),
        in_specs=[
            pl.BlockSpec((None, tm, W), lambda b, i: (b, i, 0)),
            pl.BlockSpec((None, SUBLANES, W), lambda b, i: (b, jnp.maximum(i * hb - 1, 0), 0)),
            pl.BlockSpec((None, SUBLANES, W), lambda b, i: (b, jnp.minimum((i + 1) * hb, nh - 1), 0)),
            full(mu_p), full(seg), full(w2_p), full(a2_p), full(g2_p), full(vecs),
        ],
        out_specs=[out] * 11,
        out_shape=[jax.ShapeDtypeStruct((B, T, GROUP_W), F32)] * 11,
        compiler_params=_cparams(("parallel", "parallel")),
        name="rwkv_features",
    )(ub, ub, ub, mu_p, seg, w2_p, a2_p, g2_p, vecs)


def _rwkv_chunk_kernel(r_ref, v_ref, kk_ref, lwf_ref, lwb_ref, kdf_ref, kdb_ref, af_ref, ab_ref,
                       cum_ref, tri_ref, hm_ref,
                       rh_out, y1_out, mp_out, np_out, pc_out, *, nsub):
    C = CHUNK
    hm = hm_ref[...]
    hm_b = hm.astype(BF16)
    eye_w = tri_ref[2]
    lw_refs, kd_refs, a_refs = (lwf_ref, lwb_ref), (kdf_ref, kdb_ref), (af_ref, ab_ref)

    def diag(t):
        tb = t.astype(BF16)
        return jnp.concatenate([tb, tb, tb, tb], axis=0) * hm_b

    def rows2(a, b):
        return jnp.concatenate([a, b], axis=0).astype(BF16)

    chains = [(ci, d) for ci in range(nsub) for d in range(2)]
    n = len(chains)
    rng = range(n)
    rows = [slice(ci * C, (ci + 1) * C) for ci, _ in chains]

    at, rt, vv, bh, kh, b_d, k_d, e_end = ([None] * n for _ in range(8))
    for i, (ci, d) in enumerate(chains):
        rs = rows[i]
        r, kk = r_ref[rs, :], kk_ref[rs, :]
        lw, kd, a = lw_refs[d][rs, :], kd_refs[d][rs, :], a_refs[d][rs, :]
        cl = _dot_exact_lhs(cum_ref[d], lw)
        last = C - 1 if d == 0 else 0
        e_end[i] = jnp.exp(cl[last:last + 1, :])
        e_neg = jnp.exp(-cl)
        bt = kk * a * e_neg
        kt = kd * e_neg
        vv[i] = v_ref[rs, :]
        at[i] = -kk * jnp.exp(cl - lw)
        rt[i] = r * jnp.exp(cl)
        bh[i], kh[i] = bt * e_end[i], kt * e_end[i]
        b_d[i], k_d[i] = diag(bt), diag(kt)
    a_d = [diag(at[i]) for i in rng]
    v_d = [diag(vv[i]) for i in rng]

    strict = [tri_ref[d] for _, d in chains]
    ar = [rows2(at[i], rt[i]) for i in rng]
    gb = [_dot_nt(ar[i], b_d[i]) for i in rng]
    gk = [_dot_nt(ar[i], k_d[i]) for i in rng]
    l_ab = [gb[i][:C] * strict[i] for i in rng]
    w_rb = [(gb[i][C:] * (strict[i] + eye_w)).astype(BF16) for i in rng]
    w_akrk = [rows2(gk[i][:C] * strict[i], gk[i][C:] * (strict[i] + eye_w)) for i in rng]

    pw = [_dot(l_ab[i].astype(BF16), diag(l_ab[i])) for i in rng]
    tm = [eye_w + l_ab[i] for i in rng]
    for _ in range(4):
        sq = [_dot(rows2(pw[i], tm[i]), diag(pw[i])) for i in rng]
        tm = [tm[i] + sq[i][C:] for i in rng]
        pw = [sq[i][:C] for i in rng]
    tm = [(tm[i] + _dot(tm[i].astype(BF16), diag(pw[i]))).astype(BF16) for i in rng]

    wt = [_dot(tm[i], a_d[i]) for i in rng]
    xy = [_dot(w_akrk[i], v_d[i]) for i in rng]
    u0 = [_dot(tm[i], diag(xy[i][:C])) for i in rng]
    rh = [rt[i] + _dot(w_rb[i], diag(wt[i])) for i in rng]
    y1 = [_dot(w_rb[i], diag(u0[i])) + xy[i][C:] for i in rng]
    for i, (ci, d) in enumerate(chains):
        rh_out[d, rows[i], :] = rh[i].astype(BF16)
        y1_out[d, rows[i], :] = y1[i]
        mp_out[d, ci] = (_dot_tn(wt[i].astype(BF16), bh[i].astype(BF16)) * hm).astype(BF16)
        np_bd = _dot_tn(rows2(u0[i], vv[i]), rows2(bh[i], kh[i])) * hm
        np_out[d, ci] = functools.reduce(
            jnp.add, [np_bd[h * HEAD_DIM:(h + 1) * HEAD_DIM] for h in range(4)])
        pc_out[d, ci] = jnp.broadcast_to(e_end[i], (SUBLANES, GROUP_W))


def _rwkv_chunks(feats, cum, tri, hm):
    r = feats[0]
    B, T, G = r.shape
    nch = T // CHUNK
    nsub = 4 if nch % 4 == 0 else 1
    tb = nsub * CHUNK
    tok = pl.BlockSpec((None, tb, G), lambda b, c: (b, c, 0))
    full = lambda a: pl.BlockSpec(a.shape, lambda b, c: (0,) * a.ndim)
    return pl.pallas_call(
        functools.partial(_rwkv_chunk_kernel, nsub=nsub),
        grid=(B, nch // nsub),
        in_specs=[tok] * 9 + [full(cum), full(tri), full(hm)],
        out_specs=[
            pl.BlockSpec((None, 2, tb, G), lambda b, c: (b, 0, c, 0)),
            pl.BlockSpec((None, 2, tb, G), lambda b, c: (b, 0, c, 0)),
            pl.BlockSpec((None, 2, nsub, G, G), lambda b, c: (b, 0, c, 0, 0)),
            pl.BlockSpec((None, 2, nsub, HEAD_DIM, G), lambda b, c: (b, 0, c, 0, 0)),
            pl.BlockSpec((None, 2, nsub, SUBLANES, G), lambda b, c: (b, 0, c, 0, 0)),
        ],
        out_shape=[
            jax.ShapeDtypeStruct((B, 2, T, G), BF16),
            jax.ShapeDtypeStruct((B, 2, T, G), F32),
            jax.ShapeDtypeStruct((B, 2, nch, G, G), BF16),
            jax.ShapeDtypeStruct((B, 2, nch, HEAD_DIM, G), F32),
            jax.ShapeDtypeStruct((B, 2, nch, SUBLANES, G), F32),
        ],
        compiler_params=_cparams(("parallel", "parallel")),
        name="rwkv_chunk_transitions",
    )(*feats, cum, tri, hm)


def _rwkv_scan_kernel(s0_ref, hm_ref, rhf_ref, y1f_ref, mpf_ref, npf_ref, pcf_ref,
                      rhb_ref, y1b_ref, mpb_ref, npb_ref, pcb_ref,
                      yf_out, yb_out, s_out, *, batch, nsub):
    c = pl.program_id(0)

    @pl.when(c == 0)
    def _():
        s_out[...] = s0_ref[...]

    hm_b = hm_ref[...].astype(BF16)
    dirs = ((rhf_ref, y1f_ref, mpf_ref, npf_ref, pcf_ref, yf_out),
            (rhb_ref, y1b_ref, mpb_ref, npb_ref, pcb_ref, yb_out))
    chains = [(b, d) for b in range(batch) for d in range(2)]
    s = [s_out[b, d] for b, d in chains]
    for step in range(nsub):
        s_b = [t.astype(BF16) for t in s]
        s_d = [jnp.concatenate([t, t, t, t], axis=0) * hm_b for t in s_b]
        for i, (b, d) in enumerate(chains):
            rh_ref, y1_ref, mp_ref, np_ref, pc_ref, y_out = dirs[d]
            ci = step if d == 0 else nsub - 1 - step
            rows = slice(ci * CHUNK, (ci + 1) * CHUNK)
            y_out[b, rows, :] = _dot_nt(rh_ref[b, rows, :], s_d[i]) + y1_ref[b, rows, :]
            s[i] = s[i] * pc_ref[b, ci, 0:1, :] + _dot(s_b[i], mp_ref[b, ci]) + np_ref[b, ci]
    for i, (b, d) in enumerate(chains):
        s_out[b, d] = s[i]


def _rwkv_scan(s0, hm, rh, y1, mp, npm, pc):
    B, _, T, G = rh.shape
    nch = T // CHUNK
    nsub = 4 if nch % 4 == 0 else 1
    nblk = nch // nsub
    tb = nsub * CHUNK
    fwd = lambda c: c
    bwd = lambda c: nblk - 1 - c

    def specs(order, d):
        return [
            pl.BlockSpec((B, None, tb, G), lambda c: (0, d, order(c), 0)),
            pl.BlockSpec((B, None, tb, G), lambda c: (0, d, order(c), 0)),
            pl.BlockSpec((B, None, nsub, G, G), lambda c: (0, d, order(c), 0, 0)),
            pl.BlockSpec((B, None, nsub, HEAD_DIM, G), lambda c: (0, d, order(c), 0, 0)),
            pl.BlockSpec((B, None, nsub, SUBLANES, G), lambda c: (0, d, order(c), 0, 0)),
        ]

    return pl.pallas_call(
        functools.partial(_rwkv_scan_kernel, batch=B, nsub=nsub),
        grid=(nblk,),
        in_specs=[pl.BlockSpec(s0.shape, lambda c: (0, 0, 0, 0)), pl.BlockSpec(hm.shape, lambda c: (0, 0))]
        + specs(fwd, 0) + specs(bwd, 1),
        out_specs=[
            pl.BlockSpec((B, tb, G), lambda c: (0, fwd(c), 0)),
            pl.BlockSpec((B, tb, G), lambda c: (0, bwd(c), 0)),
            pl.BlockSpec(s0.shape, lambda c: (0, 0, 0, 0)),
        ],
        out_shape=[
            jax.ShapeDtypeStruct((B, T, G), F32),
            jax.ShapeDtypeStruct((B, T, G), F32),
            jax.ShapeDtypeStruct(s0.shape, F32),
        ],
        compiler_params=_cparams(("arbitrary",)),
        name="rwkv_state_scan",
    )(s0, hm, rh, y1, mp, npm, pc, rh, y1, mp, npm, pc)


def _seg_matrix(width):
    i = np.arange(width) // HEAD_DIM
    return jnp.asarray((i[:, None] == i[None, :]).astype(np.float32), BF16)


def _scan_constants():
    C = CHUNK
    t = np.arange(4 * C)
    head, tok = t // C, t % C
    i = np.arange(C)
    lower = i[:, None] > tok[None, :]
    upper = i[:, None] < tok[None, :]
    eye = i[:, None] == tok[None, :]
    tri = jnp.asarray(np.stack([lower, upper, eye]).astype(np.float32))
    cum = jnp.asarray(np.stack([i[:, None] >= i[None, :], i[:, None] <= i[None, :]]).astype(np.float32), BF16)
    hm = jnp.asarray((head[:, None] == (np.arange(GROUP_W) // HEAD_DIM)[None, :]).astype(np.float32))
    return cum, tri, hm


def _pad_vec(v, lo, total):
    return jnp.zeros((total,), F32).at[lo:lo + v.shape[0]].set(v)


def kernel(x, c, ctx, c_ctx, w_mod, b_mod, norm_ffn1, ffn1_w_gu, ffn1_w_down, norm_mix, w_in, w_out, mla_q_norm, mla_kv_norm, mla_w_uq, mla_w_ukv, mla_qn, mla_kn, rwkv_shift, rwkv_w0, rwkv_w2, rwkv_a0, rwkv_a2, rwkv_g2, rwkv_k_k, rwkv_k_a, rwkv_r_k, rwkv_lnx_g, rwkv_lnx_b, na_qn, na_kn, na_rpb, gqa_qn, gqa_kn, norm_ffn2, ffn2_w_gu, ffn2_w_down):
    B, S, D = x.shape
    Lc = ctx.shape[1]
    depth = w_mod.shape[0]
    assert B + 1 <= SUBLANES and S % 512 == 0 and Lc % CHUNK == 0 and S % GRID_W == 0

    cvec = jnp.zeros((SUBLANES, D), F32).at[:B].set(c).at[B].set(c_ctx)
    mods = _modulation(cvec, B + 1, w_mod, b_mod).reshape(depth, SUBLANES, 9, D)

    mla_tab = _rope_tables(S // GRID_W, LANES, MLA_NOPE, MLA_ROPE // 2, 1)
    gqa_tab = _rope_tables(S // GRID_W, 4 * HEAD_DIM, 0, HEAD_DIM // 2, 4)
    mla_tab_c = _identity_rope_tables(Lc, LANES)
    gqa_tab_c = _identity_rope_tables(Lc, 4 * HEAD_DIM)

    seg = _seg_matrix(GROUP_W)
    cum, tri, hm = _scan_constants()
    mla_scale = MLA_QK ** -0.5 * LOG2E
    dh_scale = HEAD_DIM ** -0.5 * LOG2E

    w_gu1, w_dn1 = ffn1_w_gu.astype(BF16), ffn1_w_down.astype(BF16)
    w_gu2, w_dn2 = ffn2_w_gu.astype(BF16), ffn2_w_down.astype(BF16)
    w_out_b = w_out.astype(BF16)
    zc = lambda n: jnp.zeros((depth, D, n), BF16)
    wi = w_in.astype(BF16)
    o_b = MLA_COLS
    o_n = o_b + RWKV_COLS
    o_d = o_n + NA_COLS
    w_in_p = jnp.concatenate([
        wi[..., :MLA_Q_RANK + MLA_KV_RANK], zc(64), wi[..., MLA_Q_RANK + MLA_KV_RANK:MLA_COLS], zc(32),
        wi[..., o_b:o_n], zc(UB_W - RWKV_COLS),
        wi[..., o_n:o_d],
        wi[..., o_d:o_d + GQA_COLS]], axis=2)

    h, hc = x, ctx
    for l in range(depth):
        ctx_out = l < depth - 1
        m_lat = mods[l, :B]
        m_ctx = jnp.broadcast_to(mods[l, B], (B, 9, D))


        wuq = mla_w_uq[l].reshape(MLA_Q_RANK, MLA_HEADS, MLA_QK)
        wuq_p = jnp.pad(wuq, ((0, 0), (0, 0), (0, LANES - MLA_QK))).reshape(MLA_Q_RANK, MLA_HEADS * LANES).astype(BF16)
        wukv = mla_w_ukv[l].reshape(MLA_KV_RANK, MLA_HEADS, MLA_NOPE + MLA_V)
        wuk_p = jnp.pad(wukv[..., :MLA_NOPE], ((0, 0), (0, 0), (0, LANES - MLA_NOPE))).reshape(MLA_KV_RANK, MLA_HEADS * LANES).astype(BF16)
        wuv = wukv[..., MLA_NOPE:].reshape(MLA_KV_RANK, MLA_HEADS * MLA_V).astype(BF16)
        mla_small = (mla_q_norm[l].reshape(1, -1), mla_kv_norm[l].reshape(1, -1),
                     _pad_vec(mla_qn[l] * mla_scale, 0, LANES).reshape(1, LANES),
                     _pad_vec(mla_kn[l], 0, LANES).reshape(1, LANES), wuq_p, wuk_p, wuv)

        na_gq = jnp.tile(na_qn[l] * dh_scale, 4).reshape(1, -1)
        na_gk = jnp.tile(na_kn[l], 4).reshape(1, -1)
        gqa_gq = jnp.tile(gqa_qn[l] * dh_scale, 4).reshape(1, -1)
        gqa_gk = jnp.tile(gqa_kn[l], 2).reshape(1, -1)
        na_tb = _na_bias_tables(na_rpb[l])

        mu_p = jnp.zeros((SUBLANES, UB_W), F32).at[:2, :RWKV_COLS].set(rwkv_shift[l])
        def lora_w(w, off):
            return jnp.zeros((GROUP_W, GROUP_W), F32).at[off:off + w.shape[0]].set(w).astype(BF16)

        w2_p = jnp.stack([lora_w(rwkv_w2[l, 0], 0), lora_w(rwkv_w2[l, 1], 32)])
        a2_p = jnp.stack([lora_w(rwkv_a2[l, 0], 64), lora_w(rwkv_a2[l, 1], 96)])
        g2_p = lora_w(rwkv_g2[l], 128)
        feat_vecs = jnp.zeros((SUBLANES, GROUP_W), F32).at[0].set(rwkv_k_k[l]).at[1].set(rwkv_k_a[l]) \
            .at[2].set(rwkv_r_k[l].reshape(-1)).at[3:5].set(rwkv_w0[l]).at[5:7].set(rwkv_a0[l])
        post_vecs = jnp.zeros((SUBLANES, GROUP_W), F32).at[0].set(rwkv_lnx_g[l]).at[1].set(rwkv_lnx_b[l])

        h = _ffn(h, m_lat, norm_ffn1[l], w_gu1, w_dn1, l, 0)
        hc = _ffn(hc, m_ctx, norm_ffn1[l], w_gu1, w_dn1, l, 0)

        ua, ub, un, ud = _inproj(h, m_lat, norm_mix[l], w_in_p, l)
        uca, ucb, ucn, ucd = _inproj(hc, m_ctx, norm_mix[l], w_in_p, l)

        qa, ka, vta = _mla_prep(ua, mla_tab, *mla_small)
        qca, kca, vtca = _mla_prep(uca, mla_tab_c, *mla_small)
        oa = _attention(qa, ka, vta, kca, vtca)

        qn_, kn_, vn_ = _qkv_prep(un, seg, na_gq, na_gk, 4, 4, 4)
        qcn, kcn, vcn, vtcn = _qkv_prep(ucn, seg, na_gq, na_gk, 4, 4, 4, v_cols=True)
        on = _na_attention(qn_, kn_, vn_, kcn, vcn, na_tb)

        qd, kd, vtd = _qkv_prep(ud, seg, gqa_gq, gqa_gk, 4, 2, 2, gqa_tab, v_rows=False, v_cols=True)
        qcd, kcd, vtcd = _qkv_prep(ucd, seg, gqa_gq, gqa_gk, 4, 2, 2, gqa_tab_c, v_rows=False, v_cols=True)
        od = _attention(qd, kd, vtd, kcd, vtcd)

        feats_c = _rwkv_features(ucb, mu_p, seg, w2_p, a2_p, g2_p, feat_vecs)
        feats = _rwkv_features(ub, mu_p, seg, w2_p, a2_p, g2_p, feat_vecs)
        trans_c = _rwkv_chunks(feats_c[:9], cum, tri, hm)
        trans = _rwkv_chunks(feats[:9], cum, tri, hm)
        s0 = jnp.zeros((B, 2, HEAD_DIM, GROUP_W), F32)
        yfc, ybc, s_ctx = _rwkv_scan(s0, hm, *trans_c)
        yf, yb, _ = _rwkv_scan(s_ctx, hm, *trans)

        h = _outproj(h, m_lat, oa, (yf, yb, feats[9], feats[10]), on, od, (True, False, True),
                     seg, post_vecs, w_out_b, l)
        if ctx_out:
            oca = _attention(qca, kca, vtca)
            ocn = _attention(qcn, kcn, vtcn)
            ocd = _attention(qcd, kcd, vtcd)
            hc = _outproj(hc, m_ctx, oca, (yfc, ybc, feats_c[9], feats_c[10]), ocn, ocd,
                          (True, True, True), seg, post_vecs, w_out_b, l)

        h = _ffn(h, m_lat, norm_ffn2[l], w_gu2, w_dn2, l, 6)
        if ctx_out:
            hc = _ffn(hc, m_ctx, norm_ffn2[l], w_gu2, w_dn2, l, 6)
    return h
```
